```python
import math
import jax
import jax.numpy as jnp
from jax import lax
import numpy as np

D_MODEL = 2048
BATCH = 4
SEQ = 2048
DEPTH = 4

GRID_W = 64
CTX_LEN = 256
EPS = 1e-6
N_EVEN = (DEPTH + 1) // 2
N_ODD = DEPTH // 2
FNET_HEADS = 4
FNET_DIM = D_MODEL // 16
FNET_WIDTH = FNET_HEADS * FNET_DIM
S5_WIDTH = D_MODEL - FNET_WIDTH
S5_GROUP = 16
S5_GROUPS = S5_WIDTH // S5_GROUP
S5_STATE = 64
S5_DT_MIN = 0.001
S5_DT_MAX = 0.1
POOL_WINDOWS = (2, 4, 8, 16)
POOL_GROUPS = len(POOL_WINDOWS)
POOL_DIM = D_MODEL // 16
POOL_WIDTH = POOL_GROUPS * POOL_DIM
HEAD_DIM = 128
ATT_WIDTH = D_MODEL - POOL_WIDTH
ATT_HEADS = ATT_WIDTH // HEAD_DIM
KV_HEADS = 4
Q_PER_KV = ATT_HEADS // KV_HEADS
KV_WIDTH = KV_HEADS * HEAD_DIM
KV_OFFSET = POOL_WIDTH + ATT_WIDTH
ODD_IN_WIDTH = KV_OFFSET + 2 * KV_WIDTH
Q_BLOCK = 128
ROPE_THETA = 10000.0
ROPE_FREQS = HEAD_DIM // 4
ATT_SCALE = HEAD_DIM ** -0.5
N_EXPERTS = 32
TOP_K = 4
EXPERT_FF = 512
SWIGLU_LIMIT = 7.0
SWIGLU_ALPHA = 1.702
MOE_BLOCK = 128

kernel_name = 'hybrid_s5_fnet_pool_gqa_moe_dit'

F32 = jnp.float32
C64 = jnp.complex64


def rmsnorm(x, g):
    xf = x.astype(F32)
    y = xf * lax.rsqrt(jnp.mean(xf * xf, axis=-1, keepdims=True) + EPS)
    return (y * g.astype(F32)).astype(x.dtype)


def adaln(cond, w, b):
    m = jnp.dot(jax.nn.silu(cond), w) + b
    return [t[:, None, :] for t in jnp.split(m, 6, axis=-1)]


def modulate(h, shift, scale):
    return h * (1 + scale) + shift


def s5_discretize(lam_re, lam_im, log_dt, b_re, b_im, c_re, c_im):
    lam = lax.complex(lam_re.astype(F32), lam_im.astype(F32))
    dt = jnp.exp(log_dt.astype(F32))[..., None]
    lam_bar = jnp.exp(lam * dt)
    b = lax.complex(b_re.astype(F32), b_im.astype(F32))
    b_bar = ((lam_bar - 1.0) / lam)[..., None] * b
    c_mat = lax.complex(c_re.astype(F32), c_im.astype(F32))
    return lam_bar, b_bar, c_mat


def scan_combine(left, right):
    a_l, b_l = left
    a_r, b_r = right
    return a_l * a_r, a_r * b_l + b_r


def diag_scan(lam_bar, bu, h0):
    if h0 is not None:
        bu = bu.at[:, 0].add(lam_bar * h0)
    a = jnp.broadcast_to(lam_bar, bu.shape)
    return lax.associative_scan(scan_combine, (a, bu), axis=1)[1]


def s5_direction(u_lat, u_ctx, lam_bar, b_bar, c_mat, reverse, need_ctx_out):
    flip = (lambda z: jnp.flip(z, axis=1)) if reverse else (lambda z: z)
    bu_ctx = jnp.einsum('btgc,gpc->btgp', flip(u_ctx).astype(C64), b_bar)
    h_ctx = diag_scan(lam_bar, bu_ctx, None)
    bu_lat = jnp.einsum('btgc,gpc->btgp', flip(u_lat).astype(C64), b_bar)
    h_lat = diag_scan(lam_bar, bu_lat, h_ctx[:, -1])
    y_lat = flip(jnp.einsum('btgp,gcp->btgc', h_lat, c_mat).real)
    y_ctx = flip(jnp.einsum('btgp,gcp->btgc', h_ctx, c_mat).real) if need_ctx_out else None
    return y_lat, y_ctx


def s5_output(y_ssm, u, d_skip, glu_w, glu_b):
    y = y_ssm + d_skip.astype(F32) * u.astype(F32)
    g = jax.nn.gelu(y).astype(u.dtype)
    return g * jax.nn.sigmoid(g @ glu_w + glu_b)


def fourier_mix(f, fnet_w):
    spec = jnp.fft.fft2(f.astype(F32), axes=(1, 3), norm='ortho').real.astype(f.dtype)
    return jnp.einsum('bthd,hde->bthe', spec, fnet_w)


def s5_fourier_mixer(h_lat, h_ctx, w_in, w_out, lam_re, lam_im, log_dt, b_re, b_im, c_re, c_im,
                     d_skip, glu_w, glu_b, fnet_w, need_ctx_out):
    B, S, _ = h_lat.shape
    L = h_ctx.shape[1]
    lam_bar, b_bar, c_mat = s5_discretize(lam_re, lam_im, log_dt, b_re, b_im, c_re, c_im)
    p_lat = h_lat @ w_in
    u_lat, f_lat = p_lat[..., :S5_WIDTH], p_lat[..., S5_WIDTH:]
    if need_ctx_out:
        p_ctx = h_ctx @ w_in
        u_ctx, f_ctx = p_ctx[..., :S5_WIDTH], p_ctx[..., S5_WIDTH:]
    else:
        u_ctx = h_ctx @ w_in[:, :S5_WIDTH]
    ul = u_lat.astype(F32).reshape(B, S, S5_GROUPS, S5_GROUP)
    uc = u_ctx.astype(F32).reshape(B, L, S5_GROUPS, S5_GROUP)
    yf_lat, yf_ctx = s5_direction(ul, uc, lam_bar[0], b_bar[0], c_mat[0], False, need_ctx_out)
    yb_lat, yb_ctx = s5_direction(ul, uc, lam_bar[1], b_bar[1], c_mat[1], True, need_ctx_out)

    def merge(y_ssm, u, f, n):
        a = s5_output(y_ssm.reshape(B, n, S5_WIDTH), u, d_skip, glu_w, glu_b)
        b = fourier_mix(f.reshape(B, n, FNET_HEADS, FNET_DIM), fnet_w).reshape(B, n, FNET_WIDTH)
        return jnp.concatenate([a, b], axis=-1) @ w_out

    out_lat = merge(yf_lat + yb_lat, u_lat, f_lat, S)
    out_ctx = merge(yf_ctx + yb_ctx, u_ctx, f_ctx, L) if need_ctx_out else None
    return out_lat, out_ctx


def centred_pool_minus_self(v, window):
    n = v.shape[1]
    vf = v.astype(F32)
    cs = jnp.concatenate([jnp.zeros_like(vf[:, :1]), lax.cumsum(vf, axis=1)], axis=1)
    t = jnp.arange(n)
    lo = jnp.clip(t - window // 2, 0, n)
    hi = jnp.clip(t + window - window // 2, 0, n)
    mean = (jnp.take(cs, hi, axis=1) - jnp.take(cs, lo, axis=1)) / (hi - lo).astype(F32)[None, :, None]
    return mean - vf


def pool_mix(v, pool_w, pool_scale):
    B, T, _ = v.shape
    groups = jnp.split(v, POOL_GROUPS, axis=-1)
    pooled = jnp.stack([centred_pool_minus_self(gv, w) for gv, w in zip(groups, POOL_WINDOWS)], axis=2)
    y = jnp.einsum('btgd,gde->btge', pooled.astype(v.dtype), pool_w).reshape(B, T, POOL_WIDTH)
    return y * pool_scale


def axial_rope_tables(n_tok):
    rows = n_tok // GRID_W
    row = jnp.repeat(jnp.arange(rows), GRID_W)
    col = jnp.tile(jnp.arange(GRID_W), rows)
    pos = jnp.stack([row, col], axis=-1).astype(F32)
    inv = jnp.power(ROPE_THETA, -jnp.arange(ROPE_FREQS, dtype=F32) / ROPE_FREQS)
    ang = pos[:, :, None] * inv
    return jnp.cos(ang), jnp.sin(ang)


def apply_axial_rope(x, cos, sin):
    B, T, H, Dh = x.shape
    xr = x.astype(F32).reshape(B, T, H, 2, 2, ROPE_FREQS)
    x1, x2 = xr[..., 0, :], xr[..., 1, :]
    c = cos[None, :, None]
    s = sin[None, :, None]
    out = jnp.stack([x1 * c - x2 * s, x2 * c + x1 * s], axis=-2)
    return out.reshape(B, T, H, Dh).astype(x.dtype)


def latent_attention(q, k, v, k_ctx, v_ctx):
    B, S, _, _ = q.shape
    k_all = jnp.concatenate([k_ctx, k], axis=1)
    v_all = jnp.concatenate([v_ctx, v], axis=1)
    n_blk = S // Q_BLOCK
    qb = q.reshape(B, n_blk, Q_BLOCK, KV_HEADS, Q_PER_KV, HEAD_DIM).transpose(1, 0, 2, 3, 4, 5)

    def one_block(q_blk):
        s = jnp.einsum('bqhgd,bkhd->bhgqk', q_blk, k_all, preferred_element_type=F32) * ATT_SCALE
        p = jax.nn.softmax(s, axis=-1)
        return jnp.einsum('bhgqk,bkhd->bqhgd', p.astype(v_all.dtype), v_all, preferred_element_type=F32)

    o = lax.map(one_block, qb)
    return o.transpose(1, 0, 2, 3, 4, 5).reshape(B, S, ATT_WIDTH).astype(q.dtype)


def context_attention(q, k, v):
    B, L, _, _ = q.shape
    qg = q.reshape(B, L, KV_HEADS, Q_PER_KV, HEAD_DIM)
    s = jnp.einsum('bqhgd,bkhd->bhgqk', qg, k, preferred_element_type=F32) * ATT_SCALE
    p = jax.nn.softmax(s, axis=-1)
    o = jnp.einsum('bhgqk,bkhd->bqhgd', p.astype(v.dtype), v, preferred_element_type=F32)
    return o.reshape(B, L, ATT_WIDTH).astype(q.dtype)


def pool_attention_mixer(h_lat, h_ctx, w_in, w_out, pool_w, pool_scale, q_gain, k_gain, cos, sin, need_ctx_out):
    B, S, _ = h_lat.shape
    L = h_ctx.shape[1]
    p_lat = h_lat @ w_in
    pool_lat = p_lat[..., :POOL_WIDTH]
    q_lat = p_lat[..., POOL_WIDTH:KV_OFFSET].reshape(B, S, ATT_HEADS, HEAD_DIM)
    k_lat = p_lat[..., KV_OFFSET:KV_OFFSET + KV_WIDTH].reshape(B, S, KV_HEADS, HEAD_DIM)
    v_lat = p_lat[..., KV_OFFSET + KV_WIDTH:].reshape(B, S, KV_HEADS, HEAD_DIM)
    q_lat = apply_axial_rope(rmsnorm(q_lat, q_gain), cos, sin)
    k_lat = apply_axial_rope(rmsnorm(k_lat, k_gain), cos, sin)
    kv_ctx = h_ctx @ w_in[:, KV_OFFSET:]
    k_ctx = rmsnorm(kv_ctx[..., :KV_WIDTH].reshape(B, L, KV_HEADS, HEAD_DIM), k_gain)
    v_ctx = kv_ctx[..., KV_WIDTH:].reshape(B, L, KV_HEADS, HEAD_DIM)
    att_lat = latent_attention(q_lat, k_lat, v_lat, k_ctx, v_ctx)
    out_lat = jnp.concatenate([pool_mix(pool_lat, pool_w, pool_scale), att_lat], axis=-1) @ w_out
    if not need_ctx_out:
        return out_lat, None
    pq_ctx = h_ctx @ w_in[:, :KV_OFFSET]
    q_ctx = rmsnorm(pq_ctx[..., POOL_WIDTH:].reshape(B, L, ATT_HEADS, HEAD_DIM), q_gain)
    att_ctx = context_attention(q_ctx, k_ctx, v_ctx)
    out_ctx = jnp.concatenate([pool_mix(pq_ctx[..., :POOL_WIDTH], pool_w, pool_scale), att_ctx], axis=-1) @ w_out
    return out_lat, out_ctx


def moe_ffn(h, w_router, b_router, w_gate, b_gate, w_up, b_up, w_down, b_down):
    n_tok, d = h.shape
    logits = jnp.dot(h, w_router, preferred_element_type=F32) + b_router.astype(F32)
    top_val, top_idx = lax.top_k(logits, TOP_K)
    top_w = jax.nn.softmax(top_val, axis=-1)
    n_assign = n_tok * TOP_K
    cap = -(-(n_assign + N_EXPERTS * (MOE_BLOCK - 1)) // MOE_BLOCK) * MOE_BLOCK
    n_blocks = cap // MOE_BLOCK
    flat_e = top_idx.reshape(-1)
    flat_tok = jnp.repeat(jnp.arange(n_tok, dtype=jnp.int32), TOP_K)
    flat_w = top_w.reshape(-1)
    order = jnp.argsort(flat_e)
    sorted_e = flat_e[order]
    counts = jnp.bincount(flat_e, length=N_EXPERTS)
    start = jnp.cumsum(counts) - counts
    padded = (counts + MOE_BLOCK - 1) // MOE_BLOCK * MOE_BLOCK
    pend = jnp.cumsum(padded)
    pstart = pend - padded
    dest = pstart[sorted_e] + (jnp.arange(n_assign) - start[sorted_e])
    tok_buf = jnp.full((cap,), n_tok, jnp.int32).at[dest].set(flat_tok[order])
    w_buf = jnp.zeros((cap,), F32).at[dest].set(flat_w[order])
    blk_e = jnp.clip(jnp.searchsorted(pend, jnp.arange(n_blocks) * MOE_BLOCK, side='right'), 0, N_EXPERTS - 1)
    h_pad = jnp.concatenate([h, jnp.zeros((1, d), h.dtype)], axis=0)

    def expert_block(args):
        tok, wgt, e = args
        xb = h_pad[tok]
        g = jnp.minimum(xb @ w_gate[e] + b_gate[e], SWIGLU_LIMIT)
        u = jnp.clip(xb @ w_up[e] + b_up[e], -SWIGLU_LIMIT, SWIGLU_LIMIT)
        a = g * jax.nn.sigmoid(SWIGLU_ALPHA * g) * (u + 1)
        return ((a @ w_down[e] + b_down[e]) * wgt[:, None]).astype(h.dtype)

    rows = lax.map(expert_block, (tok_buf.reshape(n_blocks, MOE_BLOCK), w_buf.reshape(n_blocks, MOE_BLOCK), blk_e))
    out = jnp.zeros_like(h_pad).at[tok_buf].add(rows.reshape(cap, d))
    return out[:n_tok]


def setup_inputs(seed: int = 0) -> dict:
    key = jax.random.key(seed)
    ks = iter(jax.random.split(key, 64))
    D, G, P, C, E, F = D_MODEL, S5_GROUPS, S5_STATE, S5_GROUP, N_EXPERTS, EXPERT_FF

    def nrm(shape, scale):
        return jax.random.normal(next(ks), shape, F32) * scale

    return {
        'x': nrm((BATCH, SEQ, D), 1.0),
        'c': nrm((BATCH, D), 1.0),
        'ctx': nrm((BATCH, CTX_LEN, D), 1.0),
        'c_ctx': nrm((D,), 1.0),
        'ada_w': nrm((DEPTH, D, 6 * D), 0.5 * D ** -0.5),
        'ada_b': nrm((DEPTH, 6 * D), 0.02),
        'norm_mix_g': 1.0 + nrm((DEPTH, D), 0.02),
        'norm_ffn_g': 1.0 + nrm((DEPTH, D), 0.02),
        'ev_w_in': nrm((N_EVEN, D, D), D ** -0.5),
        'ev_w_out': nrm((N_EVEN, D, D), D ** -0.5),
        's5_lam_re': -0.5 + nrm((N_EVEN, 2, G, P), 0.01),
        's5_lam_im': math.pi * jnp.arange(P, dtype=F32) + nrm((N_EVEN, 2, G, P), 0.01),
        's5_log_dt': jax.random.uniform(next(ks), (N_EVEN, 2, G), F32, math.log(S5_DT_MIN), math.log(S5_DT_MAX)),
        's5_b_re': nrm((N_EVEN, 2, G, P, C), (2 * C) ** -0.5),
        's5_b_im': nrm((N_EVEN, 2, G, P, C), (2 * C) ** -0.5),
        's5_c_re': nrm((N_EVEN, 2, G, C, P), P ** -0.5),
        's5_c_im': nrm((N_EVEN, 2, G, C, P), P ** -0.5),
        's5_d': nrm((N_EVEN, S5_WIDTH), 1.0),
        's5_glu_w': nrm((N_EVEN, S5_WIDTH, S5_WIDTH), S5_WIDTH ** -0.5),
        's5_glu_b': nrm((N_EVEN, S5_WIDTH), 0.02),
        'fnet_w': nrm((N_EVEN, FNET_HEADS, FNET_DIM, FNET_DIM), FNET_DIM ** -0.5),
        'od_w_in': nrm((N_ODD, D, ODD_IN_WIDTH), D ** -0.5),
        'od_w_out': nrm((N_ODD, D, D), D ** -0.5),
        'pool_w': nrm((N_ODD, POOL_GROUPS, POOL_DIM, POOL_DIM), POOL_DIM ** -0.5),
        'pool_scale': 1.0 + nrm((N_ODD, POOL_WIDTH), 0.1),
        'q_gain': 1.0 + nrm((N_ODD, HEAD_DIM), 0.02),
        'k_gain': 1.0 + nrm((N_ODD, HEAD_DIM), 0.02),
        'router_w': nrm((DEPTH, D, E), D ** -0.5),
        'router_b': nrm((DEPTH, E), 0.01),
        'exp_w_gate': nrm((DEPTH, E, D, F), D ** -0.5),
        'exp_b_gate': nrm((DEPTH, E, F), 0.02),
        'exp_w_up': nrm((DEPTH, E, D, F), D ** -0.5),
        'exp_b_up': nrm((DEPTH, E, F), 0.02),
        'exp_w_down': nrm((DEPTH, E, F, D), F ** -0.5),
        'exp_b_down': nrm((DEPTH, E, D), 0.02),
        'final_g': 1.0 + nrm((D,), 0.02),
    }


def reference(x, c, ctx, c_ctx, ada_w, ada_b, norm_mix_g, norm_ffn_g, ev_w_in, ev_w_out,
              s5_lam_re, s5_lam_im, s5_log_dt, s5_b_re, s5_b_im, s5_c_re, s5_c_im, s5_d, s5_glu_w, s5_glu_b,
              fnet_w, od_w_in, od_w_out, pool_w, pool_scale, q_gain, k_gain, router_w, router_b,
              exp_w_gate, exp_b_gate, exp_w_up, exp_b_up, exp_w_down, exp_b_down, final_g):
    B, S, D = x.shape
    cos, sin = axial_rope_tables(S)
    for layer in range(DEPTH):
        need_ctx_out = layer < DEPTH - 1
        i = layer // 2
        sh1, sc1, g1, sh2, sc2, g2 = adaln(c, ada_w[layer], ada_b[layer])
        csh1, csc1, cg1, csh2, csc2, cg2 = adaln(c_ctx[None], ada_w[layer], ada_b[layer])
        h_lat = modulate(rmsnorm(x, norm_mix_g[layer]), sh1, sc1)
        h_ctx = modulate(rmsnorm(ctx, norm_mix_g[layer]), csh1, csc1)
        if layer % 2 == 0:
            y_lat, y_ctx = s5_fourier_mixer(h_lat, h_ctx, ev_w_in[i], ev_w_out[i], s5_lam_re[i], s5_lam_im[i],
                                            s5_log_dt[i], s5_b_re[i], s5_b_im[i], s5_c_re[i], s5_c_im[i],
                                            s5_d[i], s5_glu_w[i], s5_glu_b[i], fnet_w[i], need_ctx_out)
        else:
            y_lat, y_ctx = pool_attention_mixer(h_lat, h_ctx, od_w_in[i], od_w_out[i], pool_w[i], pool_scale[i],
                                                q_gain[i], k_gain[i], cos, sin, need_ctx_out)
        x = x + g1 * y_lat
        h_lat = modulate(rmsnorm(x, norm_ffn_g[layer]), sh2, sc2)
        moe_args = (router_w[layer], router_b[layer], exp_w_gate[layer], exp_b_gate[layer],
                    exp_w_up[layer], exp_b_up[layer], exp_w_down[layer], exp_b_down[layer])
        if need_ctx_out:
            ctx = ctx + cg1 * y_ctx
            h_ctx = modulate(rmsnorm(ctx, norm_ffn_g[layer]), csh2, csc2)
            n_lat = B * S
            out = moe_ffn(jnp.concatenate([h_lat.reshape(-1, D), h_ctx.reshape(-1, D)], axis=0), *moe_args)
            x = x + g2 * out[:n_lat].reshape(x.shape)
            ctx = ctx + cg2 * out[n_lat:].reshape(ctx.shape)
        else:
            x = x + g2 * moe_ffn(h_lat.reshape(-1, D), *moe_args).reshape(x.shape)
    return rmsnorm(x, final_g)
```

```python
import functools
import math

import jax
import jax.numpy as jnp
from jax import lax
from jax.experimental import pallas as pl
from jax.experimental.pallas import tpu as pltpu

F32 = jnp.float32
BF16 = jnp.bfloat16
EPS = 1e-6

GRID_W = 64
FNET_HEADS = 4
S5_GROUP = 16
S5_STATE = 64
POOL_WINDOWS = (2, 4, 8, 16)
HEAD_DIM = 128
KV_HEADS = 4
ROPE_THETA = 10000.0
TOP_K = 4
SWIGLU_LIMIT = 7.0
SWIGLU_ALPHA = 1.702

LANES = 128
S5_BLOCK_GROUPS = LANES // S5_GROUP
VMEM_LIMIT = 48 * 1024 * 1024
MOE_ROWS = 256


def _cparams(*sem):
    return pltpu.CompilerParams(dimension_semantics=sem, vmem_limit_bytes=VMEM_LIMIT)


def _pick(n, prefs):
    for p in prefs:
        if n % p == 0:
            return p
    return n


def _adaln_kernel(c_ref, w_ref, b_ref, o_ref):
    c = c_ref[...]
    a = (c * jax.nn.sigmoid(c)).astype(BF16)
    o_ref[...] = jnp.dot(a, w_ref[...].astype(BF16), preferred_element_type=F32) + b_ref[...]


def adaln_all(cond8, ada_w, ada_b):
    depth, d, n = ada_w.shape
    tn = _pick(n, (1024, 512, 256, 128))
    return pl.pallas_call(
        _adaln_kernel,
        grid=(depth, n // tn),
        in_specs=[pl.BlockSpec((8, d), lambda l, j: (0, 0)),
                  pl.BlockSpec((None, d, tn), lambda l, j: (l, 0, j)),
                  pl.BlockSpec((None, 1, tn), lambda l, j: (l, 0, j))],
        out_specs=pl.BlockSpec((None, 8, tn), lambda l, j: (l, 0, j)),
        out_shape=jax.ShapeDtypeStruct((depth, 8, n), F32),
        compiler_params=_cparams("arbitrary", "arbitrary"),
        name="adaln",
    )(cond8, ada_w, ada_b.reshape(depth, 1, n))


def _norm_mod_kernel(x_ref, g_ref, sh_ref, sc_ref, o_ref):
    x = x_ref[...]
    y = x * lax.rsqrt(jnp.mean(x * x, axis=-1, keepdims=True) + EPS) * g_ref[...]
    o_ref[...] = (y * (1.0 + sc_ref[...]) + sh_ref[...]).astype(o_ref.dtype)


def _norm_kernel(x_ref, g_ref, o_ref):
    x = x_ref[...]
    o_ref[...] = (x * lax.rsqrt(jnp.mean(x * x, axis=-1, keepdims=True) + EPS) * g_ref[...]).astype(o_ref.dtype)


class Stream:
    def __init__(self, batch, seq, ctx_len):
        self.b, self.s, self.l = batch, seq, ctx_len
        self.n_lat = batch * seq
        self.n_ctx = batch * ctx_len
        self.nt = self.n_lat + self.n_ctx
        self.tm = _pick(math.gcd(seq, self.n_ctx), (512, 256, 128, 64, 32, 16))

    def mod_row(self, tm):
        s, b = self.s, self.b
        return lambda i: jnp.minimum((i * tm) // s, b)


def _mod_spec(st, tm, k, d, grid_rank, row_axis):
    mr = st.mod_row(tm)
    if grid_rank == 1:
        return pl.BlockSpec((None, None, 1, d), lambda i: (k, mr(i), 0, 0))
    if row_axis == 1:
        return pl.BlockSpec((None, None, 1, d), lambda j, i: (k, mr(i), 0, j))
    raise ValueError


def norm_mod(st, x, g, mod, k_shift, k_scale, out_dtype=BF16):
    nt, d = x.shape
    tm = _pick(st.tm, (256, 128, 64, 32, 16))
    return pl.pallas_call(
        _norm_mod_kernel,
        grid=(nt // tm,),
        in_specs=[pl.BlockSpec((tm, d), lambda i: (i, 0)),
                  pl.BlockSpec((1, d), lambda i: (0, 0)),
                  _mod_spec(st, tm, k_shift, d, 1, 0),
                  _mod_spec(st, tm, k_scale, d, 1, 0)],
        out_specs=pl.BlockSpec((tm, d), lambda i: (i, 0)),
        out_shape=jax.ShapeDtypeStruct((nt, d), out_dtype),
        compiler_params=_cparams("arbitrary"),
        name="norm_mod",
    )(x, g.reshape(1, d), mod, mod)


def final_norm(x, g):
    n, d = x.shape
    tm = _pick(n, (256, 128, 64, 32, 16, 8))
    return pl.pallas_call(
        _norm_kernel,
        grid=(n // tm,),
        in_specs=[pl.BlockSpec((tm, d), lambda i: (i, 0)), pl.BlockSpec((1, d), lambda i: (0, 0))],
        out_specs=pl.BlockSpec((tm, d), lambda i: (i, 0)),
        out_shape=jax.ShapeDtypeStruct((n, d), F32),
        compiler_params=_cparams("arbitrary"),
        name="final_norm",
    )(x, g.reshape(1, d))


def _mm_kernel(*refs, n_a, k_offs, mode):
    a_refs = refs[:n_a]
    w_ref = refs[n_a]
    rest = refs[n_a + 1:]
    wbf_ref = rest[-1]
    o_ref = rest[-2]

    @pl.when(pl.program_id(1) == 0)
    def _():
        wbf_ref[...] = w_ref[...].astype(BF16)

    acc = None
    for a_ref, (k0, k1) in zip(a_refs, k_offs):
        part = jnp.dot(a_ref[...], wbf_ref[k0:k1, :], preferred_element_type=F32)
        acc = part if acc is None else acc + part
    if mode == "plain":
        o_ref[...] = acc.astype(o_ref.dtype)
    elif mode == "resid":
        x_ref, gate_ref = rest[0], rest[1]
        o_ref[...] = x_ref[...] + gate_ref[...] * acc
    elif mode == "glu":
        g_ref, b_ref = rest[0], rest[1]
        o_ref[...] = (g_ref[...].astype(F32) * jax.nn.sigmoid(acc + b_ref[...])).astype(o_ref.dtype)
    else:
        raise ValueError(mode)


def matmul(a_parts, w, mode="plain", out_dtype=BF16, extra=(), st=None, tn_prefs=(1024, 768, 512, 256, 128)):
    m = a_parts[0].shape[0]
    k, n = w.shape
    tm = _pick(m if st is None else st.tm, (512, 256, 128, 64, 32, 16, 8))
    tn = _pick(n, tn_prefs)
    k_offs, off = [], 0
    in_specs, args = [], []
    for a in a_parts:
        ka = a.shape[1]
        k_offs.append((off, off + ka))
        off += ka
        in_specs.append(pl.BlockSpec((tm, ka), lambda j, i: (i, 0)))
        args.append(a)
    assert off == k
    in_specs.append(pl.BlockSpec((k, tn), lambda j, i: (0, j)))
    args.append(w)
    io_alias = {}
    if mode == "resid":
        x, mod, k_gate = extra
        in_specs.append(pl.BlockSpec((tm, tn), lambda j, i: (i, j)))
        args.append(x)
        in_specs.append(_mod_spec(st, tm, k_gate, tn, 2, 1))
        args.append(mod)
        io_alias = {len(args) - 2: 0}
        out_dtype = F32
    elif mode == "glu":
        g, bias = extra
        in_specs.append(pl.BlockSpec((tm, tn), lambda j, i: (i, j)))
        args.append(g)
        in_specs.append(pl.BlockSpec((1, tn), lambda j, i: (0, j)))
        args.append(bias.reshape(1, n))
    return pl.pallas_call(
        functools.partial(_mm_kernel, n_a=len(a_parts), k_offs=tuple(k_offs), mode=mode),
        grid=(n // tn, m // tm),
        in_specs=in_specs,
        out_specs=pl.BlockSpec((tm, tn), lambda j, i: (i, j)),
        out_shape=jax.ShapeDtypeStruct((m, n), out_dtype),
        scratch_shapes=[pltpu.VMEM((k, tn), BF16)],
        input_output_aliases=io_alias,
        compiler_params=_cparams("arbitrary", "arbitrary"),
        name="mm_" + mode,
    )(*args)


def _s5_kernel(u_ref, b_ref, c_ref, lam_ref, y_ref, h_ref, bu_ref, *, tc):
    half = b_ref.shape[1] // 2

    @pl.when(pl.program_id(1) == 0)
    def _():
        h_ref[...] = jnp.zeros_like(h_ref)

    rows = tc * 8
    is_fwd = (lax.broadcasted_iota(jnp.int32, (rows, LANES), 0) % 8) < 4
    u = u_ref[...].astype(F32)
    lhs = jnp.concatenate([jnp.where(is_fwd, u, 0.0), jnp.where(is_fwd, 0.0, u)], axis=1).astype(BF16)
    bu_ref[...] = jnp.dot(lhs, b_ref[...], preferred_element_type=F32)

    lam_r = lam_ref[0]
    lam_i = lam_ref[1]

    def step(t, carry):
        hr, hi = carry
        r0 = pl.multiple_of(t * 8, 8)
        nhr = lam_r * hr - lam_i * hi + bu_ref[pl.ds(r0, 8), 0:half]
        nhi = lam_r * hi + lam_i * hr + bu_ref[pl.ds(r0, 8), half:2 * half]
        bu_ref[pl.ds(r0, 8), 0:half] = nhr
        bu_ref[pl.ds(r0, 8), half:2 * half] = nhi
        return nhr, nhi

    hr, hi = lax.fori_loop(0, tc, step, (h_ref[:, 0:half], h_ref[:, half:2 * half]), unroll=2)
    h_ref[:, 0:half] = hr
    h_ref[:, half:2 * half] = hi

    y2 = jnp.dot(bu_ref[...].astype(BF16), c_ref[...], preferred_element_type=F32)
    y_ref[...] = jnp.where(is_fwd, y2[:, 0:LANES], y2[:, LANES:2 * LANES]).astype(y_ref.dtype)


def s5_scan(u8, b_bd, c_bd, lam8, t_len):
    nb = u8.shape[0]
    tc = _pick(t_len, (128, 64, 32, 16, 8))
    rows = tc * 8
    ns = b_bd.shape[2]
    return pl.pallas_call(
        functools.partial(_s5_kernel, tc=tc),
        grid=(nb, t_len // tc),
        in_specs=[pl.BlockSpec((None, rows, LANES), lambda j, c: (j, c, 0)),
                  pl.BlockSpec((None, 2 * LANES, ns), lambda j, c: (j, 0, 0)),
                  pl.BlockSpec((None, ns, 2 * LANES), lambda j, c: (j, 0, 0)),
                  pl.BlockSpec((None, 2, 8, ns // 2), lambda j, c: (j, 0, 0, 0))],
        out_specs=pl.BlockSpec((None, rows, LANES), lambda j, c: (j, c, 0)),
        out_shape=jax.ShapeDtypeStruct(u8.shape, BF16),
        scratch_shapes=[pltpu.VMEM((8, ns), F32), pltpu.VMEM((rows, ns), F32)],
        compiler_params=_cparams("arbitrary", "arbitrary"),
        name="s5_scan",
    )(u8, b_bd, c_bd, lam8)


def _s5_gelu_kernel(y_ref, u_ref, d_ref, o_ref):
    y = y_ref[...].astype(F32) + d_ref[...] * u_ref[...].astype(F32)
    o_ref[...] = jax.nn.gelu(y).astype(o_ref.dtype)


def s5_gelu(y, p, d_skip, width):
    nt = y.shape[0]
    tm = _pick(nt, (512, 256, 128, 64, 32, 16, 8))
    return pl.pallas_call(
        _s5_gelu_kernel,
        grid=(nt // tm,),
        in_specs=[pl.BlockSpec((tm, width), lambda i: (i, 0)),
                  pl.BlockSpec((tm, width), lambda i: (i, 0)),
                  pl.BlockSpec((1, width), lambda i: (0, 0))],
        out_specs=pl.BlockSpec((tm, width), lambda i: (i, 0)),
        out_shape=jax.ShapeDtypeStruct((nt, width), BF16),
        compiler_params=_cparams("arbitrary"),
        name="s5_gelu",
    )(y, p, d_skip.reshape(1, width))


def s5_tables(lam_re, lam_im, log_dt, b_re, b_im, c_re, c_im):
    _, g, p = lam_re.shape
    c = b_re.shape[-1]
    nb = g // S5_BLOCK_GROUPS
    gb = S5_BLOCK_GROUPS
    lam = lax.complex(lam_re.astype(F32), lam_im.astype(F32))
    dt = jnp.exp(log_dt.astype(F32))[..., None]
    lam_bar = jnp.exp(lam * dt)
    b_bar = ((lam_bar - 1.0) / lam)[..., None] * lax.complex(b_re.astype(F32), b_im.astype(F32))
    eye = jnp.eye(gb, dtype=F32)
    bb = b_bar.reshape(2, nb, gb, p, c)

    def b_lay(z):
        return jnp.einsum('djgpc,gh->jdgchp', z, eye).reshape(nb, 2 * gb * c, gb * p)

    b_bd = jnp.concatenate([b_lay(bb.real), b_lay(bb.imag)], axis=-1).astype(BF16)
    cc = lax.complex(c_re.astype(F32), c_im.astype(F32)).reshape(2, nb, gb, c, p)

    def c_lay(z):
        return jnp.einsum('djgcp,gh->jgpdhc', z, eye).reshape(nb, gb * p, 2 * gb * c)

    c_bd = jnp.concatenate([c_lay(cc.real), -c_lay(cc.imag)], axis=1).astype(BF16)
    lb = lam_bar.reshape(2, nb, gb * p)
    lam8 = jnp.stack([lb.real, lb.imag], axis=0)
    lam8 = jnp.repeat(lam8.transpose(2, 0, 1, 3), 4, axis=2)
    return b_bd, c_bd, lam8


def _fnet_chan_kernel(f_ref, cd_ref, sd_ref, w_ref, o_ref, wc_ref):
    nh, dh = w_ref.shape[0], w_ref.shape[1]

    @pl.when(pl.program_id(0) == 0)
    def _():
        for h in range(nh):
            wh = w_ref[h]
            wc_ref[h, :, 0:dh] = jnp.dot(cd_ref[...], wh, preferred_element_type=F32,
                                         precision=lax.Precision.HIGHEST).astype(BF16)
            wc_ref[h, :, dh:2 * dh] = jnp.dot(sd_ref[...], wh, preferred_element_type=F32,
                                              precision=lax.Precision.HIGHEST).astype(BF16)

    for h in range(nh):
        r = jnp.dot(f_ref[:, h * dh:(h + 1) * dh], wc_ref[h], preferred_element_type=F32)
        o_ref[:, h * dh:(h + 1) * dh] = r[:, 0:dh].astype(o_ref.dtype)
        o_ref[:, (nh + h) * dh:(nh + h + 1) * dh] = r[:, dh:2 * dh].astype(o_ref.dtype)


def fnet_chan(p, col_block, fnet_w, cd, sd):
    nt = p.shape[0]
    nh, dh, _ = fnet_w.shape
    width = nh * dh
    tm = _pick(nt, (512, 256, 128, 64, 32, 16, 8))
    return pl.pallas_call(
        _fnet_chan_kernel,
        grid=(nt // tm,),
        in_specs=[pl.BlockSpec((tm, width), lambda i: (i, col_block)),
                  pl.BlockSpec((dh, dh), lambda i: (0, 0)),
                  pl.BlockSpec((dh, dh), lambda i: (0, 0)),
                  pl.BlockSpec((nh, dh, dh), lambda i: (0, 0, 0))],
        out_specs=pl.BlockSpec((tm, 2 * width), lambda i: (i, 0)),
        out_shape=jax.ShapeDtypeStruct((nt, 2 * width), BF16),
        scratch_shapes=[pltpu.VMEM((nh, dh, 2 * dh), BF16)],
        compiler_params=_cparams("arbitrary"),
        name="fnet_chan",
    )(p, cd, sd, fnet_w)


def _fnet_seq_kernel(ct_ref, st_ref, p_ref, o_ref):
    w = o_ref.shape[1]
    o_ref[...] = (jnp.dot(ct_ref[...], p_ref[:, 0:w], preferred_element_type=F32)
                  - jnp.dot(st_ref[...], p_ref[:, w:2 * w], preferred_element_type=F32)).astype(o_ref.dtype)


def fnet_seq(pcs, ct, sn, n_seg, t_len, row_block0):
    w2 = pcs.shape[1]
    tm = _pick(t_len, (512, 256, 128, 64, 32, 16, 8))
    nq = t_len // tm
    return pl.pallas_call(
        _fnet_seq_kernel,
        grid=(n_seg, nq),
        in_specs=[pl.BlockSpec((tm, t_len), lambda b, i: (i, 0)),
                  pl.BlockSpec((tm, t_len), lambda b, i: (i, 0)),
                  pl.BlockSpec((t_len, w2), lambda b, i: (row_block0 + b, 0))],
        out_specs=pl.BlockSpec((tm, w2 // 2), lambda b, i: (b * nq + i, 0)),
        out_shape=jax.ShapeDtypeStruct((n_seg * t_len, w2 // 2), BF16),
        compiler_params=_cparams("arbitrary", "arbitrary"),
        name="fnet_seq",
    )(ct, sn, pcs)


def dft_tables(n):
    k = jnp.arange(n, dtype=jnp.int32)
    kn = (k[:, None] * k[None, :]) % n
    ang = kn.astype(F32) * (2.0 * math.pi / n)
    scale = 1.0 / math.sqrt(n)
    return jnp.cos(ang) * scale, jnp.sin(ang) * scale


def _pool_kernel(v_ref, w_ref, sc_ref, o_ref, *, windows):
    t_len = v_ref.shape[0]
    dh = w_ref.shape[1]
    t = lax.broadcasted_iota(jnp.int32, (t_len, dh), 0)
    for g, win in enumerate(windows):
        v = v_ref[:, g * dh:(g + 1) * dh].astype(F32)
        lo = win // 2
        acc = jnp.zeros_like(v)
        for j in range(-lo, win - lo):
            src = t + j
            shifted = v if j == 0 else pltpu.roll(v, (-j) % t_len, 0)
            acc = acc + jnp.where((src >= 0) & (src < t_len), shifted, 0.0)
        cnt = jnp.clip(t + (win - lo), 0, t_len) - jnp.clip(t - lo, 0, t_len)
        pooled = acc / cnt.astype(F32) - v
        y = jnp.dot(pooled.astype(BF16), w_ref[g].astype(BF16), preferred_element_type=F32)
        o_ref[:, g * dh:(g + 1) * dh] = (y * sc_ref[:, g * dh:(g + 1) * dh]).astype(o_ref.dtype)


def pool_mix(p, pool_w, pool_scale, n_seg, t_len, row_block0):
    ng, dh, _ = pool_w.shape
    width = ng * dh
    return pl.pallas_call(
        functools.partial(_pool_kernel, windows=POOL_WINDOWS),
        grid=(n_seg,),
        in_specs=[pl.BlockSpec((t_len, width), lambda b: (row_block0 + b, 0)),
                  pl.BlockSpec((ng, dh, dh), lambda b: (0, 0, 0)),
                  pl.BlockSpec((1, width), lambda b: (0, 0))],
        out_specs=pl.BlockSpec((t_len, width), lambda b: (b, 0)),
        out_shape=jax.ShapeDtypeStruct((n_seg * t_len, width), BF16),
        compiler_params=_cparams("arbitrary"),
        name="pool_mix",
    )(p, pool_w, pool_scale.reshape(1, width))


def _qk_prep_kernel(x_ref, g_ref, cos_ref, sin_ref, o_ref):
    nh = x_ref.shape[1] // HEAD_DIM
    gain = g_ref[...]
    cos = cos_ref[...]
    sin = sin_ref[...]
    lane = lax.broadcasted_iota(jnp.int32, (x_ref.shape[0], HEAD_DIM), 1)
    first = (lane % (HEAD_DIM // 2)) < (HEAD_DIM // 4)
    for h in range(nh):
        x = x_ref[:, h * HEAD_DIM:(h + 1) * HEAD_DIM].astype(F32)
        n = x * lax.rsqrt(jnp.mean(x * x, axis=-1, keepdims=True) + EPS) * gain
        partner = jnp.where(first, pltpu.roll(n, HEAD_DIM - HEAD_DIM // 4, 1), pltpu.roll(n, HEAD_DIM // 4, 1))
        o_ref[:, h * HEAD_DIM:(h + 1) * HEAD_DIM] = (n * cos + partner * sin).astype(o_ref.dtype)


def qk_prep(st, p, gains, cos_t, sin_t, q_blocks):
    nt = p.shape[0]
    cw = KV_HEADS * HEAD_DIM
    tm = _pick(st.tm, (512, 256, 128, 64, 32, 16, 8))
    n_lat, s = st.n_lat, st.s
    rope_blk = lambda i: jnp.where(i * tm < n_lat, i % (s // tm), s // tm)
    return pl.pallas_call(
        _qk_prep_kernel,
        grid=(nt // tm, q_blocks + 1),
        in_specs=[pl.BlockSpec((tm, cw), lambda i, j: (i, j + 1)),
                  pl.BlockSpec((None, 1, HEAD_DIM), lambda i, j: (j // q_blocks, 0, 0)),
                  pl.BlockSpec((tm, HEAD_DIM), lambda i, j: (rope_blk(i), 0)),
                  pl.BlockSpec((tm, HEAD_DIM), lambda i, j: (rope_blk(i), 0))],
        out_specs=pl.BlockSpec((tm, cw), lambda i, j: (i, j)),
        out_shape=jax.ShapeDtypeStruct((nt, (q_blocks + 1) * cw), BF16),
        compiler_params=_cparams("arbitrary", "arbitrary"),
        name="qk_prep",
    )(p, gains, cos_t, sin_t)


def rope_tables(s, tm):
    rows = s // GRID_W
    row = jnp.repeat(jnp.arange(rows), GRID_W)
    col = jnp.tile(jnp.arange(GRID_W), rows)
    nf = HEAD_DIM // 4
    inv = jnp.power(ROPE_THETA, -jnp.arange(nf, dtype=F32) / nf)
    a_row = row.astype(F32)[:, None] * inv
    a_col = col.astype(F32)[:, None] * inv
    ang = jnp.concatenate([a_row, a_row, a_col, a_col], axis=1)
    sign = jnp.tile(jnp.concatenate([-jnp.ones((nf,), F32), jnp.ones((nf,), F32)]), 2)
    cos_t = jnp.concatenate([jnp.cos(ang), jnp.ones((tm, HEAD_DIM), F32)], axis=0)
    sin_t = jnp.concatenate([jnp.sin(ang) * sign, jnp.zeros((tm, HEAD_DIM), F32)], axis=0)
    return cos_t, sin_t


def _attn_kernel(*refs, n_seg, q_per_kv, scale):
    q_ref = refs[0]
    k_refs = refs[1:1 + n_seg]
    v_refs = refs[1 + n_seg:1 + 2 * n_seg]
    o_ref = refs[1 + 2 * n_seg]
    dn = (((1,), (1,)), ((), ()))
    for g in range(q_per_kv):
        q = q_ref[:, g * HEAD_DIM:(g + 1) * HEAD_DIM]
        ss = [lax.dot_general(q, k_ref[...], dn, preferred_element_type=F32) * scale for k_ref in k_refs]
        m = ss[0].max(axis=-1, keepdims=True)
        for s_ in ss[1:]:
            m = jnp.maximum(m, s_.max(axis=-1, keepdims=True))
        den = None
        acc = None
        for s_, v_ref in zip(ss, v_refs):
            e = jnp.exp(s_ - m)
            d_ = e.sum(axis=-1, keepdims=True)
            a_ = jnp.dot(e.astype(BF16), v_ref[...], preferred_element_type=F32)
            den = d_ if den is None else den + d_
            acc = a_ if acc is None else acc + a_
        o_ref[:, g * HEAD_DIM:(g + 1) * HEAD_DIM] = (acc / den).astype(o_ref.dtype)


def attention(qk, p, v_col0, n_batch, q_len, q_row_block0, key_segs, n_q_heads):
    q_per_kv = n_q_heads // KV_HEADS
    qw = q_per_kv * HEAD_DIM
    tq = _pick(q_len, (256, 128, 64, 32, 16, 8))
    nq = q_len // tq
    k_col0 = n_q_heads
    in_specs = [pl.BlockSpec((tq, qw), lambda b, h, i: (q_row_block0 * nq + b * nq + i, h))]
    args = [qk]
    for (ln, off) in key_segs:
        in_specs.append(pl.BlockSpec((ln, HEAD_DIM), lambda b, h, i, off=off: (off + b, k_col0 + h)))
        args.append(qk)
    for (ln, off) in key_segs:
        in_specs.append(pl.BlockSpec((ln, HEAD_DIM), lambda b, h, i, off=off: (off + b, v_col0 + h)))
        args.append(p)
    return pl.pallas_call(
        functools.partial(_attn_kernel, n_seg=len(key_segs), q_per_kv=q_per_kv, scale=HEAD_DIM ** -0.5),
        grid=(n_batch, KV_HEADS, nq),
        in_specs=in_specs,
        out_specs=pl.BlockSpec((tq, qw), lambda b, h, i: (b * nq + i, h)),
        out_shape=jax.ShapeDtypeStruct((n_batch * q_len, n_q_heads * HEAD_DIM), BF16),
        compiler_params=_cparams("arbitrary", "arbitrary", "arbitrary"),
        name="attention",
    )(*args)


def _router_kernel(x_ref, g_ref, sh_ref, sc_ref, wr_ref, br_ref, h_ref, idx_ref, wgt_ref):
    x = x_ref[...]
    y = x * lax.rsqrt(jnp.mean(x * x, axis=-1, keepdims=True) + EPS) * g_ref[...]
    h = y * (1.0 + sc_ref[...]) + sh_ref[...]
    h_ref[...] = h.astype(h_ref.dtype)
    logits = lax.dot_general(wr_ref[...], h, (((1,), (1,)), ((), ())), preferred_element_type=F32,
                             precision=lax.Precision.HIGHEST) + br_ref[...]
    ne, tm = logits.shape
    eidx = lax.broadcasted_iota(jnp.int32, (ne, tm), 0)
    vals, idxs = [], []
    cur = logits
    for _ in range(TOP_K):
        m = cur.max(axis=0, keepdims=True)
        sel = jnp.min(jnp.where(cur == m, eidx, ne), axis=0, keepdims=True)
        vals.append(m)
        idxs.append(sel)
        cur = jnp.where(eidx == sel, -jnp.inf, cur)
    es = [jnp.exp(v - vals[0]) for v in vals]
    den = es[0]
    for e in es[1:]:
        den = den + e
    zi = jnp.zeros((8 - TOP_K, tm), jnp.int32)
    zf = jnp.zeros((8 - TOP_K, tm), F32)
    idx_ref[...] = jnp.concatenate(idxs + [zi], axis=0)
    wgt_ref[...] = jnp.concatenate([e / den for e in es] + [zf], axis=0)


def norm_router(st, x, g, mod, k_shift, k_scale, w_router, b_router):
    nt, d = x.shape
    ne = w_router.shape[1]
    tm = _pick(st.tm, (256, 128))
    return pl.pallas_call(
        _router_kernel,
        grid=(nt // tm,),
        in_specs=[pl.BlockSpec((tm, d), lambda i: (i, 0)),
                  pl.BlockSpec((1, d), lambda i: (0, 0)),
                  _mod_spec(st, tm, k_shift, d, 1, 0),
                  _mod_spec(st, tm, k_scale, d, 1, 0),
                  pl.BlockSpec((ne, d), lambda i: (0, 0)),
                  pl.BlockSpec((ne, 1), lambda i: (0, 0))],
        out_specs=[pl.BlockSpec((tm, d), lambda i: (i, 0)),
                   pl.BlockSpec((8, tm), lambda i: (0, i)),
                   pl.BlockSpec((8, tm), lambda i: (0, i))],
        out_shape=[jax.ShapeDtypeStruct((nt, d), BF16),
                   jax.ShapeDtypeStruct((8, nt), jnp.int32),
                   jax.ShapeDtypeStruct((8, nt), F32)],
        compiler_params=_cparams("arbitrary"),
        name="norm_router",
    )(x, g.reshape(1, d), mod, mod, w_router.T, b_router.reshape(ne, 1))


def _expert_kernel(be_ref, nb_ref, x_ref, wg_ref, bg_ref, wu_ref, bu_ref, wd_ref, bd_ref, wt_ref, o_ref,
                   wg_bf, wu_bf, wd_bf):
    i = pl.program_id(0)
    prev = be_ref[jnp.maximum(i - 1, 0)]
    live = i < nb_ref[0]

    @pl.when(live & ((i == 0) | (be_ref[i] != prev)))
    def _():
        wg_bf[...] = wg_ref[...].astype(BF16)
        wu_bf[...] = wu_ref[...].astype(BF16)
        wd_bf[...] = wd_ref[...].astype(BF16)

    @pl.when(live)
    def _():
        x = x_ref[...]
        g = jnp.minimum(jnp.dot(x, wg_bf[...], preferred_element_type=F32) + bg_ref[...], SWIGLU_LIMIT)
        u = jnp.clip(jnp.dot(x, wu_bf[...], preferred_element_type=F32) + bu_ref[...], -SWIGLU_LIMIT, SWIGLU_LIMIT)
        a = g * jax.nn.sigmoid(SWIGLU_ALPHA * g) * (u + 1.0)
        y = jnp.dot(a.astype(BF16), wd_bf[...], preferred_element_type=F32) + bd_ref[...]
        o_ref[...] = (y * wt_ref[...]).astype(o_ref.dtype)

    @pl.when(jnp.logical_not(live))
    def _():
        o_ref[...] = jnp.zeros_like(o_ref)


def expert_ffn(x_sorted, wt_sorted, blk_e, n_live, w_gate, b_gate, w_up, b_up, w_down, b_down):
    cap, d = x_sorted.shape
    ne, _, f = w_gate.shape
    tb = MOE_ROWS
    grid_spec = pltpu.PrefetchScalarGridSpec(
        num_scalar_prefetch=2,
        grid=(cap // tb,),
        in_specs=[pl.BlockSpec((tb, d), lambda i, be, nb: (i, 0)),
                  pl.BlockSpec((None, d, f), lambda i, be, nb: (be[i], 0, 0)),
                  pl.BlockSpec((None, 1, f), lambda i, be, nb: (be[i], 0, 0)),
                  pl.BlockSpec((None, d, f), lambda i, be, nb: (be[i], 0, 0)),
                  pl.BlockSpec((None, 1, f), lambda i, be, nb: (be[i], 0, 0)),
                  pl.BlockSpec((None, f, d), lambda i, be, nb: (be[i], 0, 0)),
                  pl.BlockSpec((None, 1, d), lambda i, be, nb: (be[i], 0, 0)),
                  pl.BlockSpec((tb, 1), lambda i, be, nb: (i, 0))],
        out_specs=pl.BlockSpec((tb, d), lambda i, be, nb: (i, 0)),
        scratch_shapes=[pltpu.VMEM((d, f), BF16), pltpu.VMEM((d, f), BF16), pltpu.VMEM((f, d), BF16)],
    )
    return pl.pallas_call(
        _expert_kernel,
        grid_spec=grid_spec,
        out_shape=jax.ShapeDtypeStruct((cap, d), BF16),
        compiler_params=_cparams("arbitrary"),
        name="expert_ffn",
    )(blk_e, n_live, x_sorted, w_gate, b_gate.reshape(ne, 1, f), w_up, b_up.reshape(ne, 1, f),
      w_down, b_down.reshape(ne, 1, d), wt_sorted.reshape(cap, 1))


def _combine_kernel(x_ref, y_ref, gate_ref, o_ref):
    o_ref[...] = x_ref[...] + gate_ref[...] * y_ref[...].astype(F32).sum(axis=1)


def moe_combine(st, x, y4, mod, k_gate):
    nt, d = x.shape
    tm = _pick(st.tm, (128, 64, 32, 16, 8))
    return pl.pallas_call(
        _combine_kernel,
        grid=(nt // tm,),
        in_specs=[pl.BlockSpec((tm, d), lambda i: (i, 0)),
                  pl.BlockSpec((tm, TOP_K, d), lambda i: (i, 0, 0)),
                  _mod_spec(st, tm, k_gate, d, 1, 0)],
        out_specs=pl.BlockSpec((tm, d), lambda i: (i, 0)),
        out_shape=jax.ShapeDtypeStruct((nt, d), F32),
        input_output_aliases={0: 0},
        compiler_params=_cparams("arbitrary"),
        name="moe_combine",
    )(x, y4, mod)


def moe_layer(st, x, g, mod, w_router, b_router, w_gate, b_gate, w_up, b_up, w_down, b_down):
    nt, d = x.shape
    ne = w_router.shape[1]
    tb = MOE_ROWS
    h, idx8, wgt8 = norm_router(st, x, g, mod, 3, 4, w_router, b_router)
    flat_e = idx8[:TOP_K].T.reshape(-1)
    flat_w = wgt8[:TOP_K].T.reshape(-1)
    n_assign = nt * TOP_K
    cap = -(-(n_assign + ne * (tb - 1)) // tb) * tb
    n_blocks = cap // tb
    onehot = (flat_e[:, None] == jnp.arange(ne, dtype=jnp.int32)[None, :]).astype(jnp.int32)
    csum = jnp.cumsum(onehot, axis=0)
    rank = jnp.take_along_axis(csum, flat_e[:, None], axis=1)[:, 0] - 1
    counts = csum[-1]
    padded = (counts + tb - 1) // tb * tb
    pend = jnp.cumsum(padded)
    pstart = pend - padded
    dest = pstart[flat_e] + rank
    flat_tok = jnp.arange(n_assign, dtype=jnp.int32) // TOP_K
    tok_buf = jnp.zeros((cap,), jnp.int32).at[dest].set(flat_tok)
    w_buf = jnp.zeros((cap,), F32).at[dest].set(flat_w)
    blk_e = jnp.clip(jnp.searchsorted(pend, jnp.arange(n_blocks, dtype=jnp.int32) * tb, side='right'),
                     0, ne - 1).astype(jnp.int32)
    n_live = (pend[-1] // tb).astype(jnp.int32).reshape(1)
    x_sorted = jnp.take(h, tok_buf, axis=0)
    y_sorted = expert_ffn(x_sorted, w_buf, blk_e, n_live, w_gate, b_gate, w_up, b_up, w_down, b_down)
    y4 = jnp.take(y_sorted, dest, axis=0).reshape(nt, TOP_K, d)
    return moe_combine(st, x, y4, mod, 5)


def even_mixer(st, x, h, mod, w_in, w_out, s5p, d_skip, glu_w, glu_b, fnet_w, tables):
    b, s, l = st.b, st.s, st.l
    d = x.shape[1]
    fw = fnet_w.shape[0] * fnet_w.shape[1]
    sw = d - fw
    nb = sw // LANES
    t_len = l + s
    p = matmul([h], w_in, st=st)
    u_lat = p[:st.n_lat, :sw].reshape(b, s, nb, LANES)
    u_ctx = p[st.n_lat:, :sw].reshape(b, l, nb, LANES)
    seq_f = jnp.concatenate([u_ctx, u_lat], axis=1)
    seq_b = jnp.concatenate([u_ctx[:, ::-1], u_lat[:, ::-1]], axis=1)
    u8 = jnp.stack([seq_f, seq_b], axis=0).transpose(3, 2, 0, 1, 4).reshape(nb, t_len * 2 * b, LANES)
    b_bd, c_bd, lam8 = s5p
    y8 = s5_scan(u8, b_bd, c_bd, lam8, t_len)
    y8 = y8.reshape(nb, t_len, 2, b, LANES).transpose(2, 3, 1, 0, 4).reshape(2, b, t_len, sw)
    y_f, y_b = y8[0].astype(F32), y8[1].astype(F32)
    y_ctx = y_f[:, :l] + y_b[:, :l][:, ::-1]
    y_lat = y_f[:, l:] + y_b[:, l:][:, ::-1]
    y_ssm = jnp.concatenate([y_lat.reshape(st.n_lat, sw), y_ctx.reshape(st.n_ctx, sw)], axis=0)
    gl = s5_gelu(y_ssm, p, d_skip, sw)
    a = matmul([gl], glu_w, mode="glu", extra=(gl, glu_b), st=st, tn_prefs=(768, 512, 256, 128))
    cd, sd, (ct_s, sn_s), (ct_l, sn_l) = tables
    pcs = fnet_chan(p, sw // fw, fnet_w, cd, sd)
    f_lat = fnet_seq(pcs, ct_s, sn_s, b, s, 0)
    f_ctx = fnet_seq(pcs, ct_l, sn_l, b, l, st.n_lat // l)
    fm = jnp.concatenate([f_lat, f_ctx], axis=0)
    return matmul([a, fm], w_out, mode="resid", extra=(x, mod, 2), st=st)


def odd_mixer(st, x, h, mod, w_in, w_out, pool_w, pool_scale, q_gain, k_gain, rope):
    b, s, l = st.b, st.s, st.l
    d = x.shape[1]
    pw = pool_w.shape[0] * pool_w.shape[1]
    n_q = (d - pw) // HEAD_DIM
    p = matmul([h], w_in, st=st)
    cos_t, sin_t = rope
    gains = jnp.stack([q_gain, k_gain]).reshape(2, 1, HEAD_DIM)
    qk = qk_prep(st, p, gains, cos_t, sin_t, n_q // KV_HEADS)
    v_col0 = (pw + n_q * HEAD_DIM + KV_HEADS * HEAD_DIM) // HEAD_DIM
    att_lat = attention(qk, p, v_col0, b, s, 0, [(l, st.n_lat // l), (s, 0)], n_q)
    att_ctx = attention(qk, p, v_col0, b, l, st.n_lat // l, [(l, st.n_lat // l)], n_q)
    att = jnp.concatenate([att_lat, att_ctx], axis=0)
    pool_lat = pool_mix(p, pool_w, pool_scale, b, s, 0)
    pool_ctx = pool_mix(p, pool_w, pool_scale, b, l, st.n_lat // l)
    pm = jnp.concatenate([pool_lat, pool_ctx], axis=0)
    return matmul([pm, att], w_out, mode="resid", extra=(x, mod, 2), st=st)


def kernel(x, c, ctx, c_ctx, ada_w, ada_b, norm_mix_g, norm_ffn_g, ev_w_in, ev_w_out, s5_lam_re, s5_lam_im, s5_log_dt, s5_b_re, s5_b_im, s5_c_re, s5_c_im, s5_d, s5_glu_w, s5_glu_b, fnet_w, od_w_in, od_w_out, pool_w, pool_scale, q_gain, k_gain, router_w, router_b, exp_w_gate, exp_b_gate, exp_w_up, exp_b_up, exp_w_down, exp_b_down, final_g):
    batch, seq, d = x.shape
    ctx_len = ctx.shape[1]
    depth = ada_w.shape[0]
    st = Stream(batch, seq, ctx_len)
    xs = jnp.concatenate([x.reshape(st.n_lat, d), ctx.reshape(st.n_ctx, d)], axis=0)

    cond8 = jnp.concatenate([c, c_ctx[None], jnp.zeros((8 - batch - 1, d), F32)], axis=0)
    mod_all = adaln_all(cond8, ada_w, ada_b)
    mod_all = mod_all.reshape(depth, 8, 6, 1, d).transpose(0, 2, 1, 3, 4)

    dh = fnet_w.shape[2]
    cd, sd = dft_tables(dh)
    tab_s = tuple(t.astype(BF16) for t in dft_tables(seq))
    tab_l = tuple(t.astype(BF16) for t in dft_tables(ctx_len))
    tables = (cd, sd, tab_s, tab_l)
    rope = rope_tables(seq, _pick(st.tm, (512, 256, 128, 64, 32, 16, 8)))

    for layer in range(depth):
        i = layer // 2
        mod = mod_all[layer]
        h = norm_mod(st, xs, norm_mix_g[layer], mod, 0, 1)
        if layer % 2 == 0:
            s5p = s5_tables(s5_lam_re[i], s5_lam_im[i], s5_log_dt[i], s5_b_re[i], s5_b_im[i], s5_c_re[i], s5_c_im[i])
            xs = even_mixer(st, xs, h, mod, ev_w_in[i], ev_w_out[i], s5p, s5_d[i], s5_glu_w[i], s5_glu_b[i],
                            fnet_w[i], tables)
        else:
            xs = odd_mixer(st, xs, h, mod, od_w_in[i], od_w_out[i], pool_w[i], pool_scale[i], q_gain[i], k_gain[i],
                           rope)
        xs = moe_layer(st, xs, norm_ffn_g[layer], mod, router_w[layer], router_b[layer],
                       exp_w_gate[layer], exp_b_gate[layer], exp_w_up[layer], exp_b_up[layer],
                       exp_w_down[layer], exp_b_down[layer])
    return final_norm(xs[:st.n_lat], final_g).reshape(batch, seq, d)
```

```python
import functools
import math

import jax
import jax.numpy as jnp
from jax import lax
from jax.experimental import pallas as pl
from jax.experimental.pallas import tpu as pltpu

F32 = jnp.float32
BF16 = jnp.bfloat16
EPS = 1e-6

GRID_W = 64
FNET_HEADS = 4
S5_GROUP = 16
S5_STATE = 64
POOL_WINDOWS = (2, 4, 8, 16)
HEAD_DIM = 128
KV_HEADS = 4
ROPE_THETA = 10000.0
TOP_K = 4
SWIGLU_LIMIT = 7.0
SWIGLU_ALPHA = 1.702

LANES = 128
S5_BLOCK_GROUPS = LANES // S5_GROUP
VMEM_LIMIT = 48 * 1024 * 1024
MOE_ROWS = 256


def _cparams(*sem):
    return pltpu.CompilerParams(dimension_semantics=sem, vmem_limit_bytes=VMEM_LIMIT)


def _pick(n, prefs):
    for p in prefs:
        if n % p == 0:
            return p
    return n


def _adaln_kernel(c_ref, w_ref, b_ref, o_ref):
    c = c_ref[...]
    a = (c * jax.nn.sigmoid(c)).astype(BF16)
    o_ref[...] = jnp.dot(a, w_ref[...].astype(BF16), preferred_element_type=F32) + b_ref[...]


def adaln_all(cond8, ada_w, ada_b):
    depth, d, n = ada_w.shape
    tn = _pick(n, (1024, 512, 256, 128))
    return pl.pallas_call(
        _adaln_kernel,
        grid=(depth, n // tn),
        in_specs=[pl.BlockSpec((8, d), lambda l, j: (0, 0)),
                  pl.BlockSpec((None, d, tn), lambda l, j: (l, 0, j)),
                  pl.BlockSpec((None, 1, tn), lambda l, j: (l, 0, j))],
        out_specs=pl.BlockSpec((None, 8, tn), lambda l, j: (l, 0, j)),
        out_shape=jax.ShapeDtypeStruct((depth, 8, n), F32),
        compiler_params=_cparams("arbitrary", "arbitrary"),
        name="adaln",
    )(cond8, ada_w, ada_b.reshape(depth, 1, n))


def _norm_mod_kernel(x_ref, g_ref, sh_ref, sc_ref, o_ref):
    x = x_ref[...]
    y = x * lax.rsqrt(jnp.mean(x * x, axis=-1, keepdims=True) + EPS) * g_ref[...]
    o_ref[...] = (y * (1.0 + sc_ref[...]) + sh_ref[...]).astype(o_ref.dtype)


def _norm_kernel(x_ref, g_ref, o_ref):
    x = x_ref[...]
    o_ref[...] = (x * lax.rsqrt(jnp.mean(x * x, axis=-1, keepdims=True) + EPS) * g_ref[...]).astype(o_ref.dtype)


class Stream:
    def __init__(self, batch, seq, ctx_len):
        self.b, self.s, self.l = batch, seq, ctx_len
        self.n_lat = batch * seq
        self.n_ctx = batch * ctx_len
        self.nt = self.n_lat + self.n_ctx
        self.tm = _pick(math.gcd(seq, self.n_ctx), (512, 256, 128, 64, 32, 16))

    def mod_row(self, tm):
        s, b = self.s, self.b
        return lambda i: jnp.minimum((i * tm) // s, b)


def _mod_spec(st, tm, k, d, grid_rank, row_axis):
    mr = st.mod_row(tm)
    if grid_rank == 1:
        return pl.BlockSpec((None, None, 1, d), lambda i: (k, mr(i), 0, 0))
    if row_axis == 1:
        return pl.BlockSpec((None, None, 1, d), lambda j, i: (k, mr(i), 0, j))
    raise ValueError


def norm_mod(st, x, g, mod, k_shift, k_scale, out_dtype=BF16):
    nt, d = x.shape
    tm = _pick(st.tm, (256, 128, 64, 32, 16))
    return pl.pallas_call(
        _norm_mod_kernel,
        grid=(nt // tm,),
        in_specs=[pl.BlockSpec((tm, d), lambda i: (i, 0)),
                  pl.BlockSpec((1, d), lambda i: (0, 0)),
                  _mod_spec(st, tm, k_shift, d, 1, 0),
                  _mod_spec(st, tm, k_scale, d, 1, 0)],
        out_specs=pl.BlockSpec((tm, d), lambda i: (i, 0)),
        out_shape=jax.ShapeDtypeStruct((nt, d), out_dtype),
        compiler_params=_cparams("arbitrary"),
        name="norm_mod",
    )(x, g.reshape(1, d), mod, mod)


def final_norm(x, g):
    n, d = x.shape
    tm = _pick(n, (256, 128, 64, 32, 16, 8))
    return pl.pallas_call(
        _norm_kernel,
        grid=(n // tm,),
        in_specs=[pl.BlockSpec((tm, d), lambda i: (i, 0)), pl.BlockSpec((1, d), lambda i: (0, 0))],
        out_specs=pl.BlockSpec((tm, d), lambda i: (i, 0)),
        out_shape=jax.ShapeDtypeStruct((n, d), F32),
        compiler_params=_cparams("arbitrary"),
        name="final_norm",
    )(x, g.reshape(1, d))


def _mm_kernel(*refs, n_a, k_offs, mode):
    a_refs = refs[:n_a]
    w_ref = refs[n_a]
    rest = refs[n_a + 1:]
    wbf_ref = rest[-1]
    o_ref = rest[-2]

    @pl.when(pl.program_id(1) == 0)
    def _():
        wbf_ref[...] = w_ref[...].astype(BF16)

    acc = None
    for a_ref, (k0, k1) in zip(a_refs, k_offs):
        part = jnp.dot(a_ref[...], wbf_ref[k0:k1, :], preferred_element_type=F32)
        acc = part if acc is None else acc + part
    if mode == "plain":
        o_ref[...] = acc.astype(o_ref.dtype)
    elif mode == "resid":
        x_ref, gate_ref = rest[0], rest[1]
        o_ref[...] = x_ref[...] + gate_ref[...] * acc
    elif mode == "glu":
        g_ref, b_ref = rest[0], rest[1]
        o_ref[...] = (g_ref[...].astype(F32) * jax.nn.sigmoid(acc + b_ref[...])).astype(o_ref.dtype)
    else:
        raise ValueError(mode)


def matmul(a_parts, w, mode="plain", out_dtype=BF16, extra=(), st=None, tn_prefs=(1024, 768, 512, 256, 128)):
    m = a_parts[0].shape[0]
    k, n = w.shape
    tm = _pick(m if st is None else st.tm, (512, 256, 128, 64, 32, 16, 8))
    tn = _pick(n, tn_prefs)
    k_offs, off = [], 0
    in_specs, args = [], []
    for a in a_parts:
        ka = a.shape[1]
        k_offs.append((off, off + ka))
        off += ka
        in_specs.append(pl.BlockSpec((tm, ka), lambda j, i: (i, 0)))
        args.append(a)
    assert off == k
    in_specs.append(pl.BlockSpec((k, tn), lambda j, i: (0, j)))
    args.append(w)
    io_alias = {}
    if mode == "resid":
        x, mod, k_gate = extra
        in_specs.append(pl.BlockSpec((tm, tn), lambda j, i: (i, j)))
        args.append(x)
        in_specs.append(_mod_spec(st, tm, k_gate, tn, 2, 1))
        args.append(mod)
        io_alias = {len(args) - 2: 0}
        out_dtype = F32
    elif mode == "glu":
        g, bias = extra
        in_specs.append(pl.BlockSpec((tm, tn), lambda j, i: (i, j)))
        args.append(g)
        in_specs.append(pl.BlockSpec((1, tn), lambda j, i: (0, j)))
        args.append(bias.reshape(1, n))
    return pl.pallas_call(
        functools.partial(_mm_kernel, n_a=len(a_parts), k_offs=tuple(k_offs), mode=mode),
        grid=(n // tn, m // tm),
        in_specs=in_specs,
        out_specs=pl.BlockSpec((tm, tn), lambda j, i: (i, j)),
        out_shape=jax.ShapeDtypeStruct((m, n), out_dtype),
        scratch_shapes=[pltpu.VMEM((k, tn), BF16)],
        input_output_aliases=io_alias,
        compiler_params=_cparams("arbitrary", "arbitrary"),
        name="mm_" + mode,
    )(*args)


def _s5_kernel(u_ref, b_ref, c_ref, lam_ref, y_ref, h_ref, bu_ref, *, tc):
    half = b_ref.shape[1] // 2

    @pl.when(pl.program_id(1) == 0)
    def _():
        h_ref[...] = jnp.zeros_like(h_ref)

    rows = tc * 8
    is_fwd = (lax.broadcasted_iota(jnp.int32, (rows, LANES), 0) % 8) < 4
    u = u_ref[...].astype(F32)
    lhs = jnp.concatenate([jnp.where(is_fwd, u, 0.0), jnp.where(is_fwd, 0.0, u)], axis=1).astype(BF16)
    bu_ref[...] = jnp.dot(lhs, b_ref[...], preferred_element_type=F32)

    lam_r = lam_ref[0]
    lam_i = lam_ref[1]

    def step(t, carry):
        hr, hi = carry
        r0 = pl.multiple_of(t * 8, 8)
        nhr = lam_r * hr - lam_i * hi + bu_ref[pl.ds(r0, 8), 0:half]
        nhi = lam_r * hi + lam_i * hr + bu_ref[pl.ds(r0, 8), half:2 * half]
        bu_ref[pl.ds(r0, 8), 0:half] = nhr
        bu_ref[pl.ds(r0, 8), half:2 * half] = nhi
        return nhr, nhi

    hr, hi = lax.fori_loop(0, tc, step, (h_ref[:, 0:half], h_ref[:, half:2 * half]), unroll=2)
    h_ref[:, 0:half] = hr
    h_ref[:, half:2 * half] = hi

    y2 = jnp.dot(bu_ref[...].astype(BF16), c_ref[...], preferred_element_type=F32)
    y_ref[...] = jnp.where(is_fwd, y2[:, 0:LANES], y2[:, LANES:2 * LANES]).astype(y_ref.dtype)


def s5_scan(u8, b_bd, c_bd, lam8, t_len):
    nb = u8.shape[0]
    tc = _pick(t_len, (128, 64, 32, 16, 8))
    rows = tc * 8
    ns = b_bd.shape[2]
    return pl.pallas_call(
        functools.partial(_s5_kernel, tc=tc),
        grid=(nb, t_len // tc),
        in_specs=[pl.BlockSpec((None, rows, LANES), lambda j, c: (j, c, 0)),
                  pl.BlockSpec((None, 2 * LANES, ns), lambda j, c: (j, 0, 0)),
                  pl.BlockSpec((None, ns, 2 * LANES), lambda j, c: (j, 0, 0)),
                  pl.BlockSpec((None, 2, 8, ns // 2), lambda j, c: (j, 0, 0, 0))],
        out_specs=pl.BlockSpec((None, rows, LANES), lambda j, c: (j, c, 0)),
        out_shape=jax.ShapeDtypeStruct(u8.shape, BF16),
        scratch_shapes=[pltpu.VMEM((8, ns), F32), pltpu.VMEM((rows, ns), F32)],
        compiler_params=_cparams("arbitrary", "arbitrary"),
        name="s5_scan",
    )(u8, b_bd, c_bd, lam8)


def _s5_gelu_kernel(y_ref, u_ref, d_ref, o_ref):
    y = y_ref[...].astype(F32) + d_ref[...] * u_ref[...].astype(F32)
    o_ref[...] = jax.nn.gelu(y).astype(o_ref.dtype)


def s5_gelu(y, p, d_skip, width):
    nt = y.shape[0]
    tm = _pick(nt, (512, 256, 128, 64, 32, 16, 8))
    return pl.pallas_call(
        _s5_gelu_kernel,
        grid=(nt // tm,),
        in_specs=[pl.BlockSpec((tm, width), lambda i: (i, 0)),
                  pl.BlockSpec((tm, width), lambda i: (i, 0)),
                  pl.BlockSpec((1, width), lambda i: (0, 0))],
        out_specs=pl.BlockSpec((tm, width), lambda i: (i, 0)),
        out_shape=jax.ShapeDtypeStruct((nt, width), BF16),
        compiler_params=_cparams("arbitrary"),
        name="s5_gelu",
    )(y, p, d_skip.reshape(1, width))


def s5_tables(lam_re, lam_im, log_dt, b_re, b_im, c_re, c_im):
    _, g, p = lam_re.shape
    c = b_re.shape[-1]
    nb = g // S5_BLOCK_GROUPS
    gb = S5_BLOCK_GROUPS
    lam = lax.complex(lam_re.astype(F32), lam_im.astype(F32))
    dt = jnp.exp(log_dt.astype(F32))[..., None]
    lam_bar = jnp.exp(lam * dt)
    b_bar = ((lam_bar - 1.0) / lam)[..., None] * lax.complex(b_re.astype(F32), b_im.astype(F32))
    eye = jnp.eye(gb, dtype=F32)
    bb = b_bar.reshape(2, nb, gb, p, c)

    def b_lay(z):
        return jnp.einsum('djgpc,gh->jdgchp', z, eye).reshape(nb, 2 * gb * c, gb * p)

    b_bd = jnp.concatenate([b_lay(bb.real), b_lay(bb.imag)], axis=-1).astype(BF16)
    cc = lax.complex(c_re.astype(F32), c_im.astype(F32)).reshape(2, nb, gb, c, p)

    def c_lay(z):
        return jnp.einsum('djgcp,gh->jgpdhc', z, eye).reshape(nb, gb * p, 2 * gb * c)

    c_bd = jnp.concatenate([c_lay(cc.real), -c_lay(cc.imag)], axis=1).astype(BF16)
    lb = lam_bar.reshape(2, nb, gb * p)
    lam8 = jnp.stack([lb.real, lb.imag], axis=0)
    lam8 = jnp.repeat(lam8.transpose(2, 0, 1, 3), 4, axis=2)
    return b_bd, c_bd, lam8


def _fnet_chan_kernel(f_ref, cd_ref, sd_ref, w_ref, o_ref, wc_ref):
    nh, dh = w_ref.shape[0], w_ref.shape[1]

    @pl.when(pl.program_id(0) == 0)
    def _():
        for h in range(nh):
            wh = w_ref[h]
            wc_ref[h, :, 0:dh] = jnp.dot(cd_ref[...], wh, preferred_element_type=F32,
                                         precision=lax.Precision.HIGHEST).astype(BF16)
            wc_ref[h, :, dh:2 * dh] = jnp.dot(sd_ref[...], wh, preferred_element_type=F32,
                                              precision=lax.Precision.HIGHEST).astype(BF16)

    for h in range(nh):
        r = jnp.dot(f_ref[:, h * dh:(h + 1) * dh], wc_ref[h], preferred_element_type=F32)
        o_ref[:, h * dh:(h + 1) * dh] = r[:, 0:dh].astype(o_ref.dtype)
        o_ref[:, (nh + h) * dh:(nh + h + 1) * dh] = r[:, dh:2 * dh].astype(o_ref.dtype)


def fnet_chan(p, col_block, fnet_w, cd, sd):
    nt = p.shape[0]
    nh, dh, _ = fnet_w.shape
    width = nh * dh
    tm = _pick(nt, (512, 256, 128, 64, 32, 16, 8))
    return pl.pallas_call(
        _fnet_chan_kernel,
        grid=(nt // tm,),
        in_specs=[pl.BlockSpec((tm, width), lambda i: (i, col_block)),
                  pl.BlockSpec((dh, dh), lambda i: (0, 0)),
                  pl.BlockSpec((dh, dh), lambda i: (0, 0)),
                  pl.BlockSpec((nh, dh, dh), lambda i: (0, 0, 0))],
        out_specs=pl.BlockSpec((tm, 2 * width), lambda i: (i, 0)),
        out_shape=jax.ShapeDtypeStruct((nt, 2 * width), BF16),
        scratch_shapes=[pltpu.VMEM((nh, dh, 2 * dh), BF16)],
        compiler_params=_cparams("arbitrary"),
        name="fnet_chan",
    )(p, cd, sd, fnet_w)


def _fnet_seq_kernel(ct_ref, st_ref, p_ref, o_ref):
    w = o_ref.shape[1]
    o_ref[...] = (jnp.dot(ct_ref[...], p_ref[:, 0:w], preferred_element_type=F32)
                  - jnp.dot(st_ref[...], p_ref[:, w:2 * w], preferred_element_type=F32)).astype(o_ref.dtype)


def fnet_seq(pcs, ct, sn, n_seg, t_len, row_block0):
    w2 = pcs.shape[1]
    tm = _pick(t_len, (512, 256, 128, 64, 32, 16, 8))
    nq = t_len // tm
    return pl.pallas_call(
        _fnet_seq_kernel,
        grid=(n_seg, nq),
        in_specs=[pl.BlockSpec((tm, t_len), lambda b, i: (i, 0)),
                  pl.BlockSpec((tm, t_len), lambda b, i: (i, 0)),
                  pl.BlockSpec((t_len, w2), lambda b, i: (row_block0 + b, 0))],
        out_specs=pl.BlockSpec((tm, w2 // 2), lambda b, i: (b * nq + i, 0)),
        out_shape=jax.ShapeDtypeStruct((n_seg * t_len, w2 // 2), BF16),
        compiler_params=_cparams("arbitrary", "arbitrary"),
        name="fnet_seq",
    )(ct, sn, pcs)


def dft_tables(n):
    k = jnp.arange(n, dtype=jnp.int32)
    kn = (k[:, None] * k[None, :]) % n
    ang = kn.astype(F32) * (2.0 * math.pi / n)
    scale = 1.0 / math.sqrt(n)
    return jnp.cos(ang) * scale, jnp.sin(ang) * scale


def _pool_kernel(v_ref, w_ref, sc_ref, o_ref, *, windows):
    t_len = v_ref.shape[0]
    dh = w_ref.shape[1]
    t = lax.broadcasted_iota(jnp.int32, (t_len, dh), 0)
    for g, win in enumerate(windows):
        v = v_ref[:, g * dh:(g + 1) * dh].astype(F32)
        lo = win // 2
        acc = jnp.zeros_like(v)
        for j in range(-lo, win - lo):
            src = t + j
            shifted = v if j == 0 else pltpu.roll(v, (-j) % t_len, 0)
            acc = acc + jnp.where((src >= 0) & (src < t_len), shifted, 0.0)
        cnt = jnp.clip(t + (win - lo), 0, t_len) - jnp.clip(t - lo, 0, t_len)
        pooled = acc / cnt.astype(F32) - v
        y = jnp.dot(pooled.astype(BF16), w_ref[g].astype(BF16), preferred_element_type=F32)
        o_ref[:, g * dh:(g + 1) * dh] = (y * sc_ref[:, g * dh:(g + 1) * dh]).astype(o_ref.dtype)


def pool_mix(p, pool_w, pool_scale, n_seg, t_len, row_block0):
    ng, dh, _ = pool_w.shape
    width = ng * dh
    return pl.pallas_call(
        functools.partial(_pool_kernel, windows=POOL_WINDOWS),
        grid=(n_seg,),
        in_specs=[pl.BlockSpec((t_len, width), lambda b: (row_block0 + b, 0)),
                  pl.BlockSpec((ng, dh, dh), lambda b: (0, 0, 0)),
                  pl.BlockSpec((1, width), lambda b: (0, 0))],
        out_specs=pl.BlockSpec((t_len, width), lambda b: (b, 0)),
        out_shape=jax.ShapeDtypeStruct((n_seg * t_len, width), BF16),
        compiler_params=_cparams("arbitrary"),
        name="pool_mix",
    )(p, pool_w, pool_scale.reshape(1, width))


def _qk_prep_kernel(x_ref, g_ref, cos_ref, sin_ref, o_ref):
    nh = x_ref.shape[1] // HEAD_DIM
    gain = g_ref[...]
    cos = cos_ref[...]
    sin = sin_ref[...]
    lane = lax.broadcasted_iota(jnp.int32, (x_ref.shape[0], HEAD_DIM), 1)
    first = (lane % (HEAD_DIM // 2)) < (HEAD_DIM // 4)
    for h in range(nh):
        x = x_ref[:, h * HEAD_DIM:(h + 1) * HEAD_DIM].astype(F32)
        n = x * lax.rsqrt(jnp.mean(x * x, axis=-1, keepdims=True) + EPS) * gain
        partner = jnp.where(first, pltpu.roll(n, HEAD_DIM - HEAD_DIM // 4, 1), pltpu.roll(n, HEAD_DIM // 4, 1))
        o_ref[:, h * HEAD_DIM:(h + 1) * HEAD_DIM] = (n * cos + partner * sin).astype(o_ref.dtype)


def qk_prep(st, p, gains, cos_t, sin_t, q_blocks):
    nt = p.shape[0]
    cw = KV_HEADS * HEAD_DIM
    tm = _pick(st.tm, (512, 256, 128, 64, 32, 16, 8))
    n_lat, s = st.n_lat, st.s
    rope_blk = lambda i: jnp.where(i * tm < n_lat, i % (s // tm), s // tm)
    return pl.pallas_call(
        _qk_prep_kernel,
        grid=(nt // tm, q_blocks + 1),
        in_specs=[pl.BlockSpec((tm, cw), lambda i, j: (i, j + 1)),
                  pl.BlockSpec((None, 1, HEAD_DIM), lambda i, j: (j // q_blocks, 0, 0)),
                  pl.BlockSpec((tm, HEAD_DIM), lambda i, j: (rope_blk(i), 0)),
                  pl.BlockSpec((tm, HEAD_DIM), lambda i, j: (rope_blk(i), 0))],
        out_specs=pl.BlockSpec((tm, cw), lambda i, j: (i, j)),
        out_shape=jax.ShapeDtypeStruct((nt, (q_blocks + 1) * cw), BF16),
        compiler_params=_cparams("arbitrary", "arbitrary"),
        name="qk_prep",
    )(p, gains, cos_t, sin_t)


def rope_tables(s, tm):
    rows = s // GRID_W
    row = jnp.repeat(jnp.arange(rows), GRID_W)
    col = jnp.tile(jnp.arange(GRID_W), rows)
    nf = HEAD_DIM // 4
    inv = jnp.power(ROPE_THETA, -jnp.arange(nf, dtype=F32) / nf)
    a_row = row.astype(F32)[:, None] * inv
    a_col = col.astype(F32)[:, None] * inv
    ang = jnp.concatenate([a_row, a_row, a_col, a_col], axis=1)
    sign = jnp.tile(jnp.concatenate([-jnp.ones((nf,), F32), jnp.ones((nf,), F32)]), 2)
    cos_t = jnp.concatenate([jnp.cos(ang), jnp.ones((tm, HEAD_DIM), F32)], axis=0)
    sin_t = jnp.concatenate([jnp.sin(ang) * sign, jnp.zeros((tm, HEAD_DIM), F32)], axis=0)
    return cos_t, sin_t


def _attn_kernel(*refs, n_seg, q_per_kv, scale):
    q_ref = refs[0]
    k_refs = refs[1:1 + n_seg]
    v_refs = refs[1 + n_seg:1 + 2 * n_seg]
    o_ref = refs[1 + 2 * n_seg]
    dn = (((1,), (1,)), ((), ()))
    for g in range(q_per_kv):
        q = q_ref[:, g * HEAD_DIM:(g + 1) * HEAD_DIM]
        ss = [lax.dot_general(q, k_ref[...], dn, preferred_element_type=F32) * scale for k_ref in k_refs]
        m = ss[0].max(axis=-1, keepdims=True)
        for s_ in ss[1:]:
            m = jnp.maximum(m, s_.max(axis=-1, keepdims=True))
        den = None
        acc = None
        for s_, v_ref in zip(ss, v_refs):
            e = jnp.exp(s_ - m)
            d_ = e.sum(axis=-1, keepdims=True)
            a_ = jnp.dot(e.astype(BF16), v_ref[...], preferred_element_type=F32)
            den = d_ if den is None else den + d_
            acc = a_ if acc is None else acc + a_
        o_ref[:, g * HEAD_DIM:(g + 1) * HEAD_DIM] = (acc / den).astype(o_ref.dtype)


def attention(qk, p, v_col0, n_batch, q_len, q_row_block0, key_segs, n_q_heads):
    q_per_kv = n_q_heads // KV_HEADS
    qw = q_per_kv * HEAD_DIM
    tq = _pick(q_len, (256, 128, 64, 32, 16, 8))
    nq = q_len // tq
    k_col0 = n_q_heads
    in_specs = [pl.BlockSpec((tq, qw), lambda b, h, i: (q_row_block0 * nq + b * nq + i, h))]
    args = [qk]
    for (ln, off) in key_segs:
        in_specs.append(pl.BlockSpec((ln, HEAD_DIM), lambda b, h, i, off=off: (off + b, k_col0 + h)))
        args.append(qk)
    for (ln, off) in key_segs:
        in_specs.append(pl.BlockSpec((ln, HEAD_DIM), lambda b, h, i, off=off: (off + b, v_col0 + h)))
        args.append(p)
    return pl.pallas_call(
        functools.partial(_attn_kernel, n_seg=len(key_segs), q_per_kv=q_per_kv, scale=HEAD_DIM ** -0.5),
        grid=(n_batch, KV_HEADS, nq),
        in_specs=in_specs,
        out_specs=pl.BlockSpec((tq, qw), lambda b, h, i: (b * nq + i, h)),
        out_shape=jax.ShapeDtypeStruct((n_batch * q_len, n_q_heads * HEAD_DIM), BF16),
        compiler_params=_cparams("arbitrary", "arbitrary", "arbitrary"),
        name="attention",
    )(*args)


def _pack_pairs(lo, hi):
    lo_b = lax.bitcast_convert_type(lo.astype(BF16).astype(F32), jnp.uint32)
    hi_b = lax.bitcast_convert_type(hi.astype(BF16).astype(F32), jnp.uint32)
    return (lo_b >> 16) | (hi_b & jnp.uint32(0xFFFF0000))


def _unpack_pairs(word):
    lo = lax.bitcast_convert_type(word << 16, F32)
    hi = lax.bitcast_convert_type(word & jnp.uint32(0xFFFF0000), F32)
    return lo, hi


def _router_kernel(x_ref, g_ref, sh_ref, sc_ref, wr_ref, br_ref, h_ref, idx_ref, wgt_ref):
    x = x_ref[...]
    y = x * lax.rsqrt(jnp.mean(x * x, axis=-1, keepdims=True) + EPS) * g_ref[...]
    h = y * (1.0 + sc_ref[...]) + sh_ref[...]
    half = h.shape[1] // 2
    h_ref[...] = _pack_pairs(h[:, 0:half], h[:, half:2 * half])
    logits = lax.dot_general(wr_ref[...], h, (((1,), (1,)), ((), ())), preferred_element_type=F32,
                             precision=lax.Precision.HIGHEST) + br_ref[...]
    ne, tm = logits.shape
    eidx = lax.broadcasted_iota(jnp.int32, (ne, tm), 0)
    vals, idxs = [], []
    cur = logits
    for _ in range(TOP_K):
        m = cur.max(axis=0, keepdims=True)
        sel = jnp.min(jnp.where(cur == m, eidx, ne), axis=0, keepdims=True)
        vals.append(m)
        idxs.append(sel)
        cur = jnp.where(eidx == sel, -jnp.inf, cur)
    es = [jnp.exp(v - vals[0]) for v in vals]
    den = es[0]
    for e in es[1:]:
        den = den + e
    zi = jnp.zeros((8 - TOP_K, tm), jnp.int32)
    zf = jnp.zeros((8 - TOP_K, tm), F32)
    idx_ref[...] = jnp.concatenate(idxs + [zi], axis=0)
    wgt_ref[...] = jnp.concatenate([e / den for e in es] + [zf], axis=0)


def norm_router(st, x, g, mod, k_shift, k_scale, w_router, b_router):
    nt, d = x.shape
    ne = w_router.shape[1]
    tm = _pick(st.tm, (256, 128))
    return pl.pallas_call(
        _router_kernel,
        grid=(nt // tm,),
        in_specs=[pl.BlockSpec((tm, d), lambda i: (i, 0)),
                  pl.BlockSpec((1, d), lambda i: (0, 0)),
                  _mod_spec(st, tm, k_shift, d, 1, 0),
                  _mod_spec(st, tm, k_scale, d, 1, 0),
                  pl.BlockSpec((ne, d), lambda i: (0, 0)),
                  pl.BlockSpec((ne, 1), lambda i: (0, 0))],
        out_specs=[pl.BlockSpec((tm, d // 2), lambda i: (i, 0)),
                   pl.BlockSpec((8, tm), lambda i: (0, i)),
                   pl.BlockSpec((8, tm), lambda i: (0, i))],
        out_shape=[jax.ShapeDtypeStruct((nt, d // 2), jnp.uint32),
                   jax.ShapeDtypeStruct((8, nt), jnp.int32),
                   jax.ShapeDtypeStruct((8, nt), F32)],
        compiler_params=_cparams("arbitrary"),
        name="norm_router",
    )(x, g.reshape(1, d), mod, mod, w_router.T, b_router.reshape(ne, 1))


def _route_plan_kernel(idx_ref, dest_ref, blk_ref, pre_ref, *, ne, tb, chunk):
    nt = idx_ref.shape[1]
    n_chunk = nt // chunk
    eidx = lax.broadcasted_iota(jnp.int32, (ne, chunk), 0)
    tri = (lax.broadcasted_iota(jnp.int32, (chunk, chunk), 0)
           < lax.broadcasted_iota(jnp.int32, (chunk, chunk), 1)).astype(BF16)

    def count(c, carry):
        c0 = pl.multiple_of(c * chunk, chunk)
        oh = jnp.zeros((ne, chunk), F32)
        for k in range(TOP_K):
            oh = oh + (eidx == idx_ref[k:k + 1, pl.ds(c0, chunk)]).astype(F32)
        pre_ref[:, pl.ds(c0, chunk)] = jnp.dot(oh.astype(BF16), tri, preferred_element_type=F32) + carry
        return carry + oh.sum(axis=1, keepdims=True)

    counts = lax.fori_loop(0, n_chunk, count, jnp.zeros((ne, 1), F32))
    padded = jnp.ceil(counts * (1.0 / tb)) * tb
    r_i = lax.broadcasted_iota(jnp.int32, (ne, ne), 0)
    c_i = lax.broadcasted_iota(jnp.int32, (ne, ne), 1)
    padded_row = jnp.sum(jnp.where(r_i == c_i, padded, 0.0), axis=0, keepdims=True)
    pstart = jnp.sum(jnp.where(c_i < r_i, padded_row, 0.0), axis=1, keepdims=True)
    pend = pstart + padded

    def place(c, carry):
        c0 = pl.multiple_of(c * chunk, chunk)
        base = pre_ref[:, pl.ds(c0, chunk)] + pstart
        rows = [jnp.sum(jnp.where(eidx == idx_ref[k:k + 1, pl.ds(c0, chunk)], base, 0.0), axis=0, keepdims=True)
                for k in range(TOP_K)]
        rows.append(jnp.zeros((8 - TOP_K, chunk), F32))
        dest_ref[:, pl.ds(c0, chunk)] = jnp.concatenate(rows, axis=0).astype(jnp.int32)
        return carry

    lax.fori_loop(0, n_chunk, place, 0)

    nbp = blk_ref.shape[1]
    starts = (lax.broadcasted_iota(jnp.int32, (ne, nbp), 1) * tb).astype(F32)
    blk_e = jnp.minimum(jnp.sum((pend <= starts).astype(F32), axis=0, keepdims=True), ne - 1.0)
    n_live = jnp.sum(padded, axis=0, keepdims=True) * (1.0 / tb)
    blk_ref[...] = jnp.concatenate([blk_e, jnp.broadcast_to(n_live, (1, nbp)), jnp.zeros((6, nbp), F32)],
                                   axis=0).astype(jnp.int32)


def route_plan(idx8, ne, tb, n_blocks):
    nt = idx8.shape[1]
    chunk = _pick(nt, (256, 128))
    nbp = -(-n_blocks // LANES) * LANES
    return pl.pallas_call(
        functools.partial(_route_plan_kernel, ne=ne, tb=tb, chunk=chunk),
        out_shape=[jax.ShapeDtypeStruct((8, nt), jnp.int32), jax.ShapeDtypeStruct((8, nbp), jnp.int32)],
        scratch_shapes=[pltpu.VMEM((ne, nt), F32)],
        compiler_params=pltpu.CompilerParams(vmem_limit_bytes=VMEM_LIMIT),
        name="route_plan",
    )(idx8)


def _dispatch_kernel(dest_ref, h_hbm, xz_hbm, xs_hbm, sem, *, rows):
    del xz_hbm
    i = pl.program_id(0)
    n = pl.num_programs(0)
    slot = i % 2
    base = i * rows

    def issue(r, carry):
        t = base + r
        for k in range(TOP_K):
            pltpu.make_async_copy(h_hbm.at[pl.ds(t, 1)], xs_hbm.at[pl.ds(dest_ref[k, t], 1)], sem.at[slot]).start()
        return carry

    lax.fori_loop(0, rows, issue, 0)

    def wait_step(s):
        pltpu.make_async_copy(h_hbm.at[pl.ds(0, rows * TOP_K)], xs_hbm.at[pl.ds(0, rows * TOP_K)], sem.at[s]).wait()

    @pl.when(i > 0)
    def _():
        wait_step(1 - slot)

    @pl.when(i == n - 1)
    def _():
        wait_step(slot)


def moe_dispatch(h_packed, dest8, cap):
    nt, w = h_packed.shape
    rows = _pick(nt, (256, 128, 64, 32, 16, 8))
    xz = jnp.zeros((cap, w), h_packed.dtype)
    return pl.pallas_call(
        functools.partial(_dispatch_kernel, rows=rows),
        grid=(nt // rows,),
        in_specs=[pl.BlockSpec(memory_space=pltpu.SMEM),
                  pl.BlockSpec(memory_space=pl.ANY),
                  pl.BlockSpec(memory_space=pl.ANY)],
        out_specs=pl.BlockSpec(memory_space=pl.ANY),
        out_shape=jax.ShapeDtypeStruct((cap, w), h_packed.dtype),
        scratch_shapes=[pltpu.SemaphoreType.DMA((2,))],
        input_output_aliases={2: 0},
        compiler_params=_cparams("arbitrary"),
        name="moe_dispatch",
    )(dest8, h_packed, xz)


def _expert_kernel(be_ref, nb_ref, x_ref, wg_ref, bg_ref, wu_ref, bu_ref, wd_ref, bd_ref, o_ref,
                   wg_bf, wu_bf, wd_bf):
    i = pl.program_id(0)
    prev = be_ref[jnp.maximum(i - 1, 0)]
    live = i < nb_ref[0]
    half = x_ref.shape[1]

    @pl.when(live & ((i == 0) | (be_ref[i] != prev)))
    def _():
        wg_bf[...] = wg_ref[...].astype(BF16)
        wu_bf[...] = wu_ref[...].astype(BF16)
        wd_bf[...] = wd_ref[...].astype(BF16)

    @pl.when(live)
    def _():
        lo, hi = _unpack_pairs(x_ref[...])
        lo = lo.astype(BF16)
        hi = hi.astype(BF16)

        def proj(w_bf, b_ref):
            return (jnp.dot(lo, w_bf[0:half, :], preferred_element_type=F32)
                    + jnp.dot(hi, w_bf[half:2 * half, :], preferred_element_type=F32) + b_ref[...])

        g = jnp.minimum(proj(wg_bf, bg_ref), SWIGLU_LIMIT)
        u = jnp.clip(proj(wu_bf, bu_ref), -SWIGLU_LIMIT, SWIGLU_LIMIT)
        a = g * jax.nn.sigmoid(SWIGLU_ALPHA * g) * (u + 1.0)
        y = jnp.dot(a.astype(BF16), wd_bf[...], preferred_element_type=F32) + bd_ref[...]
        o_ref[...] = _pack_pairs(y[:, 0:half], y[:, half:2 * half])

    @pl.when(jnp.logical_not(live))
    def _():
        o_ref[...] = jnp.zeros_like(o_ref)


def expert_ffn(x_sorted, blk_e, n_live, w_gate, b_gate, w_up, b_up, w_down, b_down):
    cap, half = x_sorted.shape
    ne, d, f = w_gate.shape
    tb = MOE_ROWS
    grid_spec = pltpu.PrefetchScalarGridSpec(
        num_scalar_prefetch=2,
        grid=(cap // tb,),
        in_specs=[pl.BlockSpec((tb, half), lambda i, be, nb: (i, 0)),
                  pl.BlockSpec((None, d, f), lambda i, be, nb: (be[i], 0, 0)),
                  pl.BlockSpec((None, 1, f), lambda i, be, nb: (be[i], 0, 0)),
                  pl.BlockSpec((None, d, f), lambda i, be, nb: (be[i], 0, 0)),
                  pl.BlockSpec((None, 1, f), lambda i, be, nb: (be[i], 0, 0)),
                  pl.BlockSpec((None, f, d), lambda i, be, nb: (be[i], 0, 0)),
                  pl.BlockSpec((None, 1, d), lambda i, be, nb: (be[i], 0, 0))],
        out_specs=pl.BlockSpec((tb, half), lambda i, be, nb: (i, 0)),
        scratch_shapes=[pltpu.VMEM((d, f), BF16), pltpu.VMEM((d, f), BF16), pltpu.VMEM((f, d), BF16)],
    )
    return pl.pallas_call(
        _expert_kernel,
        grid_spec=grid_spec,
        out_shape=jax.ShapeDtypeStruct((cap, half), jnp.uint32),
        compiler_params=_cparams("arbitrary"),
        name="expert_ffn",
    )(blk_e, n_live, x_sorted, w_gate, b_gate.reshape(ne, 1, f), w_up, b_up.reshape(ne, 1, f),
      w_down, b_down.reshape(ne, 1, d))


def _combine_kernel(dest_ref, x_ref, w_ref, gate_ref, y_hbm, o_ref, buf, sem, *, tm):
    i = pl.program_id(0)
    n = pl.num_programs(0)
    slot = i % 2
    half = buf.shape[3]

    def fetch(tile, s):
        def issue(r, carry):
            t = tile * tm + r
            for k in range(TOP_K):
                pltpu.make_async_copy(y_hbm.at[pl.ds(dest_ref[k, t], 1)], buf.at[s, k, pl.ds(r, 1)], sem.at[s]).start()
            return carry
        lax.fori_loop(0, tm, issue, 0)

    @pl.when(i == 0)
    def _():
        fetch(0, 0)

    @pl.when(i + 1 < n)
    def _():
        fetch(i + 1, 1 - slot)

    for k in range(TOP_K):
        pltpu.make_async_copy(y_hbm.at[pl.ds(0, tm)], buf.at[slot, k], sem.at[slot]).wait()

    acc_lo = None
    acc_hi = None
    for k in range(TOP_K):
        lo, hi = _unpack_pairs(buf[slot, k])
        wk = w_ref[:, k:k + 1]
        acc_lo = wk * lo if acc_lo is None else acc_lo + wk * lo
        acc_hi = wk * hi if acc_hi is None else acc_hi + wk * hi
    o_ref[:, 0:half] = x_ref[:, 0:half] + gate_ref[:, 0:half] * acc_lo
    o_ref[:, half:2 * half] = x_ref[:, half:2 * half] + gate_ref[:, half:2 * half] * acc_hi


def moe_combine(st, x, y_sorted, dest8, w_tok, mod, k_gate):
    nt, d = x.shape
    tm = _pick(st.tm, (128, 64, 32, 16, 8))
    return pl.pallas_call(
        functools.partial(_combine_kernel, tm=tm),
        grid=(nt // tm,),
        in_specs=[pl.BlockSpec(memory_space=pltpu.SMEM),
                  pl.BlockSpec((tm, d), lambda i: (i, 0)),
                  pl.BlockSpec((tm, TOP_K), lambda i: (i, 0)),
                  _mod_spec(st, tm, k_gate, d, 1, 0),
                  pl.BlockSpec(memory_space=pl.ANY)],
        out_specs=pl.BlockSpec((tm, d), lambda i: (i, 0)),
        out_shape=jax.ShapeDtypeStruct((nt, d), F32),
        scratch_shapes=[pltpu.VMEM((2, TOP_K, tm, d // 2), jnp.uint32), pltpu.SemaphoreType.DMA((2,))],
        input_output_aliases={1: 0},
        compiler_params=_cparams("arbitrary"),
        name="moe_combine",
    )(dest8, x, w_tok, mod, y_sorted)


def moe_layer(st, x, g, mod, w_router, b_router, w_gate, b_gate, w_up, b_up, w_down, b_down):
    nt, d = x.shape
    ne = w_router.shape[1]
    tb = MOE_ROWS
    h_packed, idx8, wgt8 = norm_router(st, x, g, mod, 3, 4, w_router, b_router)
    n_assign = nt * TOP_K
    cap = -(-(n_assign + ne * (tb - 1)) // tb) * tb
    n_blocks = cap // tb
    dest8, blk = route_plan(idx8, ne, tb, n_blocks)
    x_sorted = moe_dispatch(h_packed, dest8, cap)
    y_sorted = expert_ffn(x_sorted, blk[0, :n_blocks], blk[1, :1], w_gate, b_gate, w_up, b_up, w_down, b_down)
    return moe_combine(st, x, y_sorted, dest8, wgt8[:TOP_K].T, mod, 5)


def even_mixer(st, x, h, mod, w_in, w_out, s5p, d_skip, glu_w, glu_b, fnet_w, tables):
    b, s, l = st.b, st.s, st.l
    d = x.shape[1]
    fw = fnet_w.shape[0] * fnet_w.shape[1]
    sw = d - fw
    nb = sw // LANES
    t_len = l + s
    p = matmul([h], w_in, st=st)
    u_lat = p[:st.n_lat, :sw].reshape(b, s, nb, LANES)
    u_ctx = p[st.n_lat:, :sw].reshape(b, l, nb, LANES)
    seq_f = jnp.concatenate([u_ctx, u_lat], axis=1)
    seq_b = jnp.concatenate([u_ctx[:, ::-1], u_lat[:, ::-1]], axis=1)
    u8 = jnp.stack([seq_f, seq_b], axis=0).transpose(3, 2, 0, 1, 4).reshape(nb, t_len * 2 * b, LANES)
    b_bd, c_bd, lam8 = s5p
    y8 = s5_scan(u8, b_bd, c_bd, lam8, t_len)
    y8 = y8.reshape(nb, t_len, 2, b, LANES).transpose(2, 3, 1, 0, 4).reshape(2, b, t_len, sw)
    y_f, y_b = y8[0].astype(F32), y8[1].astype(F32)
    y_ctx = y_f[:, :l] + y_b[:, :l][:, ::-1]
    y_lat = y_f[:, l:] + y_b[:, l:][:, ::-1]
    y_ssm = jnp.concatenate([y_lat.reshape(st.n_lat, sw), y_ctx.reshape(st.n_ctx, sw)], axis=0)
    gl = s5_gelu(y_ssm, p, d_skip, sw)
    a = matmul([gl], glu_w, mode="glu", extra=(gl, glu_b), st=st, tn_prefs=(768, 512, 256, 128))
    cd, sd, (ct_s, sn_s), (ct_l, sn_l) = tables
    pcs = fnet_chan(p, sw // fw, fnet_w, cd, sd)
    f_lat = fnet_seq(pcs, ct_s, sn_s, b, s, 0)
    f_ctx = fnet_seq(pcs, ct_l, sn_l, b, l, st.n_lat // l)
    fm = jnp.concatenate([f_lat, f_ctx], axis=0)
    return matmul([a, fm], w_out, mode="resid", extra=(x, mod, 2), st=st)


def odd_mixer(st, x, h, mod, w_in, w_out, pool_w, pool_scale, q_gain, k_gain, rope):
    b, s, l = st.b, st.s, st.l
    d = x.shape[1]
    pw = pool_w.shape[0] * pool_w.shape[1]
    n_q = (d - pw) // HEAD_DIM
    p = matmul([h], w_in, st=st)
    cos_t, sin_t = rope
    gains = jnp.stack([q_gain, k_gain]).reshape(2, 1, HEAD_DIM)
    qk = qk_prep(st, p, gains, cos_t, sin_t, n_q // KV_HEADS)
    v_col0 = (pw + n_q * HEAD_DIM + KV_HEADS * HEAD_DIM) // HEAD_DIM
    att_lat = attention(qk, p, v_col0, b, s, 0, [(l, st.n_lat // l), (s, 0)], n_q)
    att_ctx = attention(qk, p, v_col0, b, l, st.n_lat // l, [(l, st.n_lat // l)], n_q)
    att = jnp.concatenate([att_lat, att_ctx], axis=0)
    pool_lat = pool_mix(p, pool_w, pool_scale, b, s, 0)
    pool_ctx = pool_mix(p, pool_w, pool_scale, b, l, st.n_lat // l)
    pm = jnp.concatenate([pool_lat, pool_ctx], axis=0)
    return matmul([pm, att], w_out, mode="resid", extra=(x, mod, 2), st=st)


def kernel(x, c, ctx, c_ctx, ada_w, ada_b, norm_mix_g, norm_ffn_g, ev_w_in, ev_w_out, s5_lam_re, s5_lam_im, s5_log_dt, s5_b_re, s5_b_im, s5_c_re, s5_c_im, s5_d, s5_glu_w, s5_glu_b, fnet_w, od_w_in, od_w_out, pool_w, pool_scale, q_gain, k_gain, router_w, router_b, exp_w_gate, exp_b_gate, exp_w_up, exp_b_up, exp_w_down, exp_b_down, final_g):
    batch, seq, d = x.shape
    ctx_len = ctx.shape[1]
    depth = ada_w.shape[0]
    st = Stream(batch, seq, ctx_len)
    xs = jnp.concatenate([x.reshape(st.n_lat, d), ctx.reshape(st.n_ctx, d)], axis=0)

    cond8 = jnp.concatenate([c, c_ctx[None], jnp.zeros((8 - batch - 1, d), F32)], axis=0)
    mod_all = adaln_all(cond8, ada_w, ada_b)
    mod_all = mod_all.reshape(depth, 8, 6, 1, d).transpose(0, 2, 1, 3, 4)

    dh = fnet_w.shape[2]
    cd, sd = dft_tables(dh)
    tab_s = tuple(t.astype(BF16) for t in dft_tables(seq))
    tab_l = tuple(t.astype(BF16) for t in dft_tables(ctx_len))
    tables = (cd, sd, tab_s, tab_l)
    rope = rope_tables(seq, _pick(st.tm, (512, 256, 128, 64, 32, 16, 8)))

    for layer in range(depth):
        i = layer // 2
        mod = mod_all[layer]
        h = norm_mod(st, xs, norm_mix_g[layer], mod, 0, 1)
        if layer % 2 == 0:
            s5p = s5_tables(s5_lam_re[i], s5_lam_im[i], s5_log_dt[i], s5_b_re[i], s5_b_im[i], s5_c_re[i], s5_c_im[i])
            xs = even_mixer(st, xs, h, mod, ev_w_in[i], ev_w_out[i], s5p, s5_d[i], s5_glu_w[i], s5_glu_b[i],
                            fnet_w[i], tables)
        else:
            xs = odd_mixer(st, xs, h, mod, od_w_in[i], od_w_out[i], pool_w[i], pool_scale[i], q_gain[i], k_gain[i],
                           rope)
        xs = moe_layer(st, xs, norm_ffn_g[layer], mod, router_w[layer], router_b[layer],
                       exp_w_gate[layer], exp_b_gate[layer], exp_w_up[layer], exp_b_up[layer],
                       exp_w_down[layer], exp_b_down[layer])
    return final_norm(xs[:st.n_lat], final_g).reshape(batch, seq, d)
```

```python
import functools
import math

import jax
import jax.numpy as jnp
from jax import lax
from jax.experimental import pallas as pl
from jax.experimental.pallas import tpu as pltpu

F32 = jnp.float32
BF16 = jnp.bfloat16
EPS = 1e-6

GRID_W = 64
FNET_HEADS = 4
S5_GROUP = 16
S5_STATE = 64
POOL_WINDOWS = (2, 4, 8, 16)
HEAD_DIM = 128
KV_HEADS = 4
ROPE_THETA = 10000.0
TOP_K = 4
SWIGLU_LIMIT = 7.0
SWIGLU_ALPHA = 1.702

LANES = 128
S5_BLOCK_GROUPS = LANES // S5_GROUP
VMEM_LIMIT = 48 * 1024 * 1024
MOE_ROWS = 256


def _cparams(*sem):
    return pltpu.CompilerParams(dimension_semantics=sem, vmem_limit_bytes=VMEM_LIMIT)


def _pick(n, prefs):
    for p in prefs:
        if n % p == 0:
            return p
    return n


def _adaln_kernel(c_ref, w_ref, b_ref, o_ref):
    c = c_ref[...]
    a = (c * jax.nn.sigmoid(c)).astype(BF16)
    o_ref[...] = jnp.dot(a, w_ref[...].astype(BF16), preferred_element_type=F32) + b_ref[...]


def adaln_all(cond8, ada_w, ada_b):
    depth, d, n = ada_w.shape
    tn = _pick(n, (1024, 512, 256, 128))
    return pl.pallas_call(
        _adaln_kernel,
        grid=(depth, n // tn),
        in_specs=[pl.BlockSpec((8, d), lambda l, j: (0, 0)),
                  pl.BlockSpec((None, d, tn), lambda l, j: (l, 0, j)),
                  pl.BlockSpec((None, 1, tn), lambda l, j: (l, 0, j))],
        out_specs=pl.BlockSpec((None, 8, tn), lambda l, j: (l, 0, j)),
        out_shape=jax.ShapeDtypeStruct((depth, 8, n), F32),
        compiler_params=_cparams("arbitrary", "arbitrary"),
        name="adaln",
    )(cond8, ada_w, ada_b.reshape(depth, 1, n))


def _norm_mod_kernel(x_ref, g_ref, sh_ref, sc_ref, o_ref):
    x = x_ref[...]
    y = x * lax.rsqrt(jnp.mean(x * x, axis=-1, keepdims=True) + EPS) * g_ref[...]
    o_ref[...] = (y * (1.0 + sc_ref[...]) + sh_ref[...]).astype(o_ref.dtype)


def _norm_kernel(x_ref, g_ref, o_ref):
    x = x_ref[...]
    o_ref[...] = (x * lax.rsqrt(jnp.mean(x * x, axis=-1, keepdims=True) + EPS) * g_ref[...]).astype(o_ref.dtype)


class Stream:
    def __init__(self, batch, seq, ctx_len):
        self.b, self.s, self.l = batch, seq, ctx_len
        self.t = seq + ctx_len
        self.nt = batch * self.t
        self.tm = _pick(math.gcd(seq, ctx_len), (256, 128, 64, 32, 16))

    def mod_row(self, tm):
        tpb, l, b = self.t // tm, self.l, self.b
        return lambda i: jnp.where((i % tpb) * tm < l, b, i // tpb)


def _mod_spec(st, tm, k, d, grid_rank, row_axis):
    mr = st.mod_row(tm)
    if grid_rank == 1:
        return pl.BlockSpec((None, None, 1, d), lambda i: (k, mr(i), 0, 0))
    if row_axis == 1:
        return pl.BlockSpec((None, None, 1, d), lambda j, i: (k, mr(i), 0, j))
    raise ValueError


def norm_mod(st, x, g, mod, k_shift, k_scale, out_dtype=BF16):
    nt, d = x.shape
    tm = _pick(st.tm, (256, 128, 64, 32, 16))
    return pl.pallas_call(
        _norm_mod_kernel,
        grid=(nt // tm,),
        in_specs=[pl.BlockSpec((tm, d), lambda i: (i, 0)),
                  pl.BlockSpec((1, d), lambda i: (0, 0)),
                  _mod_spec(st, tm, k_shift, d, 1, 0),
                  _mod_spec(st, tm, k_scale, d, 1, 0)],
        out_specs=pl.BlockSpec((tm, d), lambda i: (i, 0)),
        out_shape=jax.ShapeDtypeStruct((nt, d), out_dtype),
        compiler_params=_cparams("arbitrary"),
        name="norm_mod",
    )(x, g.reshape(1, d), mod, mod)


def final_norm(st, x, g):
    d = x.shape[1]
    tm = st.tm
    tpb, lt, ns = st.t // tm, st.l // tm, st.s // tm
    return pl.pallas_call(
        _norm_kernel,
        grid=(st.b, ns),
        in_specs=[pl.BlockSpec((tm, d), lambda b, i: (b * tpb + lt + i, 0)),
                  pl.BlockSpec((1, d), lambda b, i: (0, 0))],
        out_specs=pl.BlockSpec((tm, d), lambda b, i: (b * ns + i, 0)),
        out_shape=jax.ShapeDtypeStruct((st.b * st.s, d), F32),
        compiler_params=_cparams("arbitrary", "arbitrary"),
        name="final_norm",
    )(x, g.reshape(1, d))


def _mm_kernel(*refs, n_a, k_offs, mode):
    a_refs = refs[:n_a]
    w_ref = refs[n_a]
    rest = refs[n_a + 1:]
    wbf_ref = rest[-1]
    o_ref = rest[-2]

    @pl.when(pl.program_id(1) == 0)
    def _():
        wbf_ref[...] = w_ref[...].astype(BF16)

    acc = None
    for a_ref, (k0, k1) in zip(a_refs, k_offs):
        part = jnp.dot(a_ref[...], wbf_ref[k0:k1, :], preferred_element_type=F32)
        acc = part if acc is None else acc + part
    if mode == "plain":
        o_ref[...] = acc.astype(o_ref.dtype)
    elif mode == "resid":
        x_ref, gate_ref = rest[0], rest[1]
        o_ref[...] = x_ref[...] + gate_ref[...] * acc
    elif mode == "glu":
        g_ref, b_ref = rest[0], rest[1]
        o_ref[...] = (g_ref[...].astype(F32) * jax.nn.sigmoid(acc + b_ref[...])).astype(o_ref.dtype)
    else:
        raise ValueError(mode)


def matmul(a_parts, w, mode="plain", out_dtype=BF16, extra=(), st=None, tn_prefs=(2048, 1536, 1024, 512, 256, 128)):
    m = a_parts[0].shape[0]
    k, n = w.shape
    tm = _pick(m if st is None else st.tm, (512, 256, 128, 64, 32, 16, 8))
    tn = _pick(n, tn_prefs)
    k_offs, off = [], 0
    in_specs, args = [], []
    for a in a_parts:
        ka = a.shape[1]
        k_offs.append((off, off + ka))
        off += ka
        in_specs.append(pl.BlockSpec((tm, ka), lambda j, i: (i, 0)))
        args.append(a)
    assert off == k
    in_specs.append(pl.BlockSpec((k, tn), lambda j, i: (0, j), pipeline_mode=pl.Buffered(1)))
    args.append(w)
    io_alias = {}
    if mode == "resid":
        x, mod, k_gate = extra
        in_specs.append(pl.BlockSpec((tm, tn), lambda j, i: (i, j)))
        args.append(x)
        in_specs.append(_mod_spec(st, tm, k_gate, tn, 2, 1))
        args.append(mod)
        io_alias = {len(args) - 2: 0}
        out_dtype = F32
    elif mode == "glu":
        g, bias = extra
        in_specs.append(pl.BlockSpec((tm, tn), lambda j, i: (i, j)))
        args.append(g)
        in_specs.append(pl.BlockSpec((1, tn), lambda j, i: (0, j)))
        args.append(bias.reshape(1, n))
    return pl.pallas_call(
        functools.partial(_mm_kernel, n_a=len(a_parts), k_offs=tuple(k_offs), mode=mode),
        grid=(n // tn, m // tm),
        in_specs=in_specs,
        out_specs=pl.BlockSpec((tm, tn), lambda j, i: (i, j)),
        out_shape=jax.ShapeDtypeStruct((m, n), out_dtype),
        scratch_shapes=[pltpu.VMEM((k, tn), BF16)],
        input_output_aliases=io_alias,
        compiler_params=_cparams("arbitrary", "arbitrary"),
        name="mm_" + mode,
    )(*args)


def _s5_kernel(uf_ref, ub_ref, b_ref, c_ref, lam_ref, yf_ref, yb_ref, h_ref, bu_ref, u_scr, y_scr, *, tc):
    half = b_ref.shape[1] // 2
    nbat = uf_ref.shape[0]

    @pl.when(pl.program_id(1) == 0)
    def _():
        h_ref[...] = jnp.zeros_like(h_ref)

    rows = tc * 8
    is_fwd = (lax.broadcasted_iota(jnp.int32, (rows, LANES), 0) % 8) < 4
    flip = (lax.broadcasted_iota(jnp.int32, (tc, tc), 0)
            + lax.broadcasted_iota(jnp.int32, (tc, tc), 1) == tc - 1).astype(BF16)
    for b in range(nbat):
        u_scr[pl.ds(b, tc, stride=8), :] = uf_ref[b].astype(F32)
        u_scr[pl.ds(nbat + b, tc, stride=8), :] = jnp.dot(flip, ub_ref[b], preferred_element_type=F32)
    u = u_scr[...]
    lhs = jnp.concatenate([jnp.where(is_fwd, u, 0.0), jnp.where(is_fwd, 0.0, u)], axis=1).astype(BF16)
    bu_ref[...] = jnp.dot(lhs, b_ref[...], preferred_element_type=F32)

    lam_r = lam_ref[0]
    lam_i = lam_ref[1]

    def step(t, carry):
        hr, hi = carry
        r0 = pl.multiple_of(t * 8, 8)
        nhr = lam_r * hr - lam_i * hi + bu_ref[pl.ds(r0, 8), 0:half]
        nhi = lam_r * hi + lam_i * hr + bu_ref[pl.ds(r0, 8), half:2 * half]
        bu_ref[pl.ds(r0, 8), 0:half] = nhr
        bu_ref[pl.ds(r0, 8), half:2 * half] = nhi
        return nhr, nhi

    hr, hi = lax.fori_loop(0, tc, step, (h_ref[:, 0:half], h_ref[:, half:2 * half]), unroll=2)
    h_ref[:, 0:half] = hr
    h_ref[:, half:2 * half] = hi

    y2 = jnp.dot(bu_ref[...].astype(BF16), c_ref[...], preferred_element_type=F32)
    y_scr[...] = jnp.where(is_fwd, y2[:, 0:LANES], y2[:, LANES:2 * LANES])
    for b in range(nbat):
        yf_ref[b] = y_scr[pl.ds(b, tc, stride=8), :].astype(yf_ref.dtype)
        yb = y_scr[pl.ds(nbat + b, tc, stride=8), :].astype(BF16)
        yb_ref[b] = jnp.dot(flip, yb, preferred_element_type=F32).astype(yb_ref.dtype)


def s5_scan(st, p, width, b_bd, c_bd, lam8):
    bsz, t_len, l_len = st.b, st.t, st.l
    assert 2 * bsz == 8
    d = p.shape[1]
    nb = width // LANES
    tc = _pick(math.gcd(st.s, l_len), (128, 64, 32, 16, 8))
    rows = tc * 8
    ns = b_bd.shape[2]
    n_l, n_c = l_len // tc, t_len // tc

    def bwd_chunk(c):
        return jnp.where(c < n_l, n_l - 1 - c, n_c + n_l - 1 - c)

    p3 = p.reshape(bsz, t_len, d)
    out = jax.ShapeDtypeStruct((bsz, t_len, width), BF16)
    y_f, y_b = pl.pallas_call(
        functools.partial(_s5_kernel, tc=tc),
        grid=(nb, n_c),
        in_specs=[pl.BlockSpec((bsz, tc, LANES), lambda j, c: (0, c, j)),
                  pl.BlockSpec((bsz, tc, LANES), lambda j, c: (0, bwd_chunk(c), j)),
                  pl.BlockSpec((None, 2 * LANES, ns), lambda j, c: (j, 0, 0)),
                  pl.BlockSpec((None, ns, 2 * LANES), lambda j, c: (j, 0, 0)),
                  pl.BlockSpec((None, 2, 8, ns // 2), lambda j, c: (j, 0, 0, 0))],
        out_specs=[pl.BlockSpec((bsz, tc, LANES), lambda j, c: (0, c, j)),
                   pl.BlockSpec((bsz, tc, LANES), lambda j, c: (0, bwd_chunk(c), j))],
        out_shape=[out, out],
        scratch_shapes=[pltpu.VMEM((8, ns), F32), pltpu.VMEM((rows, ns), F32),
                        pltpu.VMEM((rows, LANES), F32), pltpu.VMEM((rows, LANES), F32)],
        compiler_params=_cparams("arbitrary", "arbitrary"),
        name="s5_scan",
    )(p3, p3, b_bd, c_bd, lam8)
    return y_f.reshape(bsz * t_len, width), y_b.reshape(bsz * t_len, width)


def _s5_gelu_kernel(yf_ref, yb_ref, u_ref, d_ref, o_ref):
    y = yf_ref[...].astype(F32) + yb_ref[...].astype(F32) + d_ref[...] * u_ref[...].astype(F32)
    o_ref[...] = jax.nn.gelu(y).astype(o_ref.dtype)


def s5_gelu(y_f, y_b, p, d_skip, width):
    nt = y_f.shape[0]
    tm = _pick(nt, (512, 256, 128, 64, 32, 16, 8))
    row = pl.BlockSpec((tm, width), lambda i: (i, 0))
    return pl.pallas_call(
        _s5_gelu_kernel,
        grid=(nt // tm,),
        in_specs=[row, row, row, pl.BlockSpec((1, width), lambda i: (0, 0))],
        out_specs=row,
        out_shape=jax.ShapeDtypeStruct((nt, width), BF16),
        compiler_params=_cparams("arbitrary"),
        name="s5_gelu",
    )(y_f, y_b, p, d_skip.reshape(1, width))


def s5_tables(lam_re, lam_im, log_dt, b_re, b_im, c_re, c_im):
    _, g, p = lam_re.shape
    c = b_re.shape[-1]
    nb = g // S5_BLOCK_GROUPS
    gb = S5_BLOCK_GROUPS
    lam = lax.complex(lam_re.astype(F32), lam_im.astype(F32))
    dt = jnp.exp(log_dt.astype(F32))[..., None]
    lam_bar = jnp.exp(lam * dt)
    b_bar = ((lam_bar - 1.0) / lam)[..., None] * lax.complex(b_re.astype(F32), b_im.astype(F32))
    eye = jnp.eye(gb, dtype=F32)
    bb = b_bar.reshape(2, nb, gb, p, c)

    def b_lay(z):
        return jnp.einsum('djgpc,gh->jdgchp', z, eye).reshape(nb, 2 * gb * c, gb * p)

    b_bd = jnp.concatenate([b_lay(bb.real), b_lay(bb.imag)], axis=-1).astype(BF16)
    cc = lax.complex(c_re.astype(F32), c_im.astype(F32)).reshape(2, nb, gb, c, p)

    def c_lay(z):
        return jnp.einsum('djgcp,gh->jgpdhc', z, eye).reshape(nb, gb * p, 2 * gb * c)

    c_bd = jnp.concatenate([c_lay(cc.real), -c_lay(cc.imag)], axis=1).astype(BF16)
    lb = lam_bar.reshape(2, nb, gb * p)
    lam8 = jnp.stack([lb.real, lb.imag], axis=0)
    lam8 = jnp.repeat(lam8.transpose(2, 0, 1, 3), 4, axis=2)
    return b_bd, c_bd, lam8


def _fnet_chan_kernel(f_ref, cd_ref, sd_ref, w_ref, o_ref, wc_ref):
    nh, dh = w_ref.shape[0], w_ref.shape[1]

    @pl.when(pl.program_id(0) == 0)
    def _():
        for h in range(nh):
            wh = w_ref[h]
            wc_ref[h, :, 0:dh] = jnp.dot(cd_ref[...], wh, preferred_element_type=F32,
                                         precision=lax.Precision.HIGHEST).astype(BF16)
            wc_ref[h, :, dh:2 * dh] = jnp.dot(sd_ref[...], wh, preferred_element_type=F32,
                                              precision=lax.Precision.HIGHEST).astype(BF16)

    for h in range(nh):
        r = jnp.dot(f_ref[:, h * dh:(h + 1) * dh], wc_ref[h], preferred_element_type=F32)
        o_ref[:, h * dh:(h + 1) * dh] = r[:, 0:dh].astype(o_ref.dtype)
        o_ref[:, (nh + h) * dh:(nh + h + 1) * dh] = r[:, dh:2 * dh].astype(o_ref.dtype)


def fnet_chan(p, col_block, fnet_w, cd, sd):
    nt = p.shape[0]
    nh, dh, _ = fnet_w.shape
    width = nh * dh
    tm = _pick(nt, (512, 256, 128, 64, 32, 16, 8))
    return pl.pallas_call(
        _fnet_chan_kernel,
        grid=(nt // tm,),
        in_specs=[pl.BlockSpec((tm, width), lambda i: (i, col_block)),
                  pl.BlockSpec((dh, dh), lambda i: (0, 0)),
                  pl.BlockSpec((dh, dh), lambda i: (0, 0)),
                  pl.BlockSpec((nh, dh, dh), lambda i: (0, 0, 0))],
        out_specs=pl.BlockSpec((tm, 2 * width), lambda i: (i, 0)),
        out_shape=jax.ShapeDtypeStruct((nt, 2 * width), BF16),
        scratch_shapes=[pltpu.VMEM((nh, dh, 2 * dh), BF16)],
        compiler_params=_cparams("arbitrary"),
        name="fnet_chan",
    )(p, cd, sd, fnet_w)


def _fnet_seq_kernel(cl_ref, sl_ref, cs_ref, ss_ref, p_ref, o_ref, *, l_len):
    w = o_ref.shape[1]
    tm = o_ref.shape[0]
    t_len = p_ref.shape[0]
    in_ctx = pl.program_id(1) * tm < l_len

    @pl.when(in_ctx)
    def _():
        o_ref[...] = (jnp.dot(cl_ref[...], p_ref[0:l_len, 0:w], preferred_element_type=F32)
                      - jnp.dot(sl_ref[...], p_ref[0:l_len, w:2 * w], preferred_element_type=F32)).astype(o_ref.dtype)

    @pl.when(jnp.logical_not(in_ctx))
    def _():
        o_ref[...] = (jnp.dot(cs_ref[...], p_ref[l_len:t_len, 0:w], preferred_element_type=F32)
                      - jnp.dot(ss_ref[...], p_ref[l_len:t_len, w:2 * w], preferred_element_type=F32)).astype(o_ref.dtype)


def fnet_seq(st, pcs, tab_l, tab_s):
    w2 = pcs.shape[1]
    tm = st.tm
    tpb, lt = st.t // tm, st.l // tm
    ctx_tile = lambda b, i: (jnp.minimum(i, lt - 1), 0)
    lat_tile = lambda b, i: (jnp.maximum(i - lt, 0), 0)
    return pl.pallas_call(
        functools.partial(_fnet_seq_kernel, l_len=st.l),
        grid=(st.b, tpb),
        in_specs=[pl.BlockSpec((tm, st.l), ctx_tile), pl.BlockSpec((tm, st.l), ctx_tile),
                  pl.BlockSpec((tm, st.s), lat_tile), pl.BlockSpec((tm, st.s), lat_tile),
                  pl.BlockSpec((st.t, w2), lambda b, i: (b, 0))],
        out_specs=pl.BlockSpec((tm, w2 // 2), lambda b, i: (b * tpb + i, 0)),
        out_shape=jax.ShapeDtypeStruct((st.nt, w2 // 2), BF16),
        compiler_params=_cparams("arbitrary", "arbitrary"),
        name="fnet_seq",
    )(tab_l[0], tab_l[1], tab_s[0], tab_s[1], pcs)


def dft_tables(n):
    k = jnp.arange(n, dtype=jnp.int32)
    kn = (k[:, None] * k[None, :]) % n
    ang = kn.astype(F32) * (2.0 * math.pi / n)
    scale = 1.0 / math.sqrt(n)
    return jnp.cos(ang) * scale, jnp.sin(ang) * scale


def _pool_kernel(v_ref, w_ref, sc_ref, o_ref, *, windows, l_len):
    dh = w_ref.shape[1]
    for r0, r1 in ((0, l_len), (l_len, v_ref.shape[0])):
        t_len = r1 - r0
        t = lax.broadcasted_iota(jnp.int32, (t_len, dh), 0)
        for g, win in enumerate(windows):
            v = v_ref[r0:r1, g * dh:(g + 1) * dh].astype(F32)
            lo = win // 2
            acc = jnp.zeros_like(v)
            for j in range(-lo, win - lo):
                src = t + j
                shifted = v if j == 0 else pltpu.roll(v, (-j) % t_len, 0)
                acc = acc + jnp.where((src >= 0) & (src < t_len), shifted, 0.0)
            cnt = jnp.clip(t + (win - lo), 0, t_len) - jnp.clip(t - lo, 0, t_len)
            pooled = acc / cnt.astype(F32) - v
            y = jnp.dot(pooled.astype(BF16), w_ref[g].astype(BF16), preferred_element_type=F32)
            o_ref[r0:r1, g * dh:(g + 1) * dh] = (y * sc_ref[:, g * dh:(g + 1) * dh]).astype(o_ref.dtype)


def pool_mix(st, p, pool_w, pool_scale):
    ng, dh, _ = pool_w.shape
    width = ng * dh
    return pl.pallas_call(
        functools.partial(_pool_kernel, windows=POOL_WINDOWS, l_len=st.l),
        grid=(st.b,),
        in_specs=[pl.BlockSpec((st.t, width), lambda b: (b, 0)),
                  pl.BlockSpec((ng, dh, dh), lambda b: (0, 0, 0)),
                  pl.BlockSpec((1, width), lambda b: (0, 0))],
        out_specs=pl.BlockSpec((st.t, width), lambda b: (b, 0)),
        out_shape=jax.ShapeDtypeStruct((st.nt, width), BF16),
        compiler_params=_cparams("arbitrary"),
        name="pool_mix",
    )(p, pool_w, pool_scale.reshape(1, width))


def _qk_prep_kernel(x_ref, g_ref, cos_ref, sin_ref, o_ref):
    nh = x_ref.shape[1] // HEAD_DIM
    gain = g_ref[...]
    cos = cos_ref[...]
    sin = sin_ref[...]
    lane = lax.broadcasted_iota(jnp.int32, (x_ref.shape[0], HEAD_DIM), 1)
    first = (lane % (HEAD_DIM // 2)) < (HEAD_DIM // 4)
    for h in range(nh):
        x = x_ref[:, h * HEAD_DIM:(h + 1) * HEAD_DIM].astype(F32)
        n = x * lax.rsqrt(jnp.mean(x * x, axis=-1, keepdims=True) + EPS) * gain
        partner = jnp.where(first, pltpu.roll(n, HEAD_DIM - HEAD_DIM // 4, 1), pltpu.roll(n, HEAD_DIM // 4, 1))
        o_ref[:, h * HEAD_DIM:(h + 1) * HEAD_DIM] = (n * cos + partner * sin).astype(o_ref.dtype)


def qk_prep(st, p, gains, cos_t, sin_t, q_blocks):
    nt = p.shape[0]
    cw = KV_HEADS * HEAD_DIM
    tm = st.tm
    tpb, lt, ns = st.t // tm, st.l // tm, st.s // tm
    rope_blk = lambda i: jnp.where(i % tpb < lt, ns, i % tpb - lt)
    return pl.pallas_call(
        _qk_prep_kernel,
        grid=(nt // tm, q_blocks + 1),
        in_specs=[pl.BlockSpec((tm, cw), lambda i, j: (i, j + 1)),
                  pl.BlockSpec((None, 1, HEAD_DIM), lambda i, j: (j // q_blocks, 0, 0)),
                  pl.BlockSpec((tm, HEAD_DIM), lambda i, j: (rope_blk(i), 0)),
                  pl.BlockSpec((tm, HEAD_DIM), lambda i, j: (rope_blk(i), 0))],
        out_specs=pl.BlockSpec((tm, cw), lambda i, j: (i, j)),
        out_shape=jax.ShapeDtypeStruct((nt, (q_blocks + 1) * cw), BF16),
        compiler_params=_cparams("arbitrary", "arbitrary"),
        name="qk_prep",
    )(p, gains, cos_t, sin_t)


def rope_tables(s, tm):
    rows = s // GRID_W
    row = jnp.repeat(jnp.arange(rows), GRID_W)
    col = jnp.tile(jnp.arange(GRID_W), rows)
    nf = HEAD_DIM // 4
    inv = jnp.power(ROPE_THETA, -jnp.arange(nf, dtype=F32) / nf)
    a_row = row.astype(F32)[:, None] * inv
    a_col = col.astype(F32)[:, None] * inv
    ang = jnp.concatenate([a_row, a_row, a_col, a_col], axis=1)
    sign = jnp.tile(jnp.concatenate([-jnp.ones((nf,), F32), jnp.ones((nf,), F32)]), 2)
    cos_t = jnp.concatenate([jnp.cos(ang), jnp.ones((tm, HEAD_DIM), F32)], axis=0)
    sin_t = jnp.concatenate([jnp.sin(ang) * sign, jnp.zeros((tm, HEAD_DIM), F32)], axis=0)
    return cos_t, sin_t


def _attn_kernel(q_ref, k_ref, v_ref, o_ref, *, q_per_kv, scale, l_len):
    tq = q_ref.shape[0]
    dn = (((1,), (1,)), ((), ()))

    def run(n_keys):
        k = k_ref[0:n_keys, :]
        v = v_ref[0:n_keys, :]
        for g in range(q_per_kv):
            q = q_ref[:, g * HEAD_DIM:(g + 1) * HEAD_DIM]
            s_ = lax.dot_general(q, k, dn, preferred_element_type=F32) * scale
            e = jnp.exp(s_ - s_.max(axis=-1, keepdims=True))
            acc = jnp.dot(e.astype(BF16), v, preferred_element_type=F32)
            o_ref[:, g * HEAD_DIM:(g + 1) * HEAD_DIM] = (acc / e.sum(axis=-1, keepdims=True)).astype(o_ref.dtype)

    in_ctx = pl.program_id(2) * tq < l_len

    @pl.when(in_ctx)
    def _():
        run(l_len)

    @pl.when(jnp.logical_not(in_ctx))
    def _():
        run(k_ref.shape[0])


def attention(st, qk, p, v_col0, n_q_heads):
    q_per_kv = n_q_heads // KV_HEADS
    qw = q_per_kv * HEAD_DIM
    tq = st.tm
    tpb = st.t // tq
    k_col0 = n_q_heads
    return pl.pallas_call(
        functools.partial(_attn_kernel, q_per_kv=q_per_kv, scale=HEAD_DIM ** -0.5, l_len=st.l),
        grid=(st.b, KV_HEADS, tpb),
        in_specs=[pl.BlockSpec((tq, qw), lambda b, h, i: (b * tpb + i, h)),
                  pl.BlockSpec((st.t, HEAD_DIM), lambda b, h, i: (b, k_col0 + h)),
                  pl.BlockSpec((st.t, HEAD_DIM), lambda b, h, i: (b, v_col0 + h))],
        out_specs=pl.BlockSpec((tq, qw), lambda b, h, i: (b * tpb + i, h)),
        out_shape=jax.ShapeDtypeStruct((st.nt, n_q_heads * HEAD_DIM), BF16),
        compiler_params=_cparams("arbitrary", "arbitrary", "arbitrary"),
        name="attention",
    )(qk, qk, p)


def _pack_pairs(lo, hi):
    lo_b = lax.bitcast_convert_type(lo.astype(BF16).astype(F32), jnp.uint32)
    hi_b = lax.bitcast_convert_type(hi.astype(BF16).astype(F32), jnp.uint32)
    return (lo_b >> 16) | (hi_b & jnp.uint32(0xFFFF0000))


def _unpack_pairs(word):
    lo = lax.bitcast_convert_type(word << 16, F32)
    hi = lax.bitcast_convert_type(word & jnp.uint32(0xFFFF0000), F32)
    return lo, hi


def _router_kernel(x_ref, g_ref, sh_ref, sc_ref, wr_ref, br_ref, h_ref, idx_ref, wgt_ref):
    x = x_ref[...]
    y = x * lax.rsqrt(jnp.mean(x * x, axis=-1, keepdims=True) + EPS) * g_ref[...]
    h = y * (1.0 + sc_ref[...]) + sh_ref[...]
    half = h.shape[1] // 2
    h_ref[...] = _pack_pairs(h[:, 0:half], h[:, half:2 * half])
    logits = lax.dot_general(wr_ref[...], h, (((1,), (1,)), ((), ())), preferred_element_type=F32,
                             precision=lax.Precision.HIGHEST) + br_ref[...]
    ne, tm = logits.shape
    eidx = lax.broadcasted_iota(jnp.int32, (ne, tm), 0)
    vals, idxs = [], []
    cur = logits
    for _ in range(TOP_K):
        m = cur.max(axis=0, keepdims=True)
        sel = jnp.min(jnp.where(cur == m, eidx, ne), axis=0, keepdims=True)
        vals.append(m)
        idxs.append(sel)
        cur = jnp.where(eidx == sel, -jnp.inf, cur)
    es = [jnp.exp(v - vals[0]) for v in vals]
    den = es[0]
    for e in es[1:]:
        den = den + e
    zi = jnp.zeros((8 - TOP_K, tm), jnp.int32)
    zf = jnp.zeros((8 - TOP_K, tm), F32)
    idx_ref[...] = jnp.concatenate(idxs + [zi], axis=0)
    wgt_ref[...] = jnp.concatenate([e / den for e in es] + [zf], axis=0)


def norm_router(st, x, g, mod, k_shift, k_scale, w_router, b_router):
    nt, d = x.shape
    ne = w_router.shape[1]
    tm = _pick(st.tm, (256, 128))
    return pl.pallas_call(
        _router_kernel,
        grid=(nt // tm,),
        in_specs=[pl.BlockSpec((tm, d), lambda i: (i, 0)),
                  pl.BlockSpec((1, d), lambda i: (0, 0)),
                  _mod_spec(st, tm, k_shift, d, 1, 0),
                  _mod_spec(st, tm, k_scale, d, 1, 0),
                  pl.BlockSpec((ne, d), lambda i: (0, 0)),
                  pl.BlockSpec((ne, 1), lambda i: (0, 0))],
        out_specs=[pl.BlockSpec((tm, d // 2), lambda i: (i, 0)),
                   pl.BlockSpec((8, tm), lambda i: (0, i)),
                   pl.BlockSpec((8, tm), lambda i: (0, i))],
        out_shape=[jax.ShapeDtypeStruct((nt, d // 2), jnp.uint32),
                   jax.ShapeDtypeStruct((8, nt), jnp.int32),
                   jax.ShapeDtypeStruct((8, nt), F32)],
        compiler_params=_cparams("arbitrary"),
        name="norm_router",
    )(x, g.reshape(1, d), mod, mod, w_router.T, b_router.reshape(ne, 1))


def _route_plan_kernel(idx_ref, dest_ref, blk_ref, pre_ref, *, ne, tb, chunk):
    nt = idx_ref.shape[1]
    n_chunk = nt // chunk
    eidx = lax.broadcasted_iota(jnp.int32, (ne, chunk), 0)
    tri = (lax.broadcasted_iota(jnp.int32, (chunk, chunk), 0)
           < lax.broadcasted_iota(jnp.int32, (chunk, chunk), 1)).astype(BF16)

    def count(c, carry):
        c0 = pl.multiple_of(c * chunk, chunk)
        oh = jnp.zeros((ne, chunk), F32)
        for k in range(TOP_K):
            oh = oh + (eidx == idx_ref[k:k + 1, pl.ds(c0, chunk)]).astype(F32)
        pre_ref[:, pl.ds(c0, chunk)] = jnp.dot(oh.astype(BF16), tri, preferred_element_type=F32) + carry
        return carry + oh.sum(axis=1, keepdims=True)

    counts = lax.fori_loop(0, n_chunk, count, jnp.zeros((ne, 1), F32))
    padded = jnp.ceil(counts * (1.0 / tb)) * tb
    r_i = lax.broadcasted_iota(jnp.int32, (ne, ne), 0)
    c_i = lax.broadcasted_iota(jnp.int32, (ne, ne), 1)
    padded_row = jnp.sum(jnp.where(r_i == c_i, padded, 0.0), axis=0, keepdims=True)
    pstart = jnp.sum(jnp.where(c_i < r_i, padded_row, 0.0), axis=1, keepdims=True)
    pend = pstart + padded

    def place(c, carry):
        c0 = pl.multiple_of(c * chunk, chunk)
        base = pre_ref[:, pl.ds(c0, chunk)] + pstart
        rows = [jnp.sum(jnp.where(eidx == idx_ref[k:k + 1, pl.ds(c0, chunk)], base, 0.0), axis=0, keepdims=True)
                for k in range(TOP_K)]
        rows.append(jnp.zeros((8 - TOP_K, chunk), F32))
        dest_ref[:, pl.ds(c0, chunk)] = jnp.concatenate(rows, axis=0).astype(jnp.int32)
        return carry

    lax.fori_loop(0, n_chunk, place, 0)

    nbp = blk_ref.shape[1]
    starts = (lax.broadcasted_iota(jnp.int32, (ne, nbp), 1) * tb).astype(F32)
    blk_e = jnp.minimum(jnp.sum((pend <= starts).astype(F32), axis=0, keepdims=True), ne - 1.0)
    n_live = jnp.sum(padded, axis=0, keepdims=True) * (1.0 / tb)
    blk_ref[...] = jnp.concatenate([blk_e, jnp.broadcast_to(n_live, (1, nbp)), jnp.zeros((6, nbp), F32)],
                                   axis=0).astype(jnp.int32)


def route_plan(idx8, ne, tb, n_blocks):
    nt = idx8.shape[1]
    chunk = _pick(nt, (256, 128))
    nbp = -(-n_blocks // LANES) * LANES
    return pl.pallas_call(
        functools.partial(_route_plan_kernel, ne=ne, tb=tb, chunk=chunk),
        out_shape=[jax.ShapeDtypeStruct((8, nt), jnp.int32), jax.ShapeDtypeStruct((8, nbp), jnp.int32)],
        scratch_shapes=[pltpu.VMEM((ne, nt), F32)],
        compiler_params=pltpu.CompilerParams(vmem_limit_bytes=VMEM_LIMIT),
        name="route_plan",
    )(idx8)


def _dispatch_kernel(dest_ref, h_ref, xz_hbm, xs_hbm, sem, *, rows):
    del xz_hbm
    base = pl.program_id(0) * rows

    def issue(r, carry):
        t = base + r
        for k in range(TOP_K):
            pltpu.make_async_copy(h_ref.at[pl.ds(r, 1)], xs_hbm.at[pl.ds(dest_ref[k, t], 1)], sem.at[0]).start()
        return carry

    lax.fori_loop(0, rows, issue, 0)
    for _ in range(TOP_K):
        pltpu.make_async_copy(h_ref, xs_hbm.at[pl.ds(0, rows)], sem.at[0]).wait()


def moe_dispatch(h_packed, dest8, cap):
    nt, w = h_packed.shape
    rows = _pick(nt, (256, 128, 64, 32, 16, 8))
    xz = jnp.zeros((cap, w), h_packed.dtype)
    return pl.pallas_call(
        functools.partial(_dispatch_kernel, rows=rows),
        grid=(nt // rows,),
        in_specs=[pl.BlockSpec(memory_space=pltpu.SMEM),
                  pl.BlockSpec((rows, w), lambda i: (i, 0)),
                  pl.BlockSpec(memory_space=pl.ANY)],
        out_specs=pl.BlockSpec(memory_space=pl.ANY),
        out_shape=jax.ShapeDtypeStruct((cap, w), h_packed.dtype),
        scratch_shapes=[pltpu.SemaphoreType.DMA((1,))],
        input_output_aliases={2: 0},
        compiler_params=_cparams("arbitrary"),
        name="moe_dispatch",
    )(dest8, h_packed, xz)


def _expert_kernel(be_ref, nb_ref, x_ref, wg_ref, bg_ref, wu_ref, bu_ref, wd_ref, bd_ref, o_ref,
                   wg_bf, wu_bf, wd_bf):
    i = pl.program_id(0)
    prev = be_ref[jnp.maximum(i - 1, 0)]
    live = i < nb_ref[0]
    half = x_ref.shape[1]

    @pl.when(live & ((i == 0) | (be_ref[i] != prev)))
    def _():
        wg_bf[...] = wg_ref[...].astype(BF16)
        wu_bf[...] = wu_ref[...].astype(BF16)
        wd_bf[...] = wd_ref[...].astype(BF16)

    @pl.when(live)
    def _():
        lo, hi = _unpack_pairs(x_ref[...])
        lo = lo.astype(BF16)
        hi = hi.astype(BF16)

        def proj(w_bf, b_ref):
            return (jnp.dot(lo, w_bf[0:half, :], preferred_element_type=F32)
                    + jnp.dot(hi, w_bf[half:2 * half, :], preferred_element_type=F32) + b_ref[...])

        g = jnp.minimum(proj(wg_bf, bg_ref), SWIGLU_LIMIT)
        u = jnp.clip(proj(wu_bf, bu_ref), -SWIGLU_LIMIT, SWIGLU_LIMIT)
        a = g * jax.nn.sigmoid(SWIGLU_ALPHA * g) * (u + 1.0)
        y = jnp.dot(a.astype(BF16), wd_bf[...], preferred_element_type=F32) + bd_ref[...]
        o_ref[...] = _pack_pairs(y[:, 0:half], y[:, half:2 * half])

    @pl.when(jnp.logical_not(live))
    def _():
        o_ref[...] = jnp.zeros_like(o_ref)


def expert_ffn(x_sorted, blk_e, n_live, w_gate, b_gate, w_up, b_up, w_down, b_down):
    cap, half = x_sorted.shape
    ne, d, f = w_gate.shape
    tb = MOE_ROWS
    grid_spec = pltpu.PrefetchScalarGridSpec(
        num_scalar_prefetch=2,
        grid=(cap // tb,),
        in_specs=[pl.BlockSpec((tb, half), lambda i, be, nb: (i, 0)),
                  pl.BlockSpec((None, d, f), lambda i, be, nb: (be[i], 0, 0)),
                  pl.BlockSpec((None, 1, f), lambda i, be, nb: (be[i], 0, 0)),
                  pl.BlockSpec((None, d, f), lambda i, be, nb: (be[i], 0, 0)),
                  pl.BlockSpec((None, 1, f), lambda i, be, nb: (be[i], 0, 0)),
                  pl.BlockSpec((None, f, d), lambda i, be, nb: (be[i], 0, 0)),
                  pl.BlockSpec((None, 1, d), lambda i, be, nb: (be[i], 0, 0))],
        out_specs=pl.BlockSpec((tb, half), lambda i, be, nb: (i, 0)),
        scratch_shapes=[pltpu.VMEM((d, f), BF16), pltpu.VMEM((d, f), BF16), pltpu.VMEM((f, d), BF16)],
    )
    return pl.pallas_call(
        _expert_kernel,
        grid_spec=grid_spec,
        out_shape=jax.ShapeDtypeStruct((cap, half), jnp.uint32),
        compiler_params=_cparams("arbitrary"),
        name="expert_ffn",
    )(blk_e, n_live, x_sorted, w_gate, b_gate.reshape(ne, 1, f), w_up, b_up.reshape(ne, 1, f),
      w_down, b_down.reshape(ne, 1, d))


def _combine_kernel(dest_ref, x_ref, w_ref, gate_ref, y_hbm, o_ref, buf, sem, *, tm):
    i = pl.program_id(0)
    n = pl.num_programs(0)
    slot = i % 2
    half = buf.shape[3]

    def fetch(tile, s):
        def issue(r, carry):
            t = tile * tm + r
            for k in range(TOP_K):
                pltpu.make_async_copy(y_hbm.at[pl.ds(dest_ref[k, t], 1)], buf.at[s, k, pl.ds(r, 1)], sem.at[s]).start()
            return carry
        lax.fori_loop(0, tm, issue, 0)

    @pl.when(i == 0)
    def _():
        fetch(0, 0)

    @pl.when(i + 1 < n)
    def _():
        fetch(i + 1, 1 - slot)

    for k in range(TOP_K):
        pltpu.make_async_copy(y_hbm.at[pl.ds(0, tm)], buf.at[slot, k], sem.at[slot]).wait()

    acc_lo = None
    acc_hi = None
    for k in range(TOP_K):
        lo, hi = _unpack_pairs(buf[slot, k])
        wk = w_ref[:, k:k + 1]
        acc_lo = wk * lo if acc_lo is None else acc_lo + wk * lo
        acc_hi = wk * hi if acc_hi is None else acc_hi + wk * hi
    o_ref[:, 0:half] = x_ref[:, 0:half] + gate_ref[:, 0:half] * acc_lo
    o_ref[:, half:2 * half] = x_ref[:, half:2 * half] + gate_ref[:, half:2 * half] * acc_hi


def moe_combine(st, x, y_sorted, dest8, w_tok, mod, k_gate):
    nt, d = x.shape
    tm = _pick(st.tm, (128, 64, 32, 16, 8))
    return pl.pallas_call(
        functools.partial(_combine_kernel, tm=tm),
        grid=(nt // tm,),
        in_specs=[pl.BlockSpec(memory_space=pltpu.SMEM),
                  pl.BlockSpec((tm, d), lambda i: (i, 0)),
                  pl.BlockSpec((tm, TOP_K), lambda i: (i, 0)),
                  _mod_spec(st, tm, k_gate, d, 1, 0),
                  pl.BlockSpec(memory_space=pl.ANY)],
        out_specs=pl.BlockSpec((tm, d), lambda i: (i, 0)),
        out_shape=jax.ShapeDtypeStruct((nt, d), F32),
        scratch_shapes=[pltpu.VMEM((2, TOP_K, tm, d // 2), jnp.uint32), pltpu.SemaphoreType.DMA((2,))],
        input_output_aliases={1: 0},
        compiler_params=_cparams("arbitrary"),
        name="moe_combine",
    )(dest8, x, w_tok, mod, y_sorted)


def moe_layer(st, x, g, mod, w_router, b_router, w_gate, b_gate, w_up, b_up, w_down, b_down):
    nt, d = x.shape
    ne = w_router.shape[1]
    tb = MOE_ROWS
    h_packed, idx8, wgt8 = norm_router(st, x, g, mod, 3, 4, w_router, b_router)
    n_assign = nt * TOP_K
    cap = -(-(n_assign + ne * (tb - 1)) // tb) * tb
    n_blocks = cap // tb
    dest8, blk = route_plan(idx8, ne, tb, n_blocks)
    x_sorted = moe_dispatch(h_packed, dest8, cap)
    y_sorted = expert_ffn(x_sorted, blk[0, :n_blocks], blk[1, :1], w_gate, b_gate, w_up, b_up, w_down, b_down)
    return moe_combine(st, x, y_sorted, dest8, wgt8[:TOP_K].T, mod, 5)


def even_mixer(st, x, h, mod, w_in, w_out, s5p, d_skip, glu_w, glu_b, fnet_w, tables):
    d = x.shape[1]
    fw = fnet_w.shape[0] * fnet_w.shape[1]
    sw = d - fw
    p = matmul([h], w_in, st=st)
    b_bd, c_bd, lam8 = s5p
    y_f, y_b = s5_scan(st, p, sw, b_bd, c_bd, lam8)
    gl = s5_gelu(y_f, y_b, p, d_skip, sw)
    a = matmul([gl], glu_w, mode="glu", extra=(gl, glu_b), st=st)
    cd, sd, tab_s, tab_l = tables
    pcs = fnet_chan(p, sw // fw, fnet_w, cd, sd)
    fm = fnet_seq(st, pcs, tab_l, tab_s)
    return matmul([a, fm], w_out, mode="resid", extra=(x, mod, 2), st=st)


def odd_mixer(st, x, h, mod, w_in, w_out, pool_w, pool_scale, q_gain, k_gain, rope):
    d = x.shape[1]
    pw = pool_w.shape[0] * pool_w.shape[1]
    n_q = (d - pw) // HEAD_DIM
    p = matmul([h], w_in, st=st)
    cos_t, sin_t = rope
    gains = jnp.stack([q_gain, k_gain]).reshape(2, 1, HEAD_DIM)
    qk = qk_prep(st, p, gains, cos_t, sin_t, n_q // KV_HEADS)
    v_col0 = (pw + n_q * HEAD_DIM + KV_HEADS * HEAD_DIM) // HEAD_DIM
    att = attention(st, qk, p, v_col0, n_q)
    pm = pool_mix(st, p, pool_w, pool_scale)
    return matmul([pm, att], w_out, mode="resid", extra=(x, mod, 2), st=st)


def kernel(x, c, ctx, c_ctx, ada_w, ada_b, norm_mix_g, norm_ffn_g, ev_w_in, ev_w_out, s5_lam_re, s5_lam_im, s5_log_dt, s5_b_re, s5_b_im, s5_c_re, s5_c_im, s5_d, s5_glu_w, s5_glu_b, fnet_w, od_w_in, od_w_out, pool_w, pool_scale, q_gain, k_gain, router_w, router_b, exp_w_gate, exp_b_gate, exp_w_up, exp_b_up, exp_w_down, exp_b_down, final_g):
    batch, seq, d = x.shape
    ctx_len = ctx.shape[1]
    depth = ada_w.shape[0]
    st = Stream(batch, seq, ctx_len)
    xs = jnp.concatenate([ctx, x], axis=1).reshape(st.nt, d)

    cond8 = jnp.concatenate([c, c_ctx[None], jnp.zeros((8 - batch - 1, d), F32)], axis=0)
    mod_all = adaln_all(cond8, ada_w, ada_b)
    mod_all = mod_all.reshape(depth, 8, 6, 1, d).transpose(0, 2, 1, 3, 4)

    dh = fnet_w.shape[2]
    cd, sd = dft_tables(dh)
    tab_s = tuple(t.astype(BF16) for t in dft_tables(seq))
    tab_l = tuple(t.astype(BF16) for t in dft_tables(ctx_len))
    tables = (cd, sd, tab_s, tab_l)
    rope = rope_tables(seq, st.tm)

    for layer in range(depth):
        i = layer // 2
        mod = mod_all[layer]
        h = norm_mod(st, xs, norm_mix_g[layer], mod, 0, 1)
        if layer % 2 == 0:
            s5p = s5_tables(s5_lam_re[i], s5_lam_im[i], s5_log_dt[i], s5_b_re[i], s5_b_im[i], s5_c_re[i], s5_c_im[i])
            xs = even_mixer(st, xs, h, mod, ev_w_in[i], ev_w_out[i], s5p, s5_d[i], s5_glu_w[i], s5_glu_b[i],
                            fnet_w[i], tables)
        else:
            xs = odd_mixer(st, xs, h, mod, od_w_in[i], od_w_out[i], pool_w[i], pool_scale[i], q_gain[i], k_gain[i],
                           rope)
        xs = moe_layer(st, xs, norm_ffn_g[layer], mod, router_w[layer], router_b[layer],
                       exp_w_gate[layer], exp_b_gate[layer], exp_w_up[layer], exp_b_up[layer],
                       exp_w_down[layer], exp_b_down[layer])
    return final_norm(st, xs, final_g).reshape(batch, seq, d)
```

```python
import functools
import math

import jax
import jax.numpy as jnp
from jax import lax
from jax.experimental import pallas as pl
from jax.experimental.pallas import tpu as pltpu

F32 = jnp.float32
BF16 = jnp.bfloat16
EPS = 1e-6

GRID_W = 64
FNET_HEADS = 4
S5_GROUP = 16
S5_STATE = 64
POOL_WINDOWS = (2, 4, 8, 16)
HEAD_DIM = 128
KV_HEADS = 4
ROPE_THETA = 10000.0
TOP_K = 4
SWIGLU_LIMIT = 7.0
SWIGLU_ALPHA = 1.702

LANES = 128
S5_BLOCK_GROUPS = LANES // S5_GROUP
VMEM_LIMIT = 48 * 1024 * 1024
MOE_ROWS = 256


def _cparams(*sem):
    return pltpu.CompilerParams(dimension_semantics=sem, vmem_limit_bytes=VMEM_LIMIT)


def _pick(n, prefs):
    for p in prefs:
        if n % p == 0:
            return p
    return n


def _adaln_kernel(c_ref, w_ref, b_ref, o_ref):
    c = c_ref[...]
    a = (c * jax.nn.sigmoid(c)).astype(BF16)
    o_ref[...] = jnp.dot(a, w_ref[...].astype(BF16), preferred_element_type=F32) + b_ref[...]


def adaln_all(cond8, ada_w, ada_b):
    depth, d, n = ada_w.shape
    tn = _pick(n, (1024, 512, 256, 128))
    return pl.pallas_call(
        _adaln_kernel,
        grid=(depth, n // tn),
        in_specs=[pl.BlockSpec((8, d), lambda l, j: (0, 0)),
                  pl.BlockSpec((None, d, tn), lambda l, j: (l, 0, j)),
                  pl.BlockSpec((None, 1, tn), lambda l, j: (l, 0, j))],
        out_specs=pl.BlockSpec((None, 8, tn), lambda l, j: (l, 0, j)),
        out_shape=jax.ShapeDtypeStruct((depth, 8, n), F32),
        compiler_params=_cparams("arbitrary", "arbitrary"),
        name="adaln",
    )(cond8, ada_w, ada_b.reshape(depth, 1, n))


def _norm_mod_kernel(x_ref, g_ref, sh_ref, sc_ref, o_ref):
    x = x_ref[...]
    y = x * lax.rsqrt(jnp.mean(x * x, axis=-1, keepdims=True) + EPS) * g_ref[...]
    o_ref[...] = (y * (1.0 + sc_ref[...]) + sh_ref[...]).astype(o_ref.dtype)


def _norm_kernel(x_ref, g_ref, o_ref):
    x = x_ref[...]
    o_ref[...] = (x * lax.rsqrt(jnp.mean(x * x, axis=-1, keepdims=True) + EPS) * g_ref[...]).astype(o_ref.dtype)


class Stream:
    def __init__(self, batch, seq, ctx_len):
        self.b, self.s, self.l = batch, seq, ctx_len
        self.t = seq + ctx_len
        self.nt = batch * self.t
        self.tm = _pick(math.gcd(seq, ctx_len), (256, 128, 64, 32, 16))

    def mod_row(self, tm):
        tpb, l, b = self.t // tm, self.l, self.b
        return lambda i: jnp.where((i % tpb) * tm < l, b, i // tpb)


def _mod_spec(st, tm, k, d, grid_rank, row_axis):
    mr = st.mod_row(tm)
    if grid_rank == 1:
        return pl.BlockSpec((None, None, 1, d), lambda i: (k, mr(i), 0, 0))
    if row_axis == 1:
        return pl.BlockSpec((None, None, 1, d), lambda j, i: (k, mr(i), 0, j))
    raise ValueError


def norm_mod(st, x, g, mod, k_shift, k_scale, out_dtype=BF16):
    nt, d = x.shape
    tm = _pick(st.tm, (256, 128, 64, 32, 16))
    return pl.pallas_call(
        _norm_mod_kernel,
        grid=(nt // tm,),
        in_specs=[pl.BlockSpec((tm, d), lambda i: (i, 0)),
                  pl.BlockSpec((1, d), lambda i: (0, 0)),
                  _mod_spec(st, tm, k_shift, d, 1, 0),
                  _mod_spec(st, tm, k_scale, d, 1, 0)],
        out_specs=pl.BlockSpec((tm, d), lambda i: (i, 0)),
        out_shape=jax.ShapeDtypeStruct((nt, d), out_dtype),
        compiler_params=_cparams("arbitrary"),
        name="norm_mod",
    )(x, g.reshape(1, d), mod, mod)


def final_norm(st, x, g):
    d = x.shape[1]
    tm = st.tm
    tpb, lt, ns = st.t // tm, st.l // tm, st.s // tm
    return pl.pallas_call(
        _norm_kernel,
        grid=(st.b, ns),
        in_specs=[pl.BlockSpec((tm, d), lambda b, i: (b * tpb + lt + i, 0)),
                  pl.BlockSpec((1, d), lambda b, i: (0, 0))],
        out_specs=pl.BlockSpec((tm, d), lambda b, i: (b * ns + i, 0)),
        out_shape=jax.ShapeDtypeStruct((st.b * st.s, d), F32),
        compiler_params=_cparams("arbitrary", "arbitrary"),
        name="final_norm",
    )(x, g.reshape(1, d))


def _mm_kernel(*refs, n_a, k_offs, mode):
    a_refs = refs[:n_a]
    w_ref = refs[n_a]
    rest = refs[n_a + 1:]
    wbf_ref = rest[-1]
    o_ref = rest[-2]

    @pl.when(pl.program_id(1) == 0)
    def _():
        wbf_ref[...] = w_ref[...].astype(BF16)

    acc = None
    for a_ref, (k0, k1) in zip(a_refs, k_offs):
        part = jnp.dot(a_ref[...], wbf_ref[k0:k1, :], preferred_element_type=F32)
        acc = part if acc is None else acc + part
    if mode == "plain":
        o_ref[...] = acc.astype(o_ref.dtype)
    elif mode == "resid":
        x_ref, gate_ref = rest[0], rest[1]
        o_ref[...] = x_ref[...] + gate_ref[...] * acc
    elif mode == "glu":
        g_ref, b_ref = rest[0], rest[1]
        o_ref[...] = (g_ref[...].astype(F32) * jax.nn.sigmoid(acc + b_ref[...])).astype(o_ref.dtype)
    else:
        raise ValueError(mode)


def matmul(a_parts, w_stack, w_idx, mode="plain", out_dtype=BF16, extra=(), st=None,
           tn_prefs=(2048, 1536, 1024, 512, 256, 128)):
    m = a_parts[0].shape[0]
    _, k, n = w_stack.shape
    tm = _pick(m if st is None else st.tm, (512, 256, 128, 64, 32, 16, 8))
    tn = _pick(n, tn_prefs)
    k_offs, off = [], 0
    in_specs, args = [], []
    for a in a_parts:
        ka = a.shape[1]
        k_offs.append((off, off + ka))
        off += ka
        in_specs.append(pl.BlockSpec((tm, ka), lambda j, i: (i, 0)))
        args.append(a)
    assert off == k
    in_specs.append(pl.BlockSpec((None, k, tn), lambda j, i: (w_idx, 0, j), pipeline_mode=pl.Buffered(1)))
    args.append(w_stack)
    io_alias = {}
    if mode == "resid":
        x, mod, k_gate = extra
        in_specs.append(pl.BlockSpec((tm, tn), lambda j, i: (i, j)))
        args.append(x)
        in_specs.append(_mod_spec(st, tm, k_gate, tn, 2, 1))
        args.append(mod)
        io_alias = {len(args) - 2: 0}
        out_dtype = F32
    elif mode == "glu":
        g, bias = extra
        in_specs.append(pl.BlockSpec((tm, tn), lambda j, i: (i, j)))
        args.append(g)
        in_specs.append(pl.BlockSpec((1, tn), lambda j, i: (0, j)))
        args.append(bias.reshape(1, n))
    return pl.pallas_call(
        functools.partial(_mm_kernel, n_a=len(a_parts), k_offs=tuple(k_offs), mode=mode),
        grid=(n // tn, m // tm),
        in_specs=in_specs,
        out_specs=pl.BlockSpec((tm, tn), lambda j, i: (i, j)),
        out_shape=jax.ShapeDtypeStruct((m, n), out_dtype),
        scratch_shapes=[pltpu.VMEM((k, tn), BF16)],
        input_output_aliases=io_alias,
        compiler_params=_cparams("arbitrary", "arbitrary"),
        name="mm_" + mode,
    )(*args)


def _s5_kernel(uf_ref, ub_ref, b_ref, c_ref, lam_ref, yf_ref, yb_ref, h_ref, bu_ref, u_scr, y_scr, *, tc):
    half = b_ref.shape[1] // 2
    nbat = uf_ref.shape[0]

    @pl.when(pl.program_id(1) == 0)
    def _():
        h_ref[...] = jnp.zeros_like(h_ref)

    rows = tc * 8
    is_fwd = (lax.broadcasted_iota(jnp.int32, (rows, LANES), 0) % 8) < 4
    flip = (lax.broadcasted_iota(jnp.int32, (tc, tc), 0)
            + lax.broadcasted_iota(jnp.int32, (tc, tc), 1) == tc - 1).astype(BF16)
    for b in range(nbat):
        u_scr[pl.ds(b, tc, stride=8), :] = uf_ref[b].astype(F32)
        u_scr[pl.ds(nbat + b, tc, stride=8), :] = jnp.dot(flip, ub_ref[b], preferred_element_type=F32)
    u = u_scr[...]
    lhs = jnp.concatenate([jnp.where(is_fwd, u, 0.0), jnp.where(is_fwd, 0.0, u)], axis=1).astype(BF16)
    bu_ref[...] = jnp.dot(lhs, b_ref[...], preferred_element_type=F32)

    lam_r = lam_ref[0]
    lam_i = lam_ref[1]

    def step(t, carry):
        hr, hi = carry
        r0 = pl.multiple_of(t * 8, 8)
        nhr = lam_r * hr - lam_i * hi + bu_ref[pl.ds(r0, 8), 0:half]
        nhi = lam_r * hi + lam_i * hr + bu_ref[pl.ds(r0, 8), half:2 * half]
        bu_ref[pl.ds(r0, 8), 0:half] = nhr
        bu_ref[pl.ds(r0, 8), half:2 * half] = nhi
        return nhr, nhi

    hr, hi = lax.fori_loop(0, tc, step, (h_ref[:, 0:half], h_ref[:, half:2 * half]), unroll=2)
    h_ref[:, 0:half] = hr
    h_ref[:, half:2 * half] = hi

    y2 = jnp.dot(bu_ref[...].astype(BF16), c_ref[...], preferred_element_type=F32)
    y_scr[...] = jnp.where(is_fwd, y2[:, 0:LANES], y2[:, LANES:2 * LANES])
    for b in range(nbat):
        yf_ref[b] = y_scr[pl.ds(b, tc, stride=8), :].astype(yf_ref.dtype)
        yb = y_scr[pl.ds(nbat + b, tc, stride=8), :].astype(BF16)
        yb_ref[b] = jnp.dot(flip, yb, preferred_element_type=F32).astype(yb_ref.dtype)


def s5_scan(st, p, width, b_bd, c_bd, lam8):
    bsz, t_len, l_len = st.b, st.t, st.l
    assert 2 * bsz == 8
    d = p.shape[1]
    nb = width // LANES
    tc = _pick(math.gcd(st.s, l_len), (128, 64, 32, 16, 8))
    rows = tc * 8
    ns = b_bd.shape[2]
    n_l, n_c = l_len // tc, t_len // tc

    def bwd_chunk(c):
        return jnp.where(c < n_l, n_l - 1 - c, n_c + n_l - 1 - c)

    p3 = p.reshape(bsz, t_len, d)
    out = jax.ShapeDtypeStruct((bsz, t_len, width), BF16)
    y_f, y_b = pl.pallas_call(
        functools.partial(_s5_kernel, tc=tc),
        grid=(nb, n_c),
        in_specs=[pl.BlockSpec((bsz, tc, LANES), lambda j, c: (0, c, j)),
                  pl.BlockSpec((bsz, tc, LANES), lambda j, c: (0, bwd_chunk(c), j)),
                  pl.BlockSpec((None, 2 * LANES, ns), lambda j, c: (j, 0, 0)),
                  pl.BlockSpec((None, ns, 2 * LANES), lambda j, c: (j, 0, 0)),
                  pl.BlockSpec((None, 2, 8, ns // 2), lambda j, c: (j, 0, 0, 0))],
        out_specs=[pl.BlockSpec((bsz, tc, LANES), lambda j, c: (0, c, j)),
                   pl.BlockSpec((bsz, tc, LANES), lambda j, c: (0, bwd_chunk(c), j))],
        out_shape=[out, out],
        scratch_shapes=[pltpu.VMEM((8, ns), F32), pltpu.VMEM((rows, ns), F32),
                        pltpu.VMEM((rows, LANES), F32), pltpu.VMEM((rows, LANES), F32)],
        compiler_params=_cparams("arbitrary", "arbitrary"),
        name="s5_scan",
    )(p3, p3, b_bd, c_bd, lam8)
    return y_f.reshape(bsz * t_len, width), y_b.reshape(bsz * t_len, width)


def _s5_gelu_kernel(yf_ref, yb_ref, u_ref, d_ref, o_ref):
    y = yf_ref[...].astype(F32) + yb_ref[...].astype(F32) + d_ref[...] * u_ref[...].astype(F32)
    o_ref[...] = jax.nn.gelu(y).astype(o_ref.dtype)


def s5_gelu(y_f, y_b, p, d_skip, width):
    nt = y_f.shape[0]
    tm = _pick(nt, (512, 256, 128, 64, 32, 16, 8))
    row = pl.BlockSpec((tm, width), lambda i: (i, 0))
    return pl.pallas_call(
        _s5_gelu_kernel,
        grid=(nt // tm,),
        in_specs=[row, row, row, pl.BlockSpec((1, width), lambda i: (0, 0))],
        out_specs=row,
        out_shape=jax.ShapeDtypeStruct((nt, width), BF16),
        compiler_params=_cparams("arbitrary"),
        name="s5_gelu",
    )(y_f, y_b, p, d_skip.reshape(1, width))


def s5_tables(lam_re, lam_im, log_dt, b_re, b_im, c_re, c_im):
    _, g, p = lam_re.shape
    c = b_re.shape[-1]
    nb = g // S5_BLOCK_GROUPS
    gb = S5_BLOCK_GROUPS
    lam = lax.complex(lam_re.astype(F32), lam_im.astype(F32))
    dt = jnp.exp(log_dt.astype(F32))[..., None]
    lam_bar = jnp.exp(lam * dt)
    b_bar = ((lam_bar - 1.0) / lam)[..., None] * lax.complex(b_re.astype(F32), b_im.astype(F32))
    eye = jnp.eye(gb, dtype=F32)
    bb = b_bar.reshape(2, nb, gb, p, c)

    def b_lay(z):
        return jnp.einsum('djgpc,gh->jdgchp', z, eye).reshape(nb, 2 * gb * c, gb * p)

    b_bd = jnp.concatenate([b_lay(bb.real), b_lay(bb.imag)], axis=-1).astype(BF16)
    cc = lax.complex(c_re.astype(F32), c_im.astype(F32)).reshape(2, nb, gb, c, p)

    def c_lay(z):
        return jnp.einsum('djgcp,gh->jgpdhc', z, eye).reshape(nb, gb * p, 2 * gb * c)

    c_bd = jnp.concatenate([c_lay(cc.real), -c_lay(cc.imag)], axis=1).astype(BF16)
    lb = lam_bar.reshape(2, nb, gb * p)
    lam8 = jnp.stack([lb.real, lb.imag], axis=0)
    lam8 = jnp.repeat(lam8.transpose(2, 0, 1, 3), 4, axis=2)
    return b_bd, c_bd, lam8


def _fnet_chan_kernel(f_ref, cd_ref, sd_ref, w_ref, o_ref, wc_ref):
    nh, dh = w_ref.shape[0], w_ref.shape[1]

    @pl.when(pl.program_id(0) == 0)
    def _():
        for h in range(nh):
            wh = w_ref[h]
            wc_ref[h, :, 0:dh] = jnp.dot(cd_ref[...], wh, preferred_element_type=F32,
                                         precision=lax.Precision.HIGHEST).astype(BF16)
            wc_ref[h, :, dh:2 * dh] = jnp.dot(sd_ref[...], wh, preferred_element_type=F32,
                                              precision=lax.Precision.HIGHEST).astype(BF16)

    for h in range(nh):
        r = jnp.dot(f_ref[:, h * dh:(h + 1) * dh], wc_ref[h], preferred_element_type=F32)
        o_ref[:, h * dh:(h + 1) * dh] = r[:, 0:dh].astype(o_ref.dtype)
        o_ref[:, (nh + h) * dh:(nh + h + 1) * dh] = r[:, dh:2 * dh].astype(o_ref.dtype)


def fnet_chan(p, col_block, fnet_w, cd, sd):
    nt = p.shape[0]
    nh, dh, _ = fnet_w.shape
    width = nh * dh
    tm = _pick(nt, (512, 256, 128, 64, 32, 16, 8))
    return pl.pallas_call(
        _fnet_chan_kernel,
        grid=(nt // tm,),
        in_specs=[pl.BlockSpec((tm, width), lambda i: (i, col_block)),
                  pl.BlockSpec((dh, dh), lambda i: (0, 0)),
                  pl.BlockSpec((dh, dh), lambda i: (0, 0)),
                  pl.BlockSpec((nh, dh, dh), lambda i: (0, 0, 0))],
        out_specs=pl.BlockSpec((tm, 2 * width), lambda i: (i, 0)),
        out_shape=jax.ShapeDtypeStruct((nt, 2 * width), BF16),
        scratch_shapes=[pltpu.VMEM((nh, dh, 2 * dh), BF16)],
        compiler_params=_cparams("arbitrary"),
        name="fnet_chan",
    )(p, cd, sd, fnet_w)


def _fnet_seq_kernel(cl_ref, sl_ref, cs_ref, ss_ref, p_ref, o_ref, *, l_len):
    w = o_ref.shape[1]
    tm = o_ref.shape[0]
    t_len = p_ref.shape[0]
    in_ctx = pl.program_id(1) * tm < l_len

    @pl.when(in_ctx)
    def _():
        o_ref[...] = (jnp.dot(cl_ref[...], p_ref[0:l_len, 0:w], preferred_element_type=F32)
                      - jnp.dot(sl_ref[...], p_ref[0:l_len, w:2 * w], preferred_element_type=F32)).astype(o_ref.dtype)

    @pl.when(jnp.logical_not(in_ctx))
    def _():
        o_ref[...] = (jnp.dot(cs_ref[...], p_ref[l_len:t_len, 0:w], preferred_element_type=F32)
                      - jnp.dot(ss_ref[...], p_ref[l_len:t_len, w:2 * w], preferred_element_type=F32)).astype(o_ref.dtype)


def fnet_seq(st, pcs, tab_l, tab_s):
    w2 = pcs.shape[1]
    tm = st.tm
    tpb, lt = st.t // tm, st.l // tm
    ctx_tile = lambda b, i: (jnp.minimum(i, lt - 1), 0)
    lat_tile = lambda b, i: (jnp.maximum(i - lt, 0), 0)
    return pl.pallas_call(
        functools.partial(_fnet_seq_kernel, l_len=st.l),
        grid=(st.b, tpb),
        in_specs=[pl.BlockSpec((tm, st.l), ctx_tile), pl.BlockSpec((tm, st.l), ctx_tile),
                  pl.BlockSpec((tm, st.s), lat_tile), pl.BlockSpec((tm, st.s), lat_tile),
                  pl.BlockSpec((st.t, w2), lambda b, i: (b, 0))],
        out_specs=pl.BlockSpec((tm, w2 // 2), lambda b, i: (b * tpb + i, 0)),
        out_shape=jax.ShapeDtypeStruct((st.nt, w2 // 2), BF16),
        compiler_params=_cparams("arbitrary", "arbitrary"),
        name="fnet_seq",
    )(tab_l[0], tab_l[1], tab_s[0], tab_s[1], pcs)


def dft_tables(n):
    k = jnp.arange(n, dtype=jnp.int32)
    kn = (k[:, None] * k[None, :]) % n
    ang = kn.astype(F32) * (2.0 * math.pi / n)
    scale = 1.0 / math.sqrt(n)
    return jnp.cos(ang) * scale, jnp.sin(ang) * scale


def _pool_kernel(v_ref, w_ref, sc_ref, o_ref, *, windows, l_len):
    dh = w_ref.shape[1]
    for r0, r1 in ((0, l_len), (l_len, v_ref.shape[0])):
        t_len = r1 - r0
        t = lax.broadcasted_iota(jnp.int32, (t_len, dh), 0)
        for g, win in enumerate(windows):
            v = v_ref[r0:r1, g * dh:(g + 1) * dh].astype(F32)
            lo = win // 2
            acc = jnp.zeros_like(v)
            for j in range(-lo, win - lo):
                src = t + j
                shifted = v if j == 0 else pltpu.roll(v, (-j) % t_len, 0)
                acc = acc + jnp.where((src >= 0) & (src < t_len), shifted, 0.0)
            cnt = jnp.clip(t + (win - lo), 0, t_len) - jnp.clip(t - lo, 0, t_len)
            pooled = acc / cnt.astype(F32) - v
            y = jnp.dot(pooled.astype(BF16), w_ref[g].astype(BF16), preferred_element_type=F32)
            o_ref[r0:r1, g * dh:(g + 1) * dh] = (y * sc_ref[:, g * dh:(g + 1) * dh]).astype(o_ref.dtype)


def pool_mix(st, p, pool_w, pool_scale):
    ng, dh, _ = pool_w.shape
    width = ng * dh
    return pl.pallas_call(
        functools.partial(_pool_kernel, windows=POOL_WINDOWS, l_len=st.l),
        grid=(st.b,),
        in_specs=[pl.BlockSpec((st.t, width), lambda b: (b, 0)),
                  pl.BlockSpec((ng, dh, dh), lambda b: (0, 0, 0)),
                  pl.BlockSpec((1, width), lambda b: (0, 0))],
        out_specs=pl.BlockSpec((st.t, width), lambda b: (b, 0)),
        out_shape=jax.ShapeDtypeStruct((st.nt, width), BF16),
        compiler_params=_cparams("arbitrary"),
        name="pool_mix",
    )(p, pool_w, pool_scale.reshape(1, width))


def _qk_prep_kernel(*refs, q_blocks, q_scale):
    x_refs = refs[:q_blocks + 1]
    g_ref, cos_ref, sin_ref, o_ref = refs[q_blocks + 1:]
    cw = x_refs[0].shape[1]
    cos = cos_ref[...]
    sin = sin_ref[...]
    lane = lax.broadcasted_iota(jnp.int32, (o_ref.shape[0], HEAD_DIM), 1)
    first = (lane % (HEAD_DIM // 2)) < (HEAD_DIM // 4)
    for j, x_ref in enumerate(x_refs):
        is_q = j < q_blocks
        gain = g_ref[0 if is_q else 1]
        if is_q:
            gain = gain * q_scale
        for h in range(cw // HEAD_DIM):
            x = x_ref[:, h * HEAD_DIM:(h + 1) * HEAD_DIM].astype(F32)
            n = x * lax.rsqrt(jnp.mean(x * x, axis=-1, keepdims=True) + EPS) * gain
            partner = jnp.where(first, pltpu.roll(n, HEAD_DIM - HEAD_DIM // 4, 1), pltpu.roll(n, HEAD_DIM // 4, 1))
            c0 = j * cw + h * HEAD_DIM
            o_ref[:, c0:c0 + HEAD_DIM] = (n * cos + partner * sin).astype(o_ref.dtype)


def qk_prep(st, p, gains, cos_t, sin_t, q_blocks, q_scale):
    nt = p.shape[0]
    cw = KV_HEADS * HEAD_DIM
    tm = st.tm
    tpb, lt, ns = st.t // tm, st.l // tm, st.s // tm
    rope_blk = lambda i: jnp.where(i % tpb < lt, ns, i % tpb - lt)
    x_specs = [pl.BlockSpec((tm, cw), lambda i, j=j: (i, j + 1)) for j in range(q_blocks + 1)]
    return pl.pallas_call(
        functools.partial(_qk_prep_kernel, q_blocks=q_blocks, q_scale=q_scale),
        grid=(nt // tm,),
        in_specs=x_specs + [pl.BlockSpec((2, 1, HEAD_DIM), lambda i: (0, 0, 0)),
                            pl.BlockSpec((tm, HEAD_DIM), lambda i: (rope_blk(i), 0)),
                            pl.BlockSpec((tm, HEAD_DIM), lambda i: (rope_blk(i), 0))],
        out_specs=pl.BlockSpec((tm, (q_blocks + 1) * cw), lambda i: (i, 0)),
        out_shape=jax.ShapeDtypeStruct((nt, (q_blocks + 1) * cw), BF16),
        compiler_params=_cparams("arbitrary"),
        name="qk_prep",
    )(*([p] * (q_blocks + 1)), gains, cos_t, sin_t)


def rope_tables(s, tm):
    rows = s // GRID_W
    row = jnp.repeat(jnp.arange(rows), GRID_W)
    col = jnp.tile(jnp.arange(GRID_W), rows)
    nf = HEAD_DIM // 4
    inv = jnp.power(ROPE_THETA, -jnp.arange(nf, dtype=F32) / nf)
    a_row = row.astype(F32)[:, None] * inv
    a_col = col.astype(F32)[:, None] * inv
    ang = jnp.concatenate([a_row, a_row, a_col, a_col], axis=1)
    sign = jnp.tile(jnp.concatenate([-jnp.ones((nf,), F32), jnp.ones((nf,), F32)]), 2)
    cos_t = jnp.concatenate([jnp.cos(ang), jnp.ones((tm, HEAD_DIM), F32)], axis=0)
    sin_t = jnp.concatenate([jnp.sin(ang) * sign, jnp.zeros((tm, HEAD_DIM), F32)], axis=0)
    return cos_t, sin_t


def _attn_kernel(q_ref, k_ref, v_ref, o_ref, *, q_per_kv, l_len):
    tq = q_ref.shape[0]
    dn = (((1,), (1,)), ((), ()))

    def run(n_keys):
        k = k_ref[0:n_keys, :]
        v = v_ref[0:n_keys, :]
        for g in range(q_per_kv):
            q = q_ref[:, g * HEAD_DIM:(g + 1) * HEAD_DIM]
            s_ = lax.dot_general(q, k, dn, preferred_element_type=F32)
            e = jnp.exp2(s_ - s_.max(axis=-1, keepdims=True))
            acc = jnp.dot(e.astype(BF16), v, preferred_element_type=F32)
            o_ref[:, g * HEAD_DIM:(g + 1) * HEAD_DIM] = (acc / e.sum(axis=-1, keepdims=True)).astype(o_ref.dtype)

    in_ctx = pl.program_id(2) * tq < l_len

    @pl.when(in_ctx)
    def _():
        run(l_len)

    @pl.when(jnp.logical_not(in_ctx))
    def _():
        run(k_ref.shape[0])


def attention(st, qk, p, v_col0, n_q_heads):
    q_per_kv = n_q_heads // KV_HEADS
    qw = q_per_kv * HEAD_DIM
    tq = st.tm
    tpb = st.t // tq
    k_col0 = n_q_heads
    return pl.pallas_call(
        functools.partial(_attn_kernel, q_per_kv=q_per_kv, l_len=st.l),
        grid=(st.b, KV_HEADS, tpb),
        in_specs=[pl.BlockSpec((tq, qw), lambda b, h, i: (b * tpb + i, h)),
                  pl.BlockSpec((st.t, HEAD_DIM), lambda b, h, i: (b, k_col0 + h)),
                  pl.BlockSpec((st.t, HEAD_DIM), lambda b, h, i: (b, v_col0 + h))],
        out_specs=pl.BlockSpec((tq, qw), lambda b, h, i: (b * tpb + i, h)),
        out_shape=jax.ShapeDtypeStruct((st.nt, n_q_heads * HEAD_DIM), BF16),
        compiler_params=_cparams("arbitrary", "arbitrary", "arbitrary"),
        name="attention",
    )(qk, qk, p)


def _pack_pairs(lo, hi):
    lo_b = lax.bitcast_convert_type(lo.astype(BF16).astype(F32), jnp.uint32)
    hi_b = lax.bitcast_convert_type(hi.astype(BF16).astype(F32), jnp.uint32)
    return (lo_b >> 16) | (hi_b & jnp.uint32(0xFFFF0000))


def _unpack_pairs(word):
    lo = lax.bitcast_convert_type(word << 16, F32)
    hi = lax.bitcast_convert_type(word & jnp.uint32(0xFFFF0000), F32)
    return lo, hi


def _router_kernel(x_ref, g_ref, sh_ref, sc_ref, wr_ref, br_ref, h_ref, idx_ref, wgt_ref):
    x = x_ref[...]
    y = x * lax.rsqrt(jnp.mean(x * x, axis=-1, keepdims=True) + EPS) * g_ref[...]
    h = y * (1.0 + sc_ref[...]) + sh_ref[...]
    half = h.shape[1] // 2
    h_ref[...] = _pack_pairs(h[:, 0:half], h[:, half:2 * half])
    logits = lax.dot_general(wr_ref[...], h, (((1,), (1,)), ((), ())), preferred_element_type=F32,
                             precision=lax.Precision.HIGHEST) + br_ref[...]
    ne, tm = logits.shape
    eidx = lax.broadcasted_iota(jnp.int32, (ne, tm), 0)
    vals, idxs = [], []
    cur = logits
    for _ in range(TOP_K):
        m = cur.max(axis=0, keepdims=True)
        sel = jnp.min(jnp.where(cur == m, eidx, ne), axis=0, keepdims=True)
        vals.append(m)
        idxs.append(sel)
        cur = jnp.where(eidx == sel, -jnp.inf, cur)
    es = [jnp.exp(v - vals[0]) for v in vals]
    den = es[0]
    for e in es[1:]:
        den = den + e
    zi = jnp.zeros((8 - TOP_K, tm), jnp.int32)
    zf = jnp.zeros((8 - TOP_K, tm), F32)
    idx_ref[...] = jnp.concatenate(idxs + [zi], axis=0)
    wgt_ref[...] = jnp.concatenate([e / den for e in es] + [zf], axis=0)


def norm_router(st, x, g, mod, k_shift, k_scale, w_router, b_router):
    nt, d = x.shape
    ne = w_router.shape[1]
    tm = _pick(st.tm, (256, 128))
    return pl.pallas_call(
        _router_kernel,
        grid=(nt // tm,),
        in_specs=[pl.BlockSpec((tm, d), lambda i: (i, 0)),
                  pl.BlockSpec((1, d), lambda i: (0, 0)),
                  _mod_spec(st, tm, k_shift, d, 1, 0),
                  _mod_spec(st, tm, k_scale, d, 1, 0),
                  pl.BlockSpec((ne, d), lambda i: (0, 0)),
                  pl.BlockSpec((ne, 1), lambda i: (0, 0))],
        out_specs=[pl.BlockSpec((tm, d // 2), lambda i: (i, 0)),
                   pl.BlockSpec((8, tm), lambda i: (0, i)),
                   pl.BlockSpec((8, tm), lambda i: (0, i))],
        out_shape=[jax.ShapeDtypeStruct((nt, d // 2), jnp.uint32),
                   jax.ShapeDtypeStruct((8, nt), jnp.int32),
                   jax.ShapeDtypeStruct((8, nt), F32)],
        compiler_params=_cparams("arbitrary"),
        name="norm_router",
    )(x, g.reshape(1, d), mod, mod, w_router.T, b_router.reshape(ne, 1))


def _route_plan_kernel(idx_ref, dest_ref, blk_ref, ends_ref, pre_ref, *, ne, tb, chunk):
    nt = idx_ref.shape[1]
    n_chunk = nt // chunk
    eidx = lax.broadcasted_iota(jnp.int32, (ne, chunk), 0)
    tri = (lax.broadcasted_iota(jnp.int32, (chunk, chunk), 0)
           < lax.broadcasted_iota(jnp.int32, (chunk, chunk), 1)).astype(BF16)

    def count(c, carry):
        c0 = pl.multiple_of(c * chunk, chunk)
        oh = jnp.zeros((ne, chunk), F32)
        for k in range(TOP_K):
            oh = oh + (eidx == idx_ref[k:k + 1, pl.ds(c0, chunk)]).astype(F32)
        pre_ref[:, pl.ds(c0, chunk)] = jnp.dot(oh.astype(BF16), tri, preferred_element_type=F32) + carry
        return carry + oh.sum(axis=1, keepdims=True)

    counts = lax.fori_loop(0, n_chunk, count, jnp.zeros((ne, 1), F32))
    padded = jnp.ceil(counts * (1.0 / tb)) * tb
    r_i = lax.broadcasted_iota(jnp.int32, (ne, ne), 0)
    c_i = lax.broadcasted_iota(jnp.int32, (ne, ne), 1)
    padded_row = jnp.sum(jnp.where(r_i == c_i, padded, 0.0), axis=0, keepdims=True)
    pstart = jnp.sum(jnp.where(c_i < r_i, padded_row, 0.0), axis=1, keepdims=True)
    pend = pstart + padded

    def place(c, carry):
        c0 = pl.multiple_of(c * chunk, chunk)
        base = pre_ref[:, pl.ds(c0, chunk)] + pstart
        rows = [jnp.sum(jnp.where(eidx == idx_ref[k:k + 1, pl.ds(c0, chunk)], base, 0.0), axis=0, keepdims=True)
                for k in range(TOP_K)]
        rows.append(jnp.zeros((8 - TOP_K, chunk), F32))
        dest_ref[:, pl.ds(c0, chunk)] = jnp.concatenate(rows, axis=0).astype(jnp.int32)
        return carry

    lax.fori_loop(0, n_chunk, place, 0)

    nbp = blk_ref.shape[1]
    starts = (lax.broadcasted_iota(jnp.int32, (ne, nbp), 1) * tb).astype(F32)
    blk_e = jnp.minimum(jnp.sum((pend <= starts).astype(F32), axis=0, keepdims=True), ne - 1.0)
    n_live = jnp.sum(padded, axis=0, keepdims=True) * (1.0 / tb)
    e_col = lax.broadcasted_iota(jnp.int32, (ne, nbp), 0).astype(F32)
    owns = jnp.where(padded > 0.0, 1.0, 0.0)
    later = jnp.where(e_col > blk_e, owns, 0.0)
    nxt = jnp.min(jnp.where(later > 0.0, e_col, float(ne)), axis=0, keepdims=True)
    nxt = jnp.where(nxt >= float(ne), -1.0, nxt)
    rank = jnp.sum(jnp.where(e_col < blk_e, owns, 0.0), axis=0, keepdims=True)
    slot = rank - 2.0 * jnp.floor(rank * 0.5)
    blk_ref[...] = jnp.concatenate([blk_e, jnp.broadcast_to(n_live, (1, nbp)), nxt, slot, jnp.zeros((4, nbp), F32)],
                                   axis=0).astype(jnp.int32)

    r_l = lax.broadcasted_iota(jnp.int32, (ne, LANES), 0)
    c_l = lax.broadcasted_iota(jnp.int32, (ne, LANES), 1)
    to_row = lambda col: jnp.sum(jnp.where(r_l == c_l, col, 0.0), axis=0, keepdims=True)
    ends_ref[...] = jnp.concatenate([to_row(pstart + counts), to_row(pend), jnp.zeros((6, LANES), F32)],
                                    axis=0).astype(jnp.int32)


def route_plan(idx8, ne, tb, n_blocks):
    nt = idx8.shape[1]
    chunk = _pick(nt, (256, 128))
    nbp = -(-n_blocks // LANES) * LANES
    return pl.pallas_call(
        functools.partial(_route_plan_kernel, ne=ne, tb=tb, chunk=chunk),
        out_shape=[jax.ShapeDtypeStruct((8, nt), jnp.int32), jax.ShapeDtypeStruct((8, nbp), jnp.int32),
                   jax.ShapeDtypeStruct((8, LANES), jnp.int32)],
        scratch_shapes=[pltpu.VMEM((ne, nt), F32)],
        compiler_params=pltpu.CompilerParams(vmem_limit_bytes=VMEM_LIMIT),
        name="route_plan",
    )(idx8)


def _dispatch_kernel(dest_ref, ends_ref, h_ref, xs_hbm, zrow, sem, *, rows, ne):
    i = pl.program_id(0)
    base = i * rows

    @pl.when(i == 0)
    def _():
        zrow[...] = jnp.zeros_like(zrow)

        def per_expert(e, carry):
            lo, hi = ends_ref[0, e], ends_ref[1, e]

            def put(r, c):
                pltpu.make_async_copy(zrow.at[pl.ds(0, 1)], xs_hbm.at[pl.ds(r, 1)], sem.at[1]).start()
                return c

            def done(r, c):
                pltpu.make_async_copy(zrow.at[pl.ds(0, 1)], xs_hbm.at[pl.ds(0, 1)], sem.at[1]).wait()
                return c

            lax.fori_loop(lo, hi, put, 0)
            lax.fori_loop(lo, hi, done, 0)
            return carry

        lax.fori_loop(0, ne, per_expert, 0)

        tb = zrow.shape[0]
        first_dead = ends_ref[1, ne - 1] // tb
        n_blk = xs_hbm.shape[0] // tb

        def put_blk(j, c):
            pltpu.make_async_copy(zrow, xs_hbm.at[pl.ds(pl.multiple_of(j * tb, tb), tb)], sem.at[1]).start()
            return c

        def done_blk(j, c):
            pltpu.make_async_copy(zrow, xs_hbm.at[pl.ds(0, tb)], sem.at[1]).wait()
            return c

        lax.fori_loop(first_dead, n_blk, put_blk, 0)
        lax.fori_loop(first_dead, n_blk, done_blk, 0)

    def issue(r, carry):
        t = base + r
        for k in range(TOP_K):
            pltpu.make_async_copy(h_ref.at[pl.ds(r, 1)], xs_hbm.at[pl.ds(dest_ref[k, t], 1)], sem.at[0]).start()
        return carry

    lax.fori_loop(0, rows, issue, 0)
    for _ in range(TOP_K):
        pltpu.make_async_copy(h_ref, xs_hbm.at[pl.ds(0, rows)], sem.at[0]).wait()


def moe_dispatch(h_packed, dest8, ends, cap, ne):
    nt, w = h_packed.shape
    rows = _pick(nt, (256, 128, 64, 32, 16, 8))
    return pl.pallas_call(
        functools.partial(_dispatch_kernel, rows=rows, ne=ne),
        grid=(nt // rows,),
        in_specs=[pl.BlockSpec(memory_space=pltpu.SMEM),
                  pl.BlockSpec(memory_space=pltpu.SMEM),
                  pl.BlockSpec((rows, w), lambda i: (i, 0))],
        out_specs=pl.BlockSpec(memory_space=pl.ANY),
        out_shape=jax.ShapeDtypeStruct((cap, w), h_packed.dtype),
        scratch_shapes=[pltpu.VMEM((MOE_ROWS, w), h_packed.dtype), pltpu.SemaphoreType.DMA((2,))],
        compiler_params=_cparams("arbitrary"),
        name="moe_dispatch",
    )(dest8, ends, h_packed)


def _expert_kernel(be_ref, nb_ref, nxt_ref, slot_ref, x_ref, bg_ref, bu_ref, bd_ref, wg_hbm, wu_hbm, wd_hbm, o_ref,
                   wg_st, wu_st, wd_st, wg_bf, wu_bf, wd_bf, sem, *, layer):
    i = pl.program_id(0)
    e = be_ref[i]
    live = i < nb_ref[0]
    first = (i == 0) | (e != be_ref[jnp.maximum(i - 1, 0)])
    slot = slot_ref[i]
    half = x_ref.shape[1]

    def weight_copies(ex, s):
        return (pltpu.make_async_copy(wg_hbm.at[layer, ex], wg_st.at[s], sem.at[s, 0]),
                pltpu.make_async_copy(wu_hbm.at[layer, ex], wu_st.at[s], sem.at[s, 1]),
                pltpu.make_async_copy(wd_hbm.at[layer, ex], wd_st.at[s], sem.at[s, 2]))

    @pl.when(i == 0)
    def _():
        for cp in weight_copies(e, slot):
            cp.start()

    @pl.when(live & first)
    def _():
        for cp in weight_copies(e, slot):
            cp.wait()

        @pl.when(nxt_ref[i] >= 0)
        def _():
            for cp in weight_copies(nxt_ref[i], 1 - slot):
                cp.start()

        wg_bf[...] = wg_st[slot].astype(BF16)
        wu_bf[...] = wu_st[slot].astype(BF16)
        wd_bf[...] = wd_st[slot].astype(BF16)

    @pl.when(live)
    def _():
        lo, hi = _unpack_pairs(x_ref[...])
        lo = lo.astype(BF16)
        hi = hi.astype(BF16)

        def proj(w_bf, b_ref):
            return (jnp.dot(lo, w_bf[0:half, :], preferred_element_type=F32)
                    + jnp.dot(hi, w_bf[half:2 * half, :], preferred_element_type=F32) + b_ref[...])

        g = jnp.minimum(proj(wg_bf, bg_ref), SWIGLU_LIMIT)
        u = jnp.clip(proj(wu_bf, bu_ref), -SWIGLU_LIMIT, SWIGLU_LIMIT)
        a = g * jax.nn.sigmoid(SWIGLU_ALPHA * g) * (u + 1.0)
        y = jnp.dot(a.astype(BF16), wd_bf[...], preferred_element_type=F32) + bd_ref[...]
        o_ref[...] = _pack_pairs(y[:, 0:half], y[:, half:2 * half])

    @pl.when(jnp.logical_not(live))
    def _():
        o_ref[...] = jnp.zeros_like(o_ref)


def expert_ffn(x_sorted, blk, n_blocks, layer, w_gate, b_gate, w_up, b_up, w_down, b_down):
    cap, half = x_sorted.shape
    depth, ne, d, f = w_gate.shape
    tb = MOE_ROWS
    bias = lambda n: pl.BlockSpec((None, 1, n), lambda i, be, nb, nx, sl: (layer * ne + be[i], 0, 0))
    hbm = pl.BlockSpec(memory_space=pl.ANY)
    grid_spec = pltpu.PrefetchScalarGridSpec(
        num_scalar_prefetch=4,
        grid=(n_blocks,),
        in_specs=[pl.BlockSpec((tb, half), lambda i, be, nb, nx, sl: (jnp.minimum(i, nb[0] - 1), 0)),
                  bias(f), bias(f), bias(d), hbm, hbm, hbm],
        out_specs=pl.BlockSpec((tb, half), lambda i, be, nb, nx, sl: (i, 0)),
        scratch_shapes=[pltpu.VMEM((2, d, f), F32), pltpu.VMEM((2, d, f), F32), pltpu.VMEM((2, f, d), F32),
                        pltpu.VMEM((d, f), BF16), pltpu.VMEM((d, f), BF16), pltpu.VMEM((f, d), BF16),
                        pltpu.SemaphoreType.DMA((2, 3))],
    )
    return pl.pallas_call(
        functools.partial(_expert_kernel, layer=layer),
        grid_spec=grid_spec,
        out_shape=jax.ShapeDtypeStruct((cap, half), jnp.uint32),
        compiler_params=_cparams("arbitrary"),
        name="expert_ffn",
    )(blk[0, :n_blocks], blk[1, :1], blk[2, :n_blocks], blk[3, :n_blocks], x_sorted,
      b_gate.reshape(depth * ne, 1, f), b_up.reshape(depth * ne, 1, f), b_down.reshape(depth * ne, 1, d),
      w_gate, w_up, w_down)


def _combine_kernel(dest_ref, x_ref, w_ref, gate_ref, y_hbm, o_ref, buf, sem, *, tm):
    i = pl.program_id(0)
    n = pl.num_programs(0)
    slot = i % 2
    half = buf.shape[3]

    def fetch(tile, s):
        def issue(r, carry):
            t = tile * tm + r
            for k in range(TOP_K):
                pltpu.make_async_copy(y_hbm.at[pl.ds(dest_ref[k, t], 1)], buf.at[s, k, pl.ds(r, 1)], sem.at[s]).start()
            return carry
        lax.fori_loop(0, tm, issue, 0)

    @pl.when(i == 0)
    def _():
        fetch(0, 0)

    @pl.when(i + 1 < n)
    def _():
        fetch(i + 1, 1 - slot)

    for k in range(TOP_K):
        pltpu.make_async_copy(y_hbm.at[pl.ds(0, tm)], buf.at[slot, k], sem.at[slot]).wait()

    acc_lo = None
    acc_hi = None
    for k in range(TOP_K):
        lo, hi = _unpack_pairs(buf[slot, k])
        wk = w_ref[:, k:k + 1]
        acc_lo = wk * lo if acc_lo is None else acc_lo + wk * lo
        acc_hi = wk * hi if acc_hi is None else acc_hi + wk * hi
    o_ref[:, 0:half] = x_ref[:, 0:half] + gate_ref[:, 0:half] * acc_lo
    o_ref[:, half:2 * half] = x_ref[:, half:2 * half] + gate_ref[:, half:2 * half] * acc_hi


def moe_combine(st, x, y_sorted, dest8, w_tok, mod, k_gate):
    nt, d = x.shape
    tm = _pick(st.tm, (128, 64, 32, 16, 8))
    return pl.pallas_call(
        functools.partial(_combine_kernel, tm=tm),
        grid=(nt // tm,),
        in_specs=[pl.BlockSpec(memory_space=pltpu.SMEM),
                  pl.BlockSpec((tm, d), lambda i: (i, 0)),
                  pl.BlockSpec((tm, TOP_K), lambda i: (i, 0)),
                  _mod_spec(st, tm, k_gate, d, 1, 0),
                  pl.BlockSpec(memory_space=pl.ANY)],
        out_specs=pl.BlockSpec((tm, d), lambda i: (i, 0)),
        out_shape=jax.ShapeDtypeStruct((nt, d), F32),
        scratch_shapes=[pltpu.VMEM((2, TOP_K, tm, d // 2), jnp.uint32), pltpu.SemaphoreType.DMA((2,))],
        input_output_aliases={1: 0},
        compiler_params=_cparams("arbitrary"),
        name="moe_combine",
    )(dest8, x, w_tok, mod, y_sorted)


def moe_layer(st, x, g, mod, layer, w_router, b_router, w_gate, b_gate, w_up, b_up, w_down, b_down):
    nt, d = x.shape
    ne = w_router.shape[1]
    tb = MOE_ROWS
    h_packed, idx8, wgt8 = norm_router(st, x, g, mod, 3, 4, w_router, b_router)
    n_assign = nt * TOP_K
    cap = -(-(n_assign + ne * (tb - 1)) // tb) * tb
    n_blocks = cap // tb
    dest8, blk, ends = route_plan(idx8, ne, tb, n_blocks)
    x_sorted = moe_dispatch(h_packed, dest8, ends, cap, ne)
    y_sorted = expert_ffn(x_sorted, blk, n_blocks, layer, w_gate, b_gate, w_up, b_up, w_down, b_down)
    return moe_combine(st, x, y_sorted, dest8, wgt8[:TOP_K].T, mod, 5)


def even_mixer(st, x, h, mod, i, w_in, w_out, s5p, d_skip, glu_w, glu_b, fnet_w, tables):
    d = x.shape[1]
    fw = fnet_w.shape[0] * fnet_w.shape[1]
    sw = d - fw
    p = matmul([h], w_in, i, st=st)
    b_bd, c_bd, lam8 = s5p
    y_f, y_b = s5_scan(st, p, sw, b_bd, c_bd, lam8)
    gl = s5_gelu(y_f, y_b, p, d_skip, sw)
    a = matmul([gl], glu_w, i, mode="glu", extra=(gl, glu_b), st=st)
    cd, sd, tab_s, tab_l = tables
    pcs = fnet_chan(p, sw // fw, fnet_w, cd, sd)
    fm = fnet_seq(st, pcs, tab_l, tab_s)
    return matmul([a, fm], w_out, i, mode="resid", extra=(x, mod, 2), st=st)


def odd_mixer(st, x, h, mod, i, w_in, w_out, pool_w, pool_scale, q_gain, k_gain, rope):
    d = x.shape[1]
    pw = pool_w.shape[0] * pool_w.shape[1]
    n_q = (d - pw) // HEAD_DIM
    p = matmul([h], w_in, i, st=st)
    cos_t, sin_t = rope
    gains = jnp.stack([q_gain, k_gain]).reshape(2, 1, HEAD_DIM)
    qk = qk_prep(st, p, gains, cos_t, sin_t, n_q // KV_HEADS, HEAD_DIM ** -0.5 * math.log2(math.e))
    v_col0 = (pw + n_q * HEAD_DIM + KV_HEADS * HEAD_DIM) // HEAD_DIM
    att = attention(st, qk, p, v_col0, n_q)
    pm = pool_mix(st, p, pool_w, pool_scale)
    return matmul([pm, att], w_out, i, mode="resid", extra=(x, mod, 2), st=st)


def kernel(x, c, ctx, c_ctx, ada_w, ada_b, norm_mix_g, norm_ffn_g, ev_w_in, ev_w_out, s5_lam_re, s5_lam_im, s5_log_dt, s5_b_re, s5_b_im, s5_c_re, s5_c_im, s5_d, s5_glu_w, s5_glu_b, fnet_w, od_w_in, od_w_out, pool_w, pool_scale, q_gain, k_gain, router_w, router_b, exp_w_gate, exp_b_gate, exp_w_up, exp_b_up, exp_w_down, exp_b_down, final_g):
    batch, seq, d = x.shape
    ctx_len = ctx.shape[1]
    depth = ada_w.shape[0]
    st = Stream(batch, seq, ctx_len)
    xs = jnp.concatenate([ctx, x], axis=1).reshape(st.nt, d)

    cond8 = jnp.concatenate([c, c_ctx[None], jnp.zeros((8 - batch - 1, d), F32)], axis=0)
    mod_all = adaln_all(cond8, ada_w, ada_b)
    mod_all = mod_all.reshape(depth, 8, 6, 1, d).transpose(0, 2, 1, 3, 4)

    dh = fnet_w.shape[2]
    cd, sd = dft_tables(dh)
    tab_s = tuple(t.astype(BF16) for t in dft_tables(seq))
    tab_l = tuple(t.astype(BF16) for t in dft_tables(ctx_len))
    tables = (cd, sd, tab_s, tab_l)
    rope = rope_tables(seq, st.tm)

    for layer in range(depth):
        i = layer // 2
        mod = mod_all[layer]
        h = norm_mod(st, xs, norm_mix_g[layer], mod, 0, 1)
        if layer % 2 == 0:
            s5p = s5_tables(s5_lam_re[i], s5_lam_im[i], s5_log_dt[i], s5_b_re[i], s5_b_im[i], s5_c_re[i], s5_c_im[i])
            xs = even_mixer(st, xs, h, mod, i, ev_w_in, ev_w_out, s5p, s5_d[i], s5_glu_w, s5_glu_b[i],
                            fnet_w[i], tables)
        else:
            xs = odd_mixer(st, xs, h, mod, i, od_w_in, od_w_out, pool_w[i], pool_scale[i], q_gain[i], k_gain[i],
                           rope)
        xs = moe_layer(st, xs, norm_ffn_g[layer], mod, layer, router_w[layer], router_b[layer],
                       exp_w_gate, exp_b_gate, exp_w_up, exp_b_up, exp_w_down, exp_b_down)
    return final_norm(st, xs, final_g).reshape(batch, seq, d)
```

```python
import functools
import math

import jax
import jax.numpy as jnp
from jax import lax
from jax.experimental import pallas as pl
from jax.experimental.pallas import tpu as pltpu

F32 = jnp.float32
BF16 = jnp.bfloat16
EPS = 1e-6

GRID_W = 64
FNET_HEADS = 4
S5_GROUP = 16
S5_STATE = 64
POOL_WINDOWS = (2, 4, 8, 16)
HEAD_DIM = 128
KV_HEADS = 4
ROPE_THETA = 10000.0
TOP_K = 4
SWIGLU_LIMIT = 7.0
SWIGLU_ALPHA = 1.702

LANES = 128
S5_BLOCK_GROUPS = LANES // S5_GROUP
VMEM_LIMIT = 48 * 1024 * 1024
MOE_ROWS = 256


def _cparams(*sem):
    return pltpu.CompilerParams(dimension_semantics=sem, vmem_limit_bytes=VMEM_LIMIT)


def _pick(n, prefs):
    for p in prefs:
        if n % p == 0:
            return p
    return n


def _adaln_kernel(c_ref, w_ref, b_ref, o_ref):
    c = c_ref[...]
    a = (c * jax.nn.sigmoid(c)).astype(BF16)
    o_ref[...] = jnp.dot(a, w_ref[...].astype(BF16), preferred_element_type=F32) + b_ref[...]


def adaln_all(cond8, ada_w, ada_b):
    depth, d, n = ada_w.shape
    tn = _pick(n, (1024, 512, 256, 128))
    return pl.pallas_call(
        _adaln_kernel,
        grid=(depth, n // tn),
        in_specs=[pl.BlockSpec((8, d), lambda l, j: (0, 0)),
                  pl.BlockSpec((None, d, tn), lambda l, j: (l, 0, j)),
                  pl.BlockSpec((None, 1, tn), lambda l, j: (l, 0, j))],
        out_specs=pl.BlockSpec((None, 8, tn), lambda l, j: (l, 0, j)),
        out_shape=jax.ShapeDtypeStruct((depth, 8, n), F32),
        compiler_params=_cparams("arbitrary", "arbitrary"),
        name="adaln",
    )(cond8, ada_w, ada_b.reshape(depth, 1, n))


def _norm_mod_kernel(x_ref, g_ref, sh_ref, sc_ref, o_ref):
    x = x_ref[...]
    y = x * lax.rsqrt(jnp.mean(x * x, axis=-1, keepdims=True) + EPS) * g_ref[...]
    o_ref[...] = (y * (1.0 + sc_ref[...]) + sh_ref[...]).astype(o_ref.dtype)


def _norm_kernel(x_ref, g_ref, o_ref):
    x = x_ref[...]
    o_ref[...] = (x * lax.rsqrt(jnp.mean(x * x, axis=-1, keepdims=True) + EPS) * g_ref[...]).astype(o_ref.dtype)


class Stream:
    def __init__(self, batch, seq, ctx_len):
        self.b, self.s, self.l = batch, seq, ctx_len
        self.t = seq + ctx_len
        self.nt = batch * self.t
        self.tm = _pick(math.gcd(seq, ctx_len), (256, 128, 64, 32, 16))

    def mod_row(self, tm):
        tpb, l, b = self.t // tm, self.l, self.b
        return lambda i: jnp.where((i % tpb) * tm < l, b, i // tpb)


def _mod_spec(st, tm, k, d, grid_rank, row_axis):
    mr = st.mod_row(tm)
    if grid_rank == 1:
        return pl.BlockSpec((None, None, 1, d), lambda i: (k, mr(i), 0, 0))
    if row_axis == 1:
        return pl.BlockSpec((None, None, 1, d), lambda j, i: (k, mr(i), 0, j))
    raise ValueError


def norm_mod(st, x, g, mod, k_shift, k_scale, out_dtype=BF16):
    nt, d = x.shape
    tm = _pick(st.tm, (256, 128, 64, 32, 16))
    return pl.pallas_call(
        _norm_mod_kernel,
        grid=(nt // tm,),
        in_specs=[pl.BlockSpec((tm, d), lambda i: (i, 0)),
                  pl.BlockSpec((1, d), lambda i: (0, 0)),
                  _mod_spec(st, tm, k_shift, d, 1, 0),
                  _mod_spec(st, tm, k_scale, d, 1, 0)],
        out_specs=pl.BlockSpec((tm, d), lambda i: (i, 0)),
        out_shape=jax.ShapeDtypeStruct((nt, d), out_dtype),
        compiler_params=_cparams("arbitrary"),
        name="norm_mod",
    )(x, g.reshape(1, d), mod, mod)


def final_norm(st, x, g):
    d = x.shape[1]
    tm = st.tm
    tpb, lt, ns = st.t // tm, st.l // tm, st.s // tm
    return pl.pallas_call(
        _norm_kernel,
        grid=(st.b, ns),
        in_specs=[pl.BlockSpec((tm, d), lambda b, i: (b * tpb + lt + i, 0)),
                  pl.BlockSpec((1, d), lambda b, i: (0, 0))],
        out_specs=pl.BlockSpec((tm, d), lambda b, i: (b * ns + i, 0)),
        out_shape=jax.ShapeDtypeStruct((st.b * st.s, d), F32),
        compiler_params=_cparams("arbitrary", "arbitrary"),
        name="final_norm",
    )(x, g.reshape(1, d))


def _mm_kernel(*refs, n_a, k_offs, mode):
    a_refs = refs[:n_a]
    w_ref = refs[n_a]
    rest = refs[n_a + 1:]
    wbf_ref = rest[-1]
    o_ref = rest[-2]

    @pl.when(pl.program_id(1) == 0)
    def _():
        wbf_ref[...] = w_ref[...].astype(BF16)

    acc = None
    for a_ref, (k0, k1) in zip(a_refs, k_offs):
        part = jnp.dot(a_ref[...], wbf_ref[k0:k1, :], preferred_element_type=F32)
        acc = part if acc is None else acc + part
    if mode == "plain":
        o_ref[...] = acc.astype(o_ref.dtype)
    elif mode == "resid":
        x_ref, gate_ref = rest[0], rest[1]
        o_ref[...] = x_ref[...] + gate_ref[...] * acc
    elif mode == "glu":
        g_ref, b_ref = rest[0], rest[1]
        o_ref[...] = (g_ref[...].astype(F32) * jax.nn.sigmoid(acc + b_ref[...])).astype(o_ref.dtype)
    else:
        raise ValueError(mode)


def matmul(a_parts, w_stack, w_idx, mode="plain", out_dtype=BF16, extra=(), st=None,
           tn_prefs=(2048, 1536, 1024, 512, 256, 128)):
    m = a_parts[0].shape[0]
    _, k, n = w_stack.shape
    tm = _pick(m if st is None else st.tm, (512, 256, 128, 64, 32, 16, 8))
    tn = _pick(n, tn_prefs)
    k_offs, off = [], 0
    in_specs, args = [], []
    for a in a_parts:
        ka = a.shape[1]
        k_offs.append((off, off + ka))
        off += ka
        in_specs.append(pl.BlockSpec((tm, ka), lambda j, i: (i, 0)))
        args.append(a)
    assert off == k
    in_specs.append(pl.BlockSpec((None, k, tn), lambda j, i: (w_idx, 0, j), pipeline_mode=pl.Buffered(1)))
    args.append(w_stack)
    io_alias = {}
    if mode == "resid":
        x, mod, k_gate = extra
        in_specs.append(pl.BlockSpec((tm, tn), lambda j, i: (i, j)))
        args.append(x)
        in_specs.append(_mod_spec(st, tm, k_gate, tn, 2, 1))
        args.append(mod)
        io_alias = {len(args) - 2: 0}
        out_dtype = F32
    elif mode == "glu":
        g, bias = extra
        in_specs.append(pl.BlockSpec((tm, tn), lambda j, i: (i, j)))
        args.append(g)
        in_specs.append(pl.BlockSpec((1, tn), lambda j, i: (0, j)))
        args.append(bias.reshape(1, n))
    return pl.pallas_call(
        functools.partial(_mm_kernel, n_a=len(a_parts), k_offs=tuple(k_offs), mode=mode),
        grid=(n // tn, m // tm),
        in_specs=in_specs,
        out_specs=pl.BlockSpec((tm, tn), lambda j, i: (i, j)),
        out_shape=jax.ShapeDtypeStruct((m, n), out_dtype),
        scratch_shapes=[pltpu.VMEM((k, tn), BF16)],
        input_output_aliases=io_alias,
        compiler_params=_cparams("arbitrary", "arbitrary"),
        name="mm_" + mode,
    )(*args)


def _s5_kernel(uf_ref, ub_ref, b_ref, c_ref, lam_ref, yf_ref, yb_ref, h_ref, bu_ref, u_scr, y_scr, *, tc, jb):
    half = b_ref.shape[2] // 2
    nbat = uf_ref.shape[0]

    @pl.when(pl.program_id(1) == 0)
    def _():
        h_ref[...] = jnp.zeros_like(h_ref)

    rows = tc * 8
    is_fwd = (lax.broadcasted_iota(jnp.int32, (rows, LANES), 0) % 8) < 4
    flip = (lax.broadcasted_iota(jnp.int32, (tc, tc), 0)
            + lax.broadcasted_iota(jnp.int32, (tc, tc), 1) == tc - 1).astype(BF16)
    for q in range(jb):
        lanes = slice(q * LANES, (q + 1) * LANES)
        for b in range(nbat):
            u_scr[q, pl.ds(b, tc, stride=8), :] = uf_ref[b, :, lanes].astype(F32)
            u_scr[q, pl.ds(nbat + b, tc, stride=8), :] = jnp.dot(flip, ub_ref[b, :, lanes],
                                                                 preferred_element_type=F32)
        u = u_scr[q]
        lhs = jnp.concatenate([jnp.where(is_fwd, u, 0.0), jnp.where(is_fwd, 0.0, u)], axis=1).astype(BF16)
        bu_ref[q] = jnp.dot(lhs, b_ref[q], preferred_element_type=F32)

    for q in range(jb):
        lam_r = lam_ref[q, 0]
        lam_i = lam_ref[q, 1]

        def step(t, carry, q=q, lam_r=lam_r, lam_i=lam_i):
            hr, hi = carry
            r0 = pl.multiple_of(t * 8, 8)
            nhr = lam_r * hr - lam_i * hi + bu_ref[q, pl.ds(r0, 8), 0:half]
            nhi = lam_r * hi + lam_i * hr + bu_ref[q, pl.ds(r0, 8), half:2 * half]
            bu_ref[q, pl.ds(r0, 8), 0:half] = nhr
            bu_ref[q, pl.ds(r0, 8), half:2 * half] = nhi
            return nhr, nhi

        hr, hi = lax.fori_loop(0, tc, step, (h_ref[q, :, 0:half], h_ref[q, :, half:2 * half]), unroll=True)
        h_ref[q, :, 0:half] = hr
        h_ref[q, :, half:2 * half] = hi

    for q in range(jb):
        lanes = slice(q * LANES, (q + 1) * LANES)
        y2 = jnp.dot(bu_ref[q].astype(BF16), c_ref[q], preferred_element_type=F32)
        y_scr[q] = jnp.where(is_fwd, y2[:, 0:LANES], y2[:, LANES:2 * LANES])
        for b in range(nbat):
            yf_ref[b, :, lanes] = y_scr[q, pl.ds(b, tc, stride=8), :].astype(yf_ref.dtype)
            yb = y_scr[q, pl.ds(nbat + b, tc, stride=8), :].astype(BF16)
            yb_ref[b, :, lanes] = jnp.dot(flip, yb, preferred_element_type=F32).astype(yb_ref.dtype)


def s5_scan(st, p, width, b_bd, c_bd, lam8):
    bsz, t_len, l_len = st.b, st.t, st.l
    assert 2 * bsz == 8
    d = p.shape[1]
    nb = width // LANES
    jb = 2 if nb % 2 == 0 else 1
    tc = _pick(math.gcd(st.s, l_len), (128, 64, 32, 16, 8))
    rows = tc * 8
    ns = b_bd.shape[2]
    n_l, n_c = l_len // tc, t_len // tc

    def bwd_chunk(c):
        return jnp.where(c < n_l, n_l - 1 - c, n_c + n_l - 1 - c)

    p3 = p.reshape(bsz, t_len, d)
    out = jax.ShapeDtypeStruct((bsz, t_len, width), BF16)
    y_f, y_b = pl.pallas_call(
        functools.partial(_s5_kernel, tc=tc, jb=jb),
        grid=(nb // jb, n_c),
        in_specs=[pl.BlockSpec((bsz, tc, jb * LANES), lambda j, c: (0, c, j)),
                  pl.BlockSpec((bsz, tc, jb * LANES), lambda j, c: (0, bwd_chunk(c), j)),
                  pl.BlockSpec((jb, 2 * LANES, ns), lambda j, c: (j, 0, 0)),
                  pl.BlockSpec((jb, ns, 2 * LANES), lambda j, c: (j, 0, 0)),
                  pl.BlockSpec((jb, 2, 8, ns // 2), lambda j, c: (j, 0, 0, 0))],
        out_specs=[pl.BlockSpec((bsz, tc, jb * LANES), lambda j, c: (0, c, j)),
                   pl.BlockSpec((bsz, tc, jb * LANES), lambda j, c: (0, bwd_chunk(c), j))],
        out_shape=[out, out],
        scratch_shapes=[pltpu.VMEM((jb, 8, ns), F32), pltpu.VMEM((jb, rows, ns), F32),
                        pltpu.VMEM((jb, rows, LANES), F32), pltpu.VMEM((jb, rows, LANES), F32)],
        compiler_params=_cparams("arbitrary", "arbitrary"),
        name="s5_scan",
    )(p3, p3, b_bd, c_bd, lam8)
    return y_f.reshape(bsz * t_len, width), y_b.reshape(bsz * t_len, width)


def _s5_gelu_kernel(yf_ref, yb_ref, u_ref, d_ref, o_ref):
    y = yf_ref[...].astype(F32) + yb_ref[...].astype(F32) + d_ref[...] * u_ref[...].astype(F32)
    o_ref[...] = jax.nn.gelu(y).astype(o_ref.dtype)


def s5_gelu(y_f, y_b, p, d_skip, width):
    nt = y_f.shape[0]
    tm = _pick(nt, (512, 256, 128, 64, 32, 16, 8))
    row = pl.BlockSpec((tm, width), lambda i: (i, 0))
    return pl.pallas_call(
        _s5_gelu_kernel,
        grid=(nt // tm,),
        in_specs=[row, row, row, pl.BlockSpec((1, width), lambda i: (0, 0))],
        out_specs=row,
        out_shape=jax.ShapeDtypeStruct((nt, width), BF16),
        compiler_params=_cparams("arbitrary"),
        name="s5_gelu",
    )(y_f, y_b, p, d_skip.reshape(1, width))


def s5_tables(lam_re, lam_im, log_dt, b_re, b_im, c_re, c_im):
    _, g, p = lam_re.shape
    c = b_re.shape[-1]
    nb = g // S5_BLOCK_GROUPS
    gb = S5_BLOCK_GROUPS
    lam = lax.complex(lam_re.astype(F32), lam_im.astype(F32))
    dt = jnp.exp(log_dt.astype(F32))[..., None]
    lam_bar = jnp.exp(lam * dt)
    b_bar = ((lam_bar - 1.0) / lam)[..., None] * lax.complex(b_re.astype(F32), b_im.astype(F32))
    eye = jnp.eye(gb, dtype=F32)
    bb = b_bar.reshape(2, nb, gb, p, c)

    def b_lay(z):
        return jnp.einsum('djgpc,gh->jdgchp', z, eye).reshape(nb, 2 * gb * c, gb * p)

    b_bd = jnp.concatenate([b_lay(bb.real), b_lay(bb.imag)], axis=-1).astype(BF16)
    cc = lax.complex(c_re.astype(F32), c_im.astype(F32)).reshape(2, nb, gb, c, p)

    def c_lay(z):
        return jnp.einsum('djgcp,gh->jgpdhc', z, eye).reshape(nb, gb * p, 2 * gb * c)

    c_bd = jnp.concatenate([c_lay(cc.real), -c_lay(cc.imag)], axis=1).astype(BF16)
    lb = lam_bar.reshape(2, nb, gb * p)
    lam8 = jnp.stack([lb.real, lb.imag], axis=0)
    lam8 = jnp.repeat(lam8.transpose(2, 0, 1, 3), 4, axis=2)
    return b_bd, c_bd, lam8


def _fnet_chan_kernel(f_ref, cd_ref, sd_ref, w_ref, o_ref, wc_ref):
    nh, dh = w_ref.shape[0], w_ref.shape[1]

    @pl.when(pl.program_id(0) == 0)
    def _():
        for h in range(nh):
            wh = w_ref[h]
            wc_ref[h, :, 0:dh] = jnp.dot(cd_ref[...], wh, preferred_element_type=F32,
                                         precision=lax.Precision.HIGHEST).astype(BF16)
            wc_ref[h, :, dh:2 * dh] = jnp.dot(sd_ref[...], wh, preferred_element_type=F32,
                                              precision=lax.Precision.HIGHEST).astype(BF16)

    for h in range(nh):
        r = jnp.dot(f_ref[:, h * dh:(h + 1) * dh], wc_ref[h], preferred_element_type=F32)
        o_ref[:, h * dh:(h + 1) * dh] = r[:, 0:dh].astype(o_ref.dtype)
        o_ref[:, (nh + h) * dh:(nh + h + 1) * dh] = r[:, dh:2 * dh].astype(o_ref.dtype)


def fnet_chan(p, col_block, fnet_w, cd, sd):
    nt = p.shape[0]
    nh, dh, _ = fnet_w.shape
    width = nh * dh
    tm = _pick(nt, (512, 256, 128, 64, 32, 16, 8))
    return pl.pallas_call(
        _fnet_chan_kernel,
        grid=(nt // tm,),
        in_specs=[pl.BlockSpec((tm, width), lambda i: (i, col_block)),
                  pl.BlockSpec((dh, dh), lambda i: (0, 0)),
                  pl.BlockSpec((dh, dh), lambda i: (0, 0)),
                  pl.BlockSpec((nh, dh, dh), lambda i: (0, 0, 0))],
        out_specs=pl.BlockSpec((tm, 2 * width), lambda i: (i, 0)),
        out_shape=jax.ShapeDtypeStruct((nt, 2 * width), BF16),
        scratch_shapes=[pltpu.VMEM((nh, dh, 2 * dh), BF16)],
        compiler_params=_cparams("arbitrary"),
        name="fnet_chan",
    )(p, cd, sd, fnet_w)


def _fnet_seq_kernel(cl_ref, sl_ref, cs_ref, ss_ref, p_ref, o_ref, *, l_len):
    w = o_ref.shape[1]
    tm = o_ref.shape[0]
    t_len = p_ref.shape[0]
    in_ctx = pl.program_id(1) * tm < l_len

    @pl.when(in_ctx)
    def _():
        o_ref[...] = (jnp.dot(cl_ref[...], p_ref[0:l_len, 0:w], preferred_element_type=F32)
                      - jnp.dot(sl_ref[...], p_ref[0:l_len, w:2 * w], preferred_element_type=F32)).astype(o_ref.dtype)

    @pl.when(jnp.logical_not(in_ctx))
    def _():
        o_ref[...] = (jnp.dot(cs_ref[...], p_ref[l_len:t_len, 0:w], preferred_element_type=F32)
                      - jnp.dot(ss_ref[...], p_ref[l_len:t_len, w:2 * w], preferred_element_type=F32)).astype(o_ref.dtype)


def fnet_seq(st, pcs, tab_l, tab_s):
    w2 = pcs.shape[1]
    tm = st.tm
    tpb, lt = st.t // tm, st.l // tm
    ctx_tile = lambda b, i: (jnp.minimum(i, lt - 1), 0)
    lat_tile = lambda b, i: (jnp.maximum(i - lt, 0), 0)
    return pl.pallas_call(
        functools.partial(_fnet_seq_kernel, l_len=st.l),
        grid=(st.b, tpb),
        in_specs=[pl.BlockSpec((tm, st.l), ctx_tile), pl.BlockSpec((tm, st.l), ctx_tile),
                  pl.BlockSpec((tm, st.s), lat_tile), pl.BlockSpec((tm, st.s), lat_tile),
                  pl.BlockSpec((st.t, w2), lambda b, i: (b, 0))],
        out_specs=pl.BlockSpec((tm, w2 // 2), lambda b, i: (b * tpb + i, 0)),
        out_shape=jax.ShapeDtypeStruct((st.nt, w2 // 2), BF16),
        compiler_params=_cparams("arbitrary", "arbitrary"),
        name="fnet_seq",
    )(tab_l[0], tab_l[1], tab_s[0], tab_s[1], pcs)


def dft_tables(n):
    k = jnp.arange(n, dtype=jnp.int32)
    kn = (k[:, None] * k[None, :]) % n
    ang = kn.astype(F32) * (2.0 * math.pi / n)
    scale = 1.0 / math.sqrt(n)
    return jnp.cos(ang) * scale, jnp.sin(ang) * scale


def _pool_kernel(v_ref, w_ref, sc_ref, o_ref, *, windows, l_len):
    dh = w_ref.shape[1]
    for r0, r1 in ((0, l_len), (l_len, v_ref.shape[0])):
        t_len = r1 - r0
        t = lax.broadcasted_iota(jnp.int32, (t_len, dh), 0)
        for g, win in enumerate(windows):
            v = v_ref[r0:r1, g * dh:(g + 1) * dh].astype(F32)
            lo = win // 2
            acc = jnp.zeros_like(v)
            for j in range(-lo, win - lo):
                src = t + j
                shifted = v if j == 0 else pltpu.roll(v, (-j) % t_len, 0)
                acc = acc + jnp.where((src >= 0) & (src < t_len), shifted, 0.0)
            cnt = jnp.clip(t + (win - lo), 0, t_len) - jnp.clip(t - lo, 0, t_len)
            pooled = acc / cnt.astype(F32) - v
            y = jnp.dot(pooled.astype(BF16), w_ref[g].astype(BF16), preferred_element_type=F32)
            o_ref[r0:r1, g * dh:(g + 1) * dh] = (y * sc_ref[:, g * dh:(g + 1) * dh]).astype(o_ref.dtype)


def pool_mix(st, p, pool_w, pool_scale):
    ng, dh, _ = pool_w.shape
    width = ng * dh
    return pl.pallas_call(
        functools.partial(_pool_kernel, windows=POOL_WINDOWS, l_len=st.l),
        grid=(st.b,),
        in_specs=[pl.BlockSpec((st.t, width), lambda b: (b, 0)),
                  pl.BlockSpec((ng, dh, dh), lambda b: (0, 0, 0)),
                  pl.BlockSpec((1, width), lambda b: (0, 0))],
        out_specs=pl.BlockSpec((st.t, width), lambda b: (b, 0)),
        out_shape=jax.ShapeDtypeStruct((st.nt, width), BF16),
        compiler_params=_cparams("arbitrary"),
        name="pool_mix",
    )(p, pool_w, pool_scale.reshape(1, width))


def _qk_prep_kernel(*refs, q_blocks, q_scale):
    x_refs = refs[:q_blocks + 1]
    g_ref, cos_ref, sin_ref, o_ref = refs[q_blocks + 1:]
    cw = x_refs[0].shape[1]
    cos = cos_ref[...]
    sin = sin_ref[...]
    lane = lax.broadcasted_iota(jnp.int32, (o_ref.shape[0], HEAD_DIM), 1)
    first = (lane % (HEAD_DIM // 2)) < (HEAD_DIM // 4)
    for j, x_ref in enumerate(x_refs):
        is_q = j < q_blocks
        gain = g_ref[0 if is_q else 1]
        if is_q:
            gain = gain * q_scale
        for h in range(cw // HEAD_DIM):
            x = x_ref[:, h * HEAD_DIM:(h + 1) * HEAD_DIM].astype(F32)
            n = x * lax.rsqrt(jnp.mean(x * x, axis=-1, keepdims=True) + EPS) * gain
            partner = jnp.where(first, pltpu.roll(n, HEAD_DIM - HEAD_DIM // 4, 1), pltpu.roll(n, HEAD_DIM // 4, 1))
            c0 = j * cw + h * HEAD_DIM
            o_ref[:, c0:c0 + HEAD_DIM] = (n * cos + partner * sin).astype(o_ref.dtype)


def qk_prep(st, p, gains, cos_t, sin_t, q_blocks, q_scale):
    nt = p.shape[0]
    cw = KV_HEADS * HEAD_DIM
    tm = st.tm
    tpb, lt, ns = st.t // tm, st.l // tm, st.s // tm
    rope_blk = lambda i: jnp.where(i % tpb < lt, ns, i % tpb - lt)
    x_specs = [pl.BlockSpec((tm, cw), lambda i, j=j: (i, j + 1)) for j in range(q_blocks + 1)]
    return pl.pallas_call(
        functools.partial(_qk_prep_kernel, q_blocks=q_blocks, q_scale=q_scale),
        grid=(nt // tm,),
        in_specs=x_specs + [pl.BlockSpec((2, 1, HEAD_DIM), lambda i: (0, 0, 0)),
                            pl.BlockSpec((tm, HEAD_DIM), lambda i: (rope_blk(i), 0)),
                            pl.BlockSpec((tm, HEAD_DIM), lambda i: (rope_blk(i), 0))],
        out_specs=pl.BlockSpec((tm, (q_blocks + 1) * cw), lambda i: (i, 0)),
        out_shape=jax.ShapeDtypeStruct((nt, (q_blocks + 1) * cw), BF16),
        compiler_params=_cparams("arbitrary"),
        name="qk_prep",
    )(*([p] * (q_blocks + 1)), gains, cos_t, sin_t)


def rope_tables(s, tm):
    rows = s // GRID_W
    row = jnp.repeat(jnp.arange(rows), GRID_W)
    col = jnp.tile(jnp.arange(GRID_W), rows)
    nf = HEAD_DIM // 4
    inv = jnp.power(ROPE_THETA, -jnp.arange(nf, dtype=F32) / nf)
    a_row = row.astype(F32)[:, None] * inv
    a_col = col.astype(F32)[:, None] * inv
    ang = jnp.concatenate([a_row, a_row, a_col, a_col], axis=1)
    sign = jnp.tile(jnp.concatenate([-jnp.ones((nf,), F32), jnp.ones((nf,), F32)]), 2)
    cos_t = jnp.concatenate([jnp.cos(ang), jnp.ones((tm, HEAD_DIM), F32)], axis=0)
    sin_t = jnp.concatenate([jnp.sin(ang) * sign, jnp.zeros((tm, HEAD_DIM), F32)], axis=0)
    return cos_t, sin_t


def _attn_kernel(q_ref, k_ref, v_ref, o_ref, *, q_per_kv, l_len):
    tq = q_ref.shape[0]
    dn = (((1,), (1,)), ((), ()))

    def run(n_keys):
        k = k_ref[0:n_keys, :]
        v = v_ref[0:n_keys, :]
        for g in range(q_per_kv):
            q = q_ref[:, g * HEAD_DIM:(g + 1) * HEAD_DIM]
            s_ = lax.dot_general(q, k, dn, preferred_element_type=F32)
            e = jnp.exp2(s_ - s_.max(axis=-1, keepdims=True))
            acc = jnp.dot(e.astype(BF16), v, preferred_element_type=F32)
            o_ref[:, g * HEAD_DIM:(g + 1) * HEAD_DIM] = (acc / e.sum(axis=-1, keepdims=True)).astype(o_ref.dtype)

    in_ctx = pl.program_id(2) * tq < l_len

    @pl.when(in_ctx)
    def _():
        run(l_len)

    @pl.when(jnp.logical_not(in_ctx))
    def _():
        run(k_ref.shape[0])


def attention(st, qk, p, v_col0, n_q_heads):
    q_per_kv = n_q_heads // KV_HEADS
    qw = q_per_kv * HEAD_DIM
    tq = st.tm
    tpb = st.t // tq
    k_col0 = n_q_heads
    return pl.pallas_call(
        functools.partial(_attn_kernel, q_per_kv=q_per_kv, l_len=st.l),
        grid=(st.b, KV_HEADS, tpb),
        in_specs=[pl.BlockSpec((tq, qw), lambda b, h, i: (b * tpb + i, h)),
                  pl.BlockSpec((st.t, HEAD_DIM), lambda b, h, i: (b, k_col0 + h)),
                  pl.BlockSpec((st.t, HEAD_DIM), lambda b, h, i: (b, v_col0 + h))],
        out_specs=pl.BlockSpec((tq, qw), lambda b, h, i: (b * tpb + i, h)),
        out_shape=jax.ShapeDtypeStruct((st.nt, n_q_heads * HEAD_DIM), BF16),
        compiler_params=_cparams("arbitrary", "arbitrary", "arbitrary"),
        name="attention",
    )(qk, qk, p)


def _pack_pairs(lo, hi):
    lo_b = lax.bitcast_convert_type(lo.astype(BF16).astype(F32), jnp.uint32)
    hi_b = lax.bitcast_convert_type(hi.astype(BF16).astype(F32), jnp.uint32)
    return (lo_b >> 16) | (hi_b & jnp.uint32(0xFFFF0000))


def _unpack_pairs(word):
    lo = lax.bitcast_convert_type(word << 16, F32)
    hi = lax.bitcast_convert_type(word & jnp.uint32(0xFFFF0000), F32)
    return lo, hi


ROW_TILE = 8


def _store_row_tiles(ref, word):
    rows = word.shape[0]
    for j in range(ROW_TILE):
        ref[pl.ds(j, rows, stride=ROW_TILE), :] = word[:, j * LANES:(j + 1) * LANES]


def _load_row_tiles(ref, rows):
    return [ref[pl.ds(j, rows, stride=ROW_TILE), :] for j in range(ROW_TILE)]


def _router_kernel(x_ref, g_ref, sh_ref, sc_ref, wr_ref, br_ref, h_ref, idx_ref, wgt_ref):
    x = x_ref[...]
    y = x * lax.rsqrt(jnp.mean(x * x, axis=-1, keepdims=True) + EPS) * g_ref[...]
    h = y * (1.0 + sc_ref[...]) + sh_ref[...]
    half = h.shape[1] // 2
    _store_row_tiles(h_ref, _pack_pairs(h[:, 0:half], h[:, half:2 * half]))
    logits = lax.dot_general(wr_ref[...], h, (((1,), (1,)), ((), ())), preferred_element_type=F32,
                             precision=lax.Precision.HIGHEST) + br_ref[...]
    ne, tm = logits.shape
    eidx = lax.broadcasted_iota(jnp.int32, (ne, tm), 0)
    vals, idxs = [], []
    cur = logits
    for _ in range(TOP_K):
        m = cur.max(axis=0, keepdims=True)
        sel = jnp.min(jnp.where(cur == m, eidx, ne), axis=0, keepdims=True)
        vals.append(m)
        idxs.append(sel)
        cur = jnp.where(eidx == sel, -jnp.inf, cur)
    es = [jnp.exp(v - vals[0]) for v in vals]
    den = es[0]
    for e in es[1:]:
        den = den + e
    zi = jnp.zeros((8 - TOP_K, tm), jnp.int32)
    zf = jnp.zeros((8 - TOP_K, tm), F32)
    idx_ref[...] = jnp.concatenate(idxs + [zi], axis=0)
    wgt_ref[...] = jnp.concatenate([e / den for e in es] + [zf], axis=0)


def norm_router(st, x, g, mod, k_shift, k_scale, w_router, b_router):
    nt, d = x.shape
    ne = w_router.shape[1]
    tm = _pick(st.tm, (256, 128))
    return pl.pallas_call(
        _router_kernel,
        grid=(nt // tm,),
        in_specs=[pl.BlockSpec((tm, d), lambda i: (i, 0)),
                  pl.BlockSpec((1, d), lambda i: (0, 0)),
                  _mod_spec(st, tm, k_shift, d, 1, 0),
                  _mod_spec(st, tm, k_scale, d, 1, 0),
                  pl.BlockSpec((ne, d), lambda i: (0, 0)),
                  pl.BlockSpec((ne, 1), lambda i: (0, 0))],
        out_specs=[pl.BlockSpec((tm * ROW_TILE, LANES), lambda i: (i, 0)),
                   pl.BlockSpec((8, tm), lambda i: (0, i)),
                   pl.BlockSpec((8, tm), lambda i: (0, i))],
        out_shape=[jax.ShapeDtypeStruct((nt * ROW_TILE, LANES), jnp.uint32),
                   jax.ShapeDtypeStruct((8, nt), jnp.int32),
                   jax.ShapeDtypeStruct((8, nt), F32)],
        compiler_params=_cparams("arbitrary"),
        name="norm_router",
    )(x, g.reshape(1, d), mod, mod, w_router.T, b_router.reshape(ne, 1))


def _route_plan_kernel(idx_ref, dest_ref, blk_ref, ends_ref, pre_ref, *, ne, tb, chunk):
    nt = idx_ref.shape[1]
    n_chunk = nt // chunk
    eidx = lax.broadcasted_iota(jnp.int32, (ne, chunk), 0)
    tri = (lax.broadcasted_iota(jnp.int32, (chunk, chunk), 0)
           < lax.broadcasted_iota(jnp.int32, (chunk, chunk), 1)).astype(BF16)

    def count(c, carry):
        c0 = pl.multiple_of(c * chunk, chunk)
        oh = jnp.zeros((ne, chunk), F32)
        for k in range(TOP_K):
            oh = oh + (eidx == idx_ref[k:k + 1, pl.ds(c0, chunk)]).astype(F32)
        pre_ref[:, pl.ds(c0, chunk)] = jnp.dot(oh.astype(BF16), tri, preferred_element_type=F32) + carry
        return carry + oh.sum(axis=1, keepdims=True)

    counts = lax.fori_loop(0, n_chunk, count, jnp.zeros((ne, 1), F32))
    padded = jnp.ceil(counts * (1.0 / tb)) * tb
    r_i = lax.broadcasted_iota(jnp.int32, (ne, ne), 0)
    c_i = lax.broadcasted_iota(jnp.int32, (ne, ne), 1)
    padded_row = jnp.sum(jnp.where(r_i == c_i, padded, 0.0), axis=0, keepdims=True)
    pstart = jnp.sum(jnp.where(c_i < r_i, padded_row, 0.0), axis=1, keepdims=True)
    pend = pstart + padded

    def place(c, carry):
        c0 = pl.multiple_of(c * chunk, chunk)
        base = pre_ref[:, pl.ds(c0, chunk)] + pstart
        rows = [jnp.sum(jnp.where(eidx == idx_ref[k:k + 1, pl.ds(c0, chunk)], base, 0.0), axis=0, keepdims=True)
                for k in range(TOP_K)]
        rows.append(jnp.zeros((8 - TOP_K, chunk), F32))
        dest_ref[:, pl.ds(c0, chunk)] = jnp.concatenate(rows, axis=0).astype(jnp.int32)
        return carry

    lax.fori_loop(0, n_chunk, place, 0)

    nbp = blk_ref.shape[1]
    starts = (lax.broadcasted_iota(jnp.int32, (ne, nbp), 1) * tb).astype(F32)
    blk_e = jnp.minimum(jnp.sum((pend <= starts).astype(F32), axis=0, keepdims=True), ne - 1.0)
    n_live = jnp.sum(padded, axis=0, keepdims=True) * (1.0 / tb)
    e_col = lax.broadcasted_iota(jnp.int32, (ne, nbp), 0).astype(F32)
    owns = jnp.where(padded > 0.0, 1.0, 0.0)
    later = jnp.where(e_col > blk_e, owns, 0.0)
    nxt = jnp.min(jnp.where(later > 0.0, e_col, float(ne)), axis=0, keepdims=True)
    nxt = jnp.where(nxt >= float(ne), -1.0, nxt)
    rank = jnp.sum(jnp.where(e_col < blk_e, owns, 0.0), axis=0, keepdims=True)
    slot = rank - 2.0 * jnp.floor(rank * 0.5)
    blk_ref[...] = jnp.concatenate([blk_e, jnp.broadcast_to(n_live, (1, nbp)), nxt, slot, jnp.zeros((4, nbp), F32)],
                                   axis=0).astype(jnp.int32)

    r_l = lax.broadcasted_iota(jnp.int32, (ne, LANES), 0)
    c_l = lax.broadcasted_iota(jnp.int32, (ne, LANES), 1)
    to_row = lambda col: jnp.sum(jnp.where(r_l == c_l, col, 0.0), axis=0, keepdims=True)
    ends_ref[...] = jnp.concatenate([to_row(pstart + counts), to_row(pend), jnp.zeros((6, LANES), F32)],
                                    axis=0).astype(jnp.int32)


def route_plan(idx8, ne, tb, n_blocks):
    nt = idx8.shape[1]
    chunk = _pick(nt, (256, 128))
    nbp = -(-n_blocks // LANES) * LANES
    return pl.pallas_call(
        functools.partial(_route_plan_kernel, ne=ne, tb=tb, chunk=chunk),
        out_shape=[jax.ShapeDtypeStruct((8, nt), jnp.int32), jax.ShapeDtypeStruct((8, nbp), jnp.int32),
                   jax.ShapeDtypeStruct((8, LANES), jnp.int32)],
        scratch_shapes=[pltpu.VMEM((ne, nt), F32)],
        compiler_params=pltpu.CompilerParams(vmem_limit_bytes=VMEM_LIMIT),
        name="route_plan",
    )(idx8)


def _dispatch_kernel(dest_ref, ends_ref, h_ref, xs_hbm, zrow, sem, *, rows, ne):
    i = pl.program_id(0)
    base = i * rows

    @pl.when(i == 0)
    def _():
        zrow[...] = jnp.zeros_like(zrow)

        def per_expert(e, carry):
            lo, hi = ends_ref[0, e], ends_ref[1, e]

            def put(r, c):
                dst = xs_hbm.at[pl.ds(pl.multiple_of(r * ROW_TILE, ROW_TILE), ROW_TILE)]
                pltpu.make_async_copy(zrow.at[pl.ds(0, ROW_TILE)], dst, sem.at[1]).start()
                return c

            def done(r, c):
                pltpu.make_async_copy(zrow.at[pl.ds(0, ROW_TILE)], xs_hbm.at[pl.ds(0, ROW_TILE)], sem.at[1]).wait()
                return c

            lax.fori_loop(lo, hi, put, 0)
            lax.fori_loop(lo, hi, done, 0)
            return carry

        lax.fori_loop(0, ne, per_expert, 0)

        blk = zrow.shape[0]
        tb = blk // ROW_TILE
        first_dead = ends_ref[1, ne - 1] // tb
        n_blk = xs_hbm.shape[0] // blk

        def put_blk(j, c):
            pltpu.make_async_copy(zrow, xs_hbm.at[pl.ds(pl.multiple_of(j * blk, blk), blk)], sem.at[1]).start()
            return c

        def done_blk(j, c):
            pltpu.make_async_copy(zrow, xs_hbm.at[pl.ds(0, blk)], sem.at[1]).wait()
            return c

        lax.fori_loop(first_dead, n_blk, put_blk, 0)
        lax.fori_loop(first_dead, n_blk, done_blk, 0)

    def issue(r, carry):
        src = h_ref.at[pl.ds(pl.multiple_of(r * ROW_TILE, ROW_TILE), ROW_TILE)]
        for k in range(TOP_K):
            d = dest_ref[(base + r) * TOP_K + k]
            dst = xs_hbm.at[pl.ds(pl.multiple_of(d * ROW_TILE, ROW_TILE), ROW_TILE)]
            pltpu.make_async_copy(src, dst, sem.at[0]).start(priority=k % 2)
        return carry

    lax.fori_loop(0, rows, issue, 0, unroll=4)
    for _ in range(TOP_K):
        pltpu.make_async_copy(h_ref, xs_hbm.at[pl.ds(0, rows * ROW_TILE)], sem.at[0]).wait()


def moe_dispatch(h_tiles, dest1, ends, cap, ne):
    nt = h_tiles.shape[0] // ROW_TILE
    rows = _pick(nt, (256, 128, 64, 32, 16, 8))
    return pl.pallas_call(
        functools.partial(_dispatch_kernel, rows=rows, ne=ne),
        grid=(nt // rows,),
        in_specs=[pl.BlockSpec(memory_space=pltpu.SMEM),
                  pl.BlockSpec(memory_space=pltpu.SMEM),
                  pl.BlockSpec((rows * ROW_TILE, LANES), lambda i: (i, 0))],
        out_specs=pl.BlockSpec(memory_space=pl.ANY),
        out_shape=jax.ShapeDtypeStruct((cap * ROW_TILE, LANES), h_tiles.dtype),
        scratch_shapes=[pltpu.VMEM((MOE_ROWS * ROW_TILE, LANES), h_tiles.dtype), pltpu.SemaphoreType.DMA((2,))],
        compiler_params=_cparams("arbitrary"),
        name="moe_dispatch",
    )(dest1, ends, h_tiles)


def _expert_kernel(be_ref, nb_ref, nxt_ref, slot_ref, x_ref, bg_ref, bu_ref, bd_ref, wg_hbm, wu_hbm, wd_hbm, o_ref,
                   wg_st, wu_st, wd_st, wg_bf, wu_bf, wd_bf, sem, *, layer):
    i = pl.program_id(0)
    e = be_ref[i]
    live = i < nb_ref[0]
    first = (i == 0) | (e != be_ref[jnp.maximum(i - 1, 0)])
    slot = slot_ref[i]
    half = wg_bf.shape[0] // 2

    def weight_copies(ex, s):
        return (pltpu.make_async_copy(wg_hbm.at[layer, ex], wg_st.at[s], sem.at[s, 0]),
                pltpu.make_async_copy(wu_hbm.at[layer, ex], wu_st.at[s], sem.at[s, 1]),
                pltpu.make_async_copy(wd_hbm.at[layer, ex], wd_st.at[s], sem.at[s, 2]))

    @pl.when(i == 0)
    def _():
        for cp in weight_copies(e, slot):
            cp.start()

    @pl.when(live & first)
    def _():
        for cp in weight_copies(e, slot):
            cp.wait()

        @pl.when(nxt_ref[i] >= 0)
        def _():
            for cp in weight_copies(nxt_ref[i], 1 - slot):
                cp.start()

        wg_bf[...] = wg_st[slot].astype(BF16)
        wu_bf[...] = wu_st[slot].astype(BF16)
        wd_bf[...] = wd_st[slot].astype(BF16)

    @pl.when(live)
    def _():
        tb = x_ref.shape[0] // ROW_TILE
        lo, hi = _unpack_pairs(jnp.concatenate(_load_row_tiles(x_ref, tb), axis=1))
        lo = lo.astype(BF16)
        hi = hi.astype(BF16)

        def proj(w_bf, b_ref):
            return (jnp.dot(lo, w_bf[0:half, :], preferred_element_type=F32)
                    + jnp.dot(hi, w_bf[half:2 * half, :], preferred_element_type=F32) + b_ref[...])

        g = jnp.minimum(proj(wg_bf, bg_ref), SWIGLU_LIMIT)
        u = jnp.clip(proj(wu_bf, bu_ref), -SWIGLU_LIMIT, SWIGLU_LIMIT)
        a = g * jax.nn.sigmoid(SWIGLU_ALPHA * g) * (u + 1.0)
        y = jnp.dot(a.astype(BF16), wd_bf[...], preferred_element_type=F32) + bd_ref[...]
        _store_row_tiles(o_ref, _pack_pairs(y[:, 0:half], y[:, half:2 * half]))

    @pl.when(jnp.logical_not(live))
    def _():
        o_ref[...] = jnp.zeros_like(o_ref)


def expert_ffn(x_sorted, blk, n_blocks, layer, w_gate, b_gate, w_up, b_up, w_down, b_down):
    depth, ne, d, f = w_gate.shape
    blk_rows = MOE_ROWS * ROW_TILE
    bias = lambda n: pl.BlockSpec((None, 1, n), lambda i, be, nb, nx, sl: (layer * ne + be[i], 0, 0))
    hbm = pl.BlockSpec(memory_space=pl.ANY)
    grid_spec = pltpu.PrefetchScalarGridSpec(
        num_scalar_prefetch=4,
        grid=(n_blocks,),
        in_specs=[pl.BlockSpec((blk_rows, LANES), lambda i, be, nb, nx, sl: (jnp.minimum(i, nb[0] - 1), 0)),
                  bias(f), bias(f), bias(d), hbm, hbm, hbm],
        out_specs=pl.BlockSpec((blk_rows, LANES), lambda i, be, nb, nx, sl: (i, 0)),
        scratch_shapes=[pltpu.VMEM((2, d, f), F32), pltpu.VMEM((2, d, f), F32), pltpu.VMEM((2, f, d), F32),
                        pltpu.VMEM((d, f), BF16), pltpu.VMEM((d, f), BF16), pltpu.VMEM((f, d), BF16),
                        pltpu.SemaphoreType.DMA((2, 3))],
    )
    return pl.pallas_call(
        functools.partial(_expert_kernel, layer=layer),
        grid_spec=grid_spec,
        out_shape=jax.ShapeDtypeStruct(x_sorted.shape, jnp.uint32),
        compiler_params=_cparams("arbitrary"),
        name="expert_ffn",
    )(blk[0, :n_blocks], blk[1, :1], blk[2, :n_blocks], blk[3, :n_blocks], x_sorted,
      b_gate.reshape(depth * ne, 1, f), b_up.reshape(depth * ne, 1, f), b_down.reshape(depth * ne, 1, d),
      w_gate, w_up, w_down)


def _combine_kernel(dest_ref, x_ref, w_ref, gate_ref, y_hbm, o_ref, buf, sem, *, tm):
    i = pl.program_id(0)
    n = pl.num_programs(0)
    slot = i % 2
    half = ROW_TILE * LANES

    def fetch(tile, s):
        def issue(r, carry):
            for k in range(TOP_K):
                d = dest_ref[(tile * tm + r) * TOP_K + k]
                src = y_hbm.at[pl.ds(pl.multiple_of(d * ROW_TILE, ROW_TILE), ROW_TILE)]
                dst = buf.at[s, k, pl.ds(pl.multiple_of(r * ROW_TILE, ROW_TILE), ROW_TILE)]
                pltpu.make_async_copy(src, dst, sem.at[s]).start(priority=k % 2)
            return carry
        lax.fori_loop(0, tm, issue, 0, unroll=4)

    @pl.when(i == 0)
    def _():
        fetch(0, 0)

    @pl.when(i + 1 < n)
    def _():
        fetch(i + 1, 1 - slot)

    for k in range(TOP_K):
        pltpu.make_async_copy(y_hbm.at[pl.ds(0, tm * ROW_TILE)], buf.at[slot, k], sem.at[slot]).wait()

    ws = [w_ref[:, k:k + 1] for k in range(TOP_K)]
    for j in range(ROW_TILE):
        acc_lo = None
        acc_hi = None
        for k in range(TOP_K):
            lo, hi = _unpack_pairs(buf[slot, k, pl.ds(j, tm, stride=ROW_TILE), :])
            acc_lo = ws[k] * lo if acc_lo is None else acc_lo + ws[k] * lo
            acc_hi = ws[k] * hi if acc_hi is None else acc_hi + ws[k] * hi
        c_lo = slice(j * LANES, (j + 1) * LANES)
        c_hi = slice(half + j * LANES, half + (j + 1) * LANES)
        o_ref[:, c_lo] = x_ref[:, c_lo] + gate_ref[:, c_lo] * acc_lo
        o_ref[:, c_hi] = x_ref[:, c_hi] + gate_ref[:, c_hi] * acc_hi


def moe_combine(st, x, y_sorted, dest1, w_tok, mod, k_gate):
    nt, d = x.shape
    assert d // 2 == ROW_TILE * LANES
    tm = _pick(st.tm, (128, 64, 32, 16, 8))
    return pl.pallas_call(
        functools.partial(_combine_kernel, tm=tm),
        grid=(nt // tm,),
        in_specs=[pl.BlockSpec(memory_space=pltpu.SMEM),
                  pl.BlockSpec((tm, d), lambda i: (i, 0)),
                  pl.BlockSpec((tm, TOP_K), lambda i: (i, 0)),
                  _mod_spec(st, tm, k_gate, d, 1, 0),
                  pl.BlockSpec(memory_space=pl.ANY)],
        out_specs=pl.BlockSpec((tm, d), lambda i: (i, 0)),
        out_shape=jax.ShapeDtypeStruct((nt, d), F32),
        scratch_shapes=[pltpu.VMEM((2, TOP_K, tm * ROW_TILE, LANES), jnp.uint32), pltpu.SemaphoreType.DMA((2,))],
        input_output_aliases={1: 0},
        compiler_params=_cparams("arbitrary"),
        name="moe_combine",
    )(dest1, x, w_tok, mod, y_sorted)


def moe_layer(st, x, g, mod, layer, w_router, b_router, w_gate, b_gate, w_up, b_up, w_down, b_down):
    nt, d = x.shape
    ne = w_router.shape[1]
    tb = MOE_ROWS
    h_packed, idx8, wgt8 = norm_router(st, x, g, mod, 3, 4, w_router, b_router)
    n_assign = nt * TOP_K
    cap = -(-(n_assign + ne * (tb - 1)) // tb) * tb
    n_blocks = cap // tb
    dest8, blk, ends = route_plan(idx8, ne, tb, n_blocks)
    dest1 = dest8[:TOP_K].T.reshape(-1)
    x_sorted = moe_dispatch(h_packed, dest1, ends, cap, ne)
    y_sorted = expert_ffn(x_sorted, blk, n_blocks, layer, w_gate, b_gate, w_up, b_up, w_down, b_down)
    return moe_combine(st, x, y_sorted, dest1, wgt8[:TOP_K].T, mod, 5)


def even_mixer(st, x, h, mod, i, w_in, w_out, s5p, d_skip, glu_w, glu_b, fnet_w, tables):
    d = x.shape[1]
    fw = fnet_w.shape[0] * fnet_w.shape[1]
    sw = d - fw
    p = matmul([h], w_in, i, st=st)
    b_bd, c_bd, lam8 = s5p
    y_f, y_b = s5_scan(st, p, sw, b_bd, c_bd, lam8)
    gl = s5_gelu(y_f, y_b, p, d_skip, sw)
    a = matmul([gl], glu_w, i, mode="glu", extra=(gl, glu_b), st=st)
    cd, sd, tab_s, tab_l = tables
    pcs = fnet_chan(p, sw // fw, fnet_w, cd, sd)
    fm = fnet_seq(st, pcs, tab_l, tab_s)
    return matmul([a, fm], w_out, i, mode="resid", extra=(x, mod, 2), st=st)


def odd_mixer(st, x, h, mod, i, w_in, w_out, pool_w, pool_scale, q_gain, k_gain, rope):
    d = x.shape[1]
    pw = pool_w.shape[0] * pool_w.shape[1]
    n_q = (d - pw) // HEAD_DIM
    p = matmul([h], w_in, i, st=st)
    cos_t, sin_t = rope
    gains = jnp.stack([q_gain, k_gain]).reshape(2, 1, HEAD_DIM)
    qk = qk_prep(st, p, gains, cos_t, sin_t, n_q // KV_HEADS, HEAD_DIM ** -0.5 * math.log2(math.e))
    v_col0 = (pw + n_q * HEAD_DIM + KV_HEADS * HEAD_DIM) // HEAD_DIM
    att = attention(st, qk, p, v_col0, n_q)
    pm = pool_mix(st, p, pool_w, pool_scale)
    return matmul([pm, att], w_out, i, mode="resid", extra=(x, mod, 2), st=st)


def kernel(x, c, ctx, c_ctx, ada_w, ada_b, norm_mix_g, norm_ffn_g, ev_w_in, ev_w_out, s5_lam_re, s5_lam_im, s5_log_dt, s5_b_re, s5_b_im, s5_c_re, s5_c_im, s5_d, s5_glu_w, s5_glu_b, fnet_w, od_w_in, od_w_out, pool_w, pool_scale, q_gain, k_gain, router_w, router_b, exp_w_gate, exp_b_gate, exp_w_up, exp_b_up, exp_w_down, exp_b_down, final_g):
    batch, seq, d = x.shape
    ctx_len = ctx.shape[1]
    depth = ada_w.shape[0]
    st = Stream(batch, seq, ctx_len)
    xs = jnp.concatenate([ctx, x], axis=1).reshape(st.nt, d)

    cond8 = jnp.concatenate([c, c_ctx[None], jnp.zeros((8 - batch - 1, d), F32)], axis=0)
    mod_all = adaln_all(cond8, ada_w, ada_b)
    mod_all = mod_all.reshape(depth, 8, 6, 1, d).transpose(0, 2, 1, 3, 4)

    dh = fnet_w.shape[2]
    cd, sd = dft_tables(dh)
    tab_s = tuple(t.astype(BF16) for t in dft_tables(seq))
    tab_l = tuple(t.astype(BF16) for t in dft_tables(ctx_len))
    tables = (cd, sd, tab_s, tab_l)
    rope = rope_tables(seq, st.tm)

    for layer in range(depth):
        i = layer // 2
        mod = mod_all[layer]
        h = norm_mod(st, xs, norm_mix_g[layer], mod, 0, 1)
        if layer % 2 == 0:
            s5p = s5_tables(s5_lam_re[i], s5_lam_im[i], s5_log_dt[i], s5_b_re[i], s5_b_im[i], s5_c_re[i], s5_c_im[i])
            xs = even_mixer(st, xs, h, mod, i, ev_w_in, ev_w_out, s5p, s5_d[i], s5_glu_w, s5_glu_b[i],
                            fnet_w[i], tables)
        else:
            xs = odd_mixer(st, xs, h, mod, i, od_w_in, od_w_out, pool_w[i], pool_scale[i], q_gain[i], k_gain[i],
                           rope)
        xs = moe_layer(st, xs, norm_ffn_g[layer], mod, layer, router_w[layer], router_b[layer],
                       exp_w_gate, exp_b_gate, exp_w_up, exp_b_up, exp_w_down, exp_b_down)
    return final_norm(st, xs, final_g).reshape(batch, seq, d)
```

```python
import functools
import math

import jax
import jax.numpy as jnp
from jax import lax
from jax.experimental import pallas as pl
from jax.experimental.pallas import tpu as pltpu

F32 = jnp.float32
BF16 = jnp.bfloat16
EPS = 1e-6

GRID_W = 64
FNET_HEADS = 4
S5_GROUP = 16
S5_STATE = 64
POOL_WINDOWS = (2, 4, 8, 16)
HEAD_DIM = 128
KV_HEADS = 4
ROPE_THETA = 10000.0
TOP_K = 4
SWIGLU_LIMIT = 7.0
SWIGLU_ALPHA = 1.702

LANES = 128
S5_BLOCK_GROUPS = LANES // S5_GROUP
VMEM_LIMIT = 48 * 1024 * 1024
MOE_ROWS = 256


def _cparams(*sem):
    return pltpu.CompilerParams(dimension_semantics=sem, vmem_limit_bytes=VMEM_LIMIT)


def _pick(n, prefs):
    for p in prefs:
        if n % p == 0:
            return p
    return n


def _adaln_kernel(c_ref, w_ref, b_ref, o_ref):
    c = c_ref[...]
    a = (c * jax.nn.sigmoid(c)).astype(BF16)
    o_ref[...] = jnp.dot(a, w_ref[...].astype(BF16), preferred_element_type=F32) + b_ref[...]


def adaln_all(cond8, ada_w, ada_b):
    depth, d, n = ada_w.shape
    tn = _pick(n, (1024, 512, 256, 128))
    return pl.pallas_call(
        _adaln_kernel,
        grid=(depth, n // tn),
        in_specs=[pl.BlockSpec((8, d), lambda l, j: (0, 0)),
                  pl.BlockSpec((None, d, tn), lambda l, j: (l, 0, j)),
                  pl.BlockSpec((None, 1, tn), lambda l, j: (l, 0, j))],
        out_specs=pl.BlockSpec((None, 8, tn), lambda l, j: (l, 0, j)),
        out_shape=jax.ShapeDtypeStruct((depth, 8, n), F32),
        compiler_params=_cparams("arbitrary", "arbitrary"),
        name="adaln",
    )(cond8, ada_w, ada_b.reshape(depth, 1, n))


def _norm_mod_kernel(x_ref, g_ref, sh_ref, sc_ref, o_ref):
    x = x_ref[...]
    y = x * lax.rsqrt(jnp.mean(x * x, axis=-1, keepdims=True) + EPS) * g_ref[...]
    o_ref[...] = (y * (1.0 + sc_ref[...]) + sh_ref[...]).astype(o_ref.dtype)


def _norm_kernel(x_ref, g_ref, o_ref):
    x = x_ref[...]
    o_ref[...] = (x * lax.rsqrt(jnp.mean(x * x, axis=-1, keepdims=True) + EPS) * g_ref[...]).astype(o_ref.dtype)


class Stream:
    def __init__(self, batch, seq, ctx_len):
        self.b, self.s, self.l = batch, seq, ctx_len
        self.t = seq + ctx_len
        self.nt = batch * self.t
        self.tm = _pick(math.gcd(seq, ctx_len), (256, 128, 64, 32, 16))

    def mod_row(self, tm):
        tpb, l, b = self.t // tm, self.l, self.b
        return lambda i: jnp.where((i % tpb) * tm < l, b, i // tpb)


def _mod_spec(st, tm, k, d, grid_rank, row_axis):
    mr = st.mod_row(tm)
    if grid_rank == 1:
        return pl.BlockSpec((None, None, 1, d), lambda i: (k, mr(i), 0, 0))
    if row_axis == 1:
        return pl.BlockSpec((None, None, 1, d), lambda j, i: (k, mr(i), 0, j))
    raise ValueError


def norm_mod(st, x, g, mod, k_shift, k_scale, out_dtype=BF16):
    nt, d = x.shape
    tm = _pick(st.tm, (256, 128, 64, 32, 16))
    return pl.pallas_call(
        _norm_mod_kernel,
        grid=(nt // tm,),
        in_specs=[pl.BlockSpec((tm, d), lambda i: (i, 0)),
                  pl.BlockSpec((1, d), lambda i: (0, 0)),
                  _mod_spec(st, tm, k_shift, d, 1, 0),
                  _mod_spec(st, tm, k_scale, d, 1, 0)],
        out_specs=pl.BlockSpec((tm, d), lambda i: (i, 0)),
        out_shape=jax.ShapeDtypeStruct((nt, d), out_dtype),
        compiler_params=_cparams("arbitrary"),
        name="norm_mod",
    )(x, g.reshape(1, d), mod, mod)


def final_norm(st, x, g):
    d = x.shape[1]
    tm = st.tm
    tpb, lt, ns = st.t // tm, st.l // tm, st.s // tm
    return pl.pallas_call(
        _norm_kernel,
        grid=(st.b, ns),
        in_specs=[pl.BlockSpec((tm, d), lambda b, i: (b * tpb + lt + i, 0)),
                  pl.BlockSpec((1, d), lambda b, i: (0, 0))],
        out_specs=pl.BlockSpec((tm, d), lambda b, i: (b * ns + i, 0)),
        out_shape=jax.ShapeDtypeStruct((st.b * st.s, d), F32),
        compiler_params=_cparams("arbitrary", "arbitrary"),
        name="final_norm",
    )(x, g.reshape(1, d))


def _mm_kernel(*refs, n_a, k_offs, mode):
    a_refs = refs[:n_a]
    w_ref = refs[n_a]
    rest = refs[n_a + 1:]
    wbf_ref = rest[-1]
    o_ref = rest[-2]

    @pl.when(pl.program_id(1) == 0)
    def _():
        wbf_ref[...] = w_ref[...].astype(BF16)

    acc = None
    for a_ref, (k0, k1) in zip(a_refs, k_offs):
        part = jnp.dot(a_ref[...], wbf_ref[k0:k1, :], preferred_element_type=F32)
        acc = part if acc is None else acc + part
    if mode == "plain":
        o_ref[...] = acc.astype(o_ref.dtype)
    elif mode == "resid":
        x_ref, gate_ref = rest[0], rest[1]
        o_ref[...] = x_ref[...] + gate_ref[...] * acc
    elif mode == "glu":
        g_ref, b_ref = rest[0], rest[1]
        o_ref[...] = (g_ref[...].astype(F32) * jax.nn.sigmoid(acc + b_ref[...])).astype(o_ref.dtype)
    else:
        raise ValueError(mode)


def _norm_mm_kernel(x_ref, g_ref, sh_ref, sc_ref, w_ref, o_ref, wbf_ref):
    @pl.when(pl.program_id(1) == 0)
    def _():
        wbf_ref[...] = w_ref[...].astype(BF16)

    x = x_ref[...]
    y = x * lax.rsqrt(jnp.mean(x * x, axis=-1, keepdims=True) + EPS) * g_ref[...]
    h = (y * (1.0 + sc_ref[...]) + sh_ref[...]).astype(BF16)
    o_ref[...] = jnp.dot(h, wbf_ref[...], preferred_element_type=F32).astype(o_ref.dtype)


def norm_matmul(st, x, g, mod, k_shift, k_scale, w_stack, w_idx, tn_prefs=(2048, 1536, 1024, 512, 256, 128)):
    m, k = x.shape
    n = w_stack.shape[2]
    tm = st.tm
    tn = _pick(n, tn_prefs)
    mr = st.mod_row(tm)
    mod_spec = lambda kk: pl.BlockSpec((None, None, 1, k), lambda j, i: (kk, mr(i), 0, 0))
    return pl.pallas_call(
        _norm_mm_kernel,
        grid=(n // tn, m // tm),
        in_specs=[pl.BlockSpec((tm, k), lambda j, i: (i, 0)),
                  pl.BlockSpec((1, k), lambda j, i: (0, 0)),
                  mod_spec(k_shift), mod_spec(k_scale),
                  pl.BlockSpec((None, k, tn), lambda j, i: (w_idx, 0, j), pipeline_mode=pl.Buffered(1))],
        out_specs=pl.BlockSpec((tm, tn), lambda j, i: (i, j)),
        out_shape=jax.ShapeDtypeStruct((m, n), BF16),
        scratch_shapes=[pltpu.VMEM((k, tn), BF16)],
        compiler_params=_cparams("arbitrary", "arbitrary"),
        name="norm_mm",
    )(x, g.reshape(1, k), mod, mod, w_stack)


def _gelu_glu_kernel(yf_ref, yb_ref, u_ref, d_ref, w_ref, b_ref, o_ref, wbf_ref):
    @pl.when(pl.program_id(0) == 0)
    def _():
        wbf_ref[...] = w_ref[...].astype(BF16)

    y = yf_ref[...].astype(F32) + yb_ref[...].astype(F32) + d_ref[...] * u_ref[...].astype(F32)
    g = jax.nn.gelu(y).astype(BF16)
    acc = jnp.dot(g, wbf_ref[...], preferred_element_type=F32)
    o_ref[...] = (g.astype(F32) * jax.nn.sigmoid(acc + b_ref[...])).astype(o_ref.dtype)


def gelu_glu(st, y_f, y_b, p, d_skip, w_stack, w_idx, bias):
    m, width = y_f.shape
    tm = st.tm
    row = pl.BlockSpec((tm, width), lambda i: (i, 0))
    vec = pl.BlockSpec((1, width), lambda i: (0, 0))
    return pl.pallas_call(
        _gelu_glu_kernel,
        grid=(m // tm,),
        in_specs=[row, row, row, vec,
                  pl.BlockSpec((None, width, width), lambda i: (w_idx, 0, 0), pipeline_mode=pl.Buffered(1)),
                  vec],
        out_specs=row,
        out_shape=jax.ShapeDtypeStruct((m, width), BF16),
        scratch_shapes=[pltpu.VMEM((width, width), BF16)],
        compiler_params=_cparams("arbitrary"),
        name="gelu_glu",
    )(y_f, y_b, p, d_skip.reshape(1, width), w_stack, bias.reshape(1, width))


def matmul(a_parts, w_stack, w_idx, mode="plain", out_dtype=BF16, extra=(), st=None,
           tn_prefs=(2048, 1536, 1024, 512, 256, 128)):
    m = a_parts[0].shape[0]
    _, k, n = w_stack.shape
    tm = _pick(m if st is None else st.tm, (512, 256, 128, 64, 32, 16, 8))
    tn = _pick(n, tn_prefs)
    k_offs, off = [], 0
    in_specs, args = [], []
    for a in a_parts:
        ka = a.shape[1]
        k_offs.append((off, off + ka))
        off += ka
        in_specs.append(pl.BlockSpec((tm, ka), lambda j, i: (i, 0)))
        args.append(a)
    assert off == k
    in_specs.append(pl.BlockSpec((None, k, tn), lambda j, i: (w_idx, 0, j), pipeline_mode=pl.Buffered(1)))
    args.append(w_stack)
    io_alias = {}
    if mode == "resid":
        x, mod, k_gate = extra
        in_specs.append(pl.BlockSpec((tm, tn), lambda j, i: (i, j)))
        args.append(x)
        in_specs.append(_mod_spec(st, tm, k_gate, tn, 2, 1))
        args.append(mod)
        io_alias = {len(args) - 2: 0}
        out_dtype = F32
    elif mode == "glu":
        g, bias = extra
        in_specs.append(pl.BlockSpec((tm, tn), lambda j, i: (i, j)))
        args.append(g)
        in_specs.append(pl.BlockSpec((1, tn), lambda j, i: (0, j)))
        args.append(bias.reshape(1, n))
    return pl.pallas_call(
        functools.partial(_mm_kernel, n_a=len(a_parts), k_offs=tuple(k_offs), mode=mode),
        grid=(n // tn, m // tm),
        in_specs=in_specs,
        out_specs=pl.BlockSpec((tm, tn), lambda j, i: (i, j)),
        out_shape=jax.ShapeDtypeStruct((m, n), out_dtype),
        scratch_shapes=[pltpu.VMEM((k, tn), BF16)],
        input_output_aliases=io_alias,
        compiler_params=_cparams("arbitrary", "arbitrary"),
        name="mm_" + mode,
    )(*args)


def _s5_kernel(uf_ref, ub_ref, b_ref, c_ref, lam_ref, yf_ref, yb_ref, h_ref, bu_ref, u_scr, y_scr, *, tc, jb):
    half = b_ref.shape[2] // 2
    nbat = uf_ref.shape[0]

    @pl.when(pl.program_id(1) == 0)
    def _():
        h_ref[...] = jnp.zeros_like(h_ref)

    rows = tc * 8
    is_fwd = (lax.broadcasted_iota(jnp.int32, (rows, LANES), 0) % 8) < 4
    flip = (lax.broadcasted_iota(jnp.int32, (tc, tc), 0)
            + lax.broadcasted_iota(jnp.int32, (tc, tc), 1) == tc - 1).astype(BF16)
    for q in range(jb):
        lanes = slice(q * LANES, (q + 1) * LANES)
        for b in range(nbat):
            u_scr[q, pl.ds(b, tc, stride=8), :] = uf_ref[b, :, lanes].astype(F32)
            u_scr[q, pl.ds(nbat + b, tc, stride=8), :] = jnp.dot(flip, ub_ref[b, :, lanes],
                                                                 preferred_element_type=F32)
        u = u_scr[q]
        lhs = jnp.concatenate([jnp.where(is_fwd, u, 0.0), jnp.where(is_fwd, 0.0, u)], axis=1).astype(BF16)
        bu_ref[q] = jnp.dot(lhs, b_ref[q], preferred_element_type=F32)

    for q in range(jb):
        lam_r = lam_ref[q, 0]
        lam_i = lam_ref[q, 1]

        def step(t, carry, q=q, lam_r=lam_r, lam_i=lam_i):
            hr, hi = carry
            r0 = pl.multiple_of(t * 8, 8)
            nhr = lam_r * hr - lam_i * hi + bu_ref[q, pl.ds(r0, 8), 0:half]
            nhi = lam_r * hi + lam_i * hr + bu_ref[q, pl.ds(r0, 8), half:2 * half]
            bu_ref[q, pl.ds(r0, 8), 0:half] = nhr
            bu_ref[q, pl.ds(r0, 8), half:2 * half] = nhi
            return nhr, nhi

        hr, hi = lax.fori_loop(0, tc, step, (h_ref[q, :, 0:half], h_ref[q, :, half:2 * half]), unroll=True)
        h_ref[q, :, 0:half] = hr
        h_ref[q, :, half:2 * half] = hi

    for q in range(jb):
        lanes = slice(q * LANES, (q + 1) * LANES)
        y2 = jnp.dot(bu_ref[q].astype(BF16), c_ref[q], preferred_element_type=F32)
        y_scr[q] = jnp.where(is_fwd, y2[:, 0:LANES], y2[:, LANES:2 * LANES])
        for b in range(nbat):
            yf_ref[b, :, lanes] = y_scr[q, pl.ds(b, tc, stride=8), :].astype(yf_ref.dtype)
            yb = y_scr[q, pl.ds(nbat + b, tc, stride=8), :].astype(BF16)
            yb_ref[b, :, lanes] = jnp.dot(flip, yb, preferred_element_type=F32).astype(yb_ref.dtype)


def s5_scan(st, p, width, b_bd, c_bd, lam8):
    bsz, t_len, l_len = st.b, st.t, st.l
    assert 2 * bsz == 8
    d = p.shape[1]
    nb = width // LANES
    jb = 2 if nb % 2 == 0 else 1
    tc = _pick(math.gcd(st.s, l_len), (128, 64, 32, 16, 8))
    rows = tc * 8
    ns = b_bd.shape[2]
    n_l, n_c = l_len // tc, t_len // tc

    def bwd_chunk(c):
        return jnp.where(c < n_l, n_l - 1 - c, n_c + n_l - 1 - c)

    p3 = p.reshape(bsz, t_len, d)
    out = jax.ShapeDtypeStruct((bsz, t_len, width), BF16)
    y_f, y_b = pl.pallas_call(
        functools.partial(_s5_kernel, tc=tc, jb=jb),
        grid=(nb // jb, n_c),
        in_specs=[pl.BlockSpec((bsz, tc, jb * LANES), lambda j, c: (0, c, j)),
                  pl.BlockSpec((bsz, tc, jb * LANES), lambda j, c: (0, bwd_chunk(c), j)),
                  pl.BlockSpec((jb, 2 * LANES, ns), lambda j, c: (j, 0, 0)),
                  pl.BlockSpec((jb, ns, 2 * LANES), lambda j, c: (j, 0, 0)),
                  pl.BlockSpec((jb, 2, 8, ns // 2), lambda j, c: (j, 0, 0, 0))],
        out_specs=[pl.BlockSpec((bsz, tc, jb * LANES), lambda j, c: (0, c, j)),
                   pl.BlockSpec((bsz, tc, jb * LANES), lambda j, c: (0, bwd_chunk(c), j))],
        out_shape=[out, out],
        scratch_shapes=[pltpu.VMEM((jb, 8, ns), F32), pltpu.VMEM((jb, rows, ns), F32),
                        pltpu.VMEM((jb, rows, LANES), F32), pltpu.VMEM((jb, rows, LANES), F32)],
        compiler_params=_cparams("arbitrary", "arbitrary"),
        name="s5_scan",
    )(p3, p3, b_bd, c_bd, lam8)
    return y_f.reshape(bsz * t_len, width), y_b.reshape(bsz * t_len, width)


def _s5_gelu_kernel(yf_ref, yb_ref, u_ref, d_ref, o_ref):
    y = yf_ref[...].astype(F32) + yb_ref[...].astype(F32) + d_ref[...] * u_ref[...].astype(F32)
    o_ref[...] = jax.nn.gelu(y).astype(o_ref.dtype)


def s5_gelu(y_f, y_b, p, d_skip, width):
    nt = y_f.shape[0]
    tm = _pick(nt, (512, 256, 128, 64, 32, 16, 8))
    row = pl.BlockSpec((tm, width), lambda i: (i, 0))
    return pl.pallas_call(
        _s5_gelu_kernel,
        grid=(nt // tm,),
        in_specs=[row, row, row, pl.BlockSpec((1, width), lambda i: (0, 0))],
        out_specs=row,
        out_shape=jax.ShapeDtypeStruct((nt, width), BF16),
        compiler_params=_cparams("arbitrary"),
        name="s5_gelu",
    )(y_f, y_b, p, d_skip.reshape(1, width))


def s5_tables(lam_re, lam_im, log_dt, b_re, b_im, c_re, c_im):
    _, g, p = lam_re.shape
    c = b_re.shape[-1]
    nb = g // S5_BLOCK_GROUPS
    gb = S5_BLOCK_GROUPS
    lam = lax.complex(lam_re.astype(F32), lam_im.astype(F32))
    dt = jnp.exp(log_dt.astype(F32))[..., None]
    lam_bar = jnp.exp(lam * dt)
    b_bar = ((lam_bar - 1.0) / lam)[..., None] * lax.complex(b_re.astype(F32), b_im.astype(F32))
    eye = jnp.eye(gb, dtype=F32)
    bb = b_bar.reshape(2, nb, gb, p, c)

    def b_lay(z):
        return jnp.einsum('djgpc,gh->jdgchp', z, eye).reshape(nb, 2 * gb * c, gb * p)

    b_bd = jnp.concatenate([b_lay(bb.real), b_lay(bb.imag)], axis=-1).astype(BF16)
    cc = lax.complex(c_re.astype(F32), c_im.astype(F32)).reshape(2, nb, gb, c, p)

    def c_lay(z):
        return jnp.einsum('djgcp,gh->jgpdhc', z, eye).reshape(nb, gb * p, 2 * gb * c)

    c_bd = jnp.concatenate([c_lay(cc.real), -c_lay(cc.imag)], axis=1).astype(BF16)
    lb = lam_bar.reshape(2, nb, gb * p)
    lam8 = jnp.stack([lb.real, lb.imag], axis=0)
    lam8 = jnp.repeat(lam8.transpose(2, 0, 1, 3), 4, axis=2)
    return b_bd, c_bd, lam8


def _fnet_chan_kernel(f_ref, cd_ref, sd_ref, w_ref, o_ref, wc_ref):
    nh, dh = w_ref.shape[0], w_ref.shape[1]

    @pl.when(pl.program_id(0) == 0)
    def _():
        for h in range(nh):
            wh = w_ref[h]
            wc_ref[h, :, 0:dh] = jnp.dot(cd_ref[...], wh, preferred_element_type=F32,
                                         precision=lax.Precision.HIGHEST).astype(BF16)
            wc_ref[h, :, dh:2 * dh] = jnp.dot(sd_ref[...], wh, preferred_element_type=F32,
                                              precision=lax.Precision.HIGHEST).astype(BF16)

    for h in range(nh):
        r = jnp.dot(f_ref[:, h * dh:(h + 1) * dh], wc_ref[h], preferred_element_type=F32)
        o_ref[:, h * dh:(h + 1) * dh] = r[:, 0:dh].astype(o_ref.dtype)
        o_ref[:, (nh + h) * dh:(nh + h + 1) * dh] = r[:, dh:2 * dh].astype(o_ref.dtype)


def fnet_chan(p, col_block, fnet_w, cd, sd):
    nt = p.shape[0]
    nh, dh, _ = fnet_w.shape
    width = nh * dh
    tm = _pick(nt, (512, 256, 128, 64, 32, 16, 8))
    return pl.pallas_call(
        _fnet_chan_kernel,
        grid=(nt // tm,),
        in_specs=[pl.BlockSpec((tm, width), lambda i: (i, col_block)),
                  pl.BlockSpec((dh, dh), lambda i: (0, 0)),
                  pl.BlockSpec((dh, dh), lambda i: (0, 0)),
                  pl.BlockSpec((nh, dh, dh), lambda i: (0, 0, 0))],
        out_specs=pl.BlockSpec((tm, 2 * width), lambda i: (i, 0)),
        out_shape=jax.ShapeDtypeStruct((nt, 2 * width), BF16),
        scratch_shapes=[pltpu.VMEM((nh, dh, 2 * dh), BF16)],
        compiler_params=_cparams("arbitrary"),
        name="fnet_chan",
    )(p, cd, sd, fnet_w)


def _fnet_seq_kernel(cl_ref, sl_ref, cs_ref, ss_ref, p_ref, o_ref, *, l_len):
    w = o_ref.shape[1]
    tm = o_ref.shape[0]
    t_len = p_ref.shape[0]
    in_ctx = pl.program_id(1) * tm < l_len

    @pl.when(in_ctx)
    def _():
        o_ref[...] = (jnp.dot(cl_ref[...], p_ref[0:l_len, 0:w], preferred_element_type=F32)
                      - jnp.dot(sl_ref[...], p_ref[0:l_len, w:2 * w], preferred_element_type=F32)).astype(o_ref.dtype)

    @pl.when(jnp.logical_not(in_ctx))
    def _():
        o_ref[...] = (jnp.dot(cs_ref[...], p_ref[l_len:t_len, 0:w], preferred_element_type=F32)
                      - jnp.dot(ss_ref[...], p_ref[l_len:t_len, w:2 * w], preferred_element_type=F32)).astype(o_ref.dtype)


def fnet_seq(st, pcs, tab_l, tab_s):
    w2 = pcs.shape[1]
    tm = st.tm
    tpb, lt = st.t // tm, st.l // tm
    ctx_tile = lambda b, i: (jnp.minimum(i, lt - 1), 0)
    lat_tile = lambda b, i: (jnp.maximum(i - lt, 0), 0)
    return pl.pallas_call(
        functools.partial(_fnet_seq_kernel, l_len=st.l),
        grid=(st.b, tpb),
        in_specs=[pl.BlockSpec((tm, st.l), ctx_tile), pl.BlockSpec((tm, st.l), ctx_tile),
                  pl.BlockSpec((tm, st.s), lat_tile), pl.BlockSpec((tm, st.s), lat_tile),
                  pl.BlockSpec((st.t, w2), lambda b, i: (b, 0))],
        out_specs=pl.BlockSpec((tm, w2 // 2), lambda b, i: (b * tpb + i, 0)),
        out_shape=jax.ShapeDtypeStruct((st.nt, w2 // 2), BF16),
        compiler_params=_cparams("arbitrary", "arbitrary"),
        name="fnet_seq",
    )(tab_l[0], tab_l[1], tab_s[0], tab_s[1], pcs)


def dft_tables(n):
    k = jnp.arange(n, dtype=jnp.int32)
    kn = (k[:, None] * k[None, :]) % n
    ang = kn.astype(F32) * (2.0 * math.pi / n)
    scale = 1.0 / math.sqrt(n)
    return jnp.cos(ang) * scale, jnp.sin(ang) * scale


def _pool_kernel(v_ref, w_ref, sc_ref, o_ref, *, windows, l_len):
    dh = w_ref.shape[1]
    for r0, r1 in ((0, l_len), (l_len, v_ref.shape[0])):
        t_len = r1 - r0
        t = lax.broadcasted_iota(jnp.int32, (t_len, dh), 0)
        for g, win in enumerate(windows):
            v = v_ref[r0:r1, g * dh:(g + 1) * dh].astype(F32)
            lo = win // 2
            acc = jnp.zeros_like(v)
            for j in range(-lo, win - lo):
                src = t + j
                shifted = v if j == 0 else pltpu.roll(v, (-j) % t_len, 0)
                acc = acc + jnp.where((src >= 0) & (src < t_len), shifted, 0.0)
            cnt = jnp.clip(t + (win - lo), 0, t_len) - jnp.clip(t - lo, 0, t_len)
            pooled = acc / cnt.astype(F32) - v
            y = jnp.dot(pooled.astype(BF16), w_ref[g].astype(BF16), preferred_element_type=F32)
            o_ref[r0:r1, g * dh:(g + 1) * dh] = (y * sc_ref[:, g * dh:(g + 1) * dh]).astype(o_ref.dtype)


def pool_mix(st, p, pool_w, pool_scale):
    ng, dh, _ = pool_w.shape
    width = ng * dh
    return pl.pallas_call(
        functools.partial(_pool_kernel, windows=POOL_WINDOWS, l_len=st.l),
        grid=(st.b,),
        in_specs=[pl.BlockSpec((st.t, width), lambda b: (b, 0)),
                  pl.BlockSpec((ng, dh, dh), lambda b: (0, 0, 0)),
                  pl.BlockSpec((1, width), lambda b: (0, 0))],
        out_specs=pl.BlockSpec((st.t, width), lambda b: (b, 0)),
        out_shape=jax.ShapeDtypeStruct((st.nt, width), BF16),
        compiler_params=_cparams("arbitrary"),
        name="pool_mix",
    )(p, pool_w, pool_scale.reshape(1, width))


def _qk_prep_kernel(*refs, q_blocks, q_scale):
    x_refs = refs[:q_blocks + 1]
    g_ref, cos_ref, sin_ref, o_ref = refs[q_blocks + 1:]
    cw = x_refs[0].shape[1]
    cos = cos_ref[...]
    sin = sin_ref[...]
    lane = lax.broadcasted_iota(jnp.int32, (o_ref.shape[0], HEAD_DIM), 1)
    first = (lane % (HEAD_DIM // 2)) < (HEAD_DIM // 4)
    for j, x_ref in enumerate(x_refs):
        is_q = j < q_blocks
        gain = g_ref[0 if is_q else 1]
        if is_q:
            gain = gain * q_scale
        for h in range(cw // HEAD_DIM):
            x = x_ref[:, h * HEAD_DIM:(h + 1) * HEAD_DIM].astype(F32)
            n = x * lax.rsqrt(jnp.mean(x * x, axis=-1, keepdims=True) + EPS) * gain
            partner = jnp.where(first, pltpu.roll(n, HEAD_DIM - HEAD_DIM // 4, 1), pltpu.roll(n, HEAD_DIM // 4, 1))
            c0 = j * cw + h * HEAD_DIM
            o_ref[:, c0:c0 + HEAD_DIM] = (n * cos + partner * sin).astype(o_ref.dtype)


def qk_prep(st, p, gains, cos_t, sin_t, q_blocks, q_scale):
    nt = p.shape[0]
    cw = KV_HEADS * HEAD_DIM
    tm = st.tm
    tpb, lt, ns = st.t // tm, st.l // tm, st.s // tm
    rope_blk = lambda i: jnp.where(i % tpb < lt, ns, i % tpb - lt)
    x_specs = [pl.BlockSpec((tm, cw), lambda i, j=j: (i, j + 1)) for j in range(q_blocks + 1)]
    return pl.pallas_call(
        functools.partial(_qk_prep_kernel, q_blocks=q_blocks, q_scale=q_scale),
        grid=(nt // tm,),
        in_specs=x_specs + [pl.BlockSpec((2, 1, HEAD_DIM), lambda i: (0, 0, 0)),
                            pl.BlockSpec((tm, HEAD_DIM), lambda i: (rope_blk(i), 0)),
                            pl.BlockSpec((tm, HEAD_DIM), lambda i: (rope_blk(i), 0))],
        out_specs=pl.BlockSpec((tm, (q_blocks + 1) * cw), lambda i: (i, 0)),
        out_shape=jax.ShapeDtypeStruct((nt, (q_blocks + 1) * cw), BF16),
        compiler_params=_cparams("arbitrary"),
        name="qk_prep",
    )(*([p] * (q_blocks + 1)), gains, cos_t, sin_t)


def rope_tables(s, tm):
    rows = s // GRID_W
    row = jnp.repeat(jnp.arange(rows), GRID_W)
    col = jnp.tile(jnp.arange(GRID_W), rows)
    nf = HEAD_DIM // 4
    inv = jnp.power(ROPE_THETA, -jnp.arange(nf, dtype=F32) / nf)
    a_row = row.astype(F32)[:, None] * inv
    a_col = col.astype(F32)[:, None] * inv
    ang = jnp.concatenate([a_row, a_row, a_col, a_col], axis=1)
    sign = jnp.tile(jnp.concatenate([-jnp.ones((nf,), F32), jnp.ones((nf,), F32)]), 2)
    cos_t = jnp.concatenate([jnp.cos(ang), jnp.ones((tm, HEAD_DIM), F32)], axis=0)
    sin_t = jnp.concatenate([jnp.sin(ang) * sign, jnp.zeros((tm, HEAD_DIM), F32)], axis=0)
    return cos_t, sin_t


def _attn_kernel(q_ref, k_ref, v_ref, o_ref, *, q_per_kv, l_len):
    tq = q_ref.shape[0]
    dn = (((1,), (1,)), ((), ()))

    def run(n_keys):
        k = k_ref[0:n_keys, :]
        v = v_ref[0:n_keys, :]
        for g in range(q_per_kv):
            q = q_ref[:, g * HEAD_DIM:(g + 1) * HEAD_DIM]
            s_ = lax.dot_general(q, k, dn, preferred_element_type=F32)
            e = jnp.exp2(s_ - s_.max(axis=-1, keepdims=True))
            acc = jnp.dot(e.astype(BF16), v, preferred_element_type=F32)
            o_ref[:, g * HEAD_DIM:(g + 1) * HEAD_DIM] = (acc / e.sum(axis=-1, keepdims=True)).astype(o_ref.dtype)

    in_ctx = pl.program_id(2) * tq < l_len

    @pl.when(in_ctx)
    def _():
        run(l_len)

    @pl.when(jnp.logical_not(in_ctx))
    def _():
        run(k_ref.shape[0])


def attention(st, qk, p, v_col0, n_q_heads):
    q_per_kv = n_q_heads // KV_HEADS
    qw = q_per_kv * HEAD_DIM
    tq = st.tm
    tpb = st.t // tq
    k_col0 = n_q_heads
    return pl.pallas_call(
        functools.partial(_attn_kernel, q_per_kv=q_per_kv, l_len=st.l),
        grid=(st.b, KV_HEADS, tpb),
        in_specs=[pl.BlockSpec((tq, qw), lambda b, h, i: (b * tpb + i, h)),
                  pl.BlockSpec((st.t, HEAD_DIM), lambda b, h, i: (b, k_col0 + h)),
                  pl.BlockSpec((st.t, HEAD_DIM), lambda b, h, i: (b, v_col0 + h))],
        out_specs=pl.BlockSpec((tq, qw), lambda b, h, i: (b * tpb + i, h)),
        out_shape=jax.ShapeDtypeStruct((st.nt, n_q_heads * HEAD_DIM), BF16),
        compiler_params=_cparams("arbitrary", "arbitrary", "arbitrary"),
        name="attention",
    )(qk, qk, p)


def _pack_pairs(lo, hi):
    lo_b = lax.bitcast_convert_type(lo.astype(BF16).astype(F32), jnp.uint32)
    hi_b = lax.bitcast_convert_type(hi.astype(BF16).astype(F32), jnp.uint32)
    return (lo_b >> 16) | (hi_b & jnp.uint32(0xFFFF0000))


def _unpack_pairs(word):
    lo = lax.bitcast_convert_type(word << 16, F32)
    hi = lax.bitcast_convert_type(word & jnp.uint32(0xFFFF0000), F32)
    return lo, hi


ROW_TILE = 8


def _store_row_tiles(ref, word):
    rows = word.shape[0]
    for j in range(ROW_TILE):
        ref[pl.ds(j, rows, stride=ROW_TILE), :] = word[:, j * LANES:(j + 1) * LANES]


def _load_row_tiles(ref, rows):
    return [ref[pl.ds(j, rows, stride=ROW_TILE), :] for j in range(ROW_TILE)]


def _router_kernel(x_ref, g_ref, sh_ref, sc_ref, wr_ref, br_ref, h_ref, idx_ref, wgt_ref):
    x = x_ref[...]
    y = x * lax.rsqrt(jnp.mean(x * x, axis=-1, keepdims=True) + EPS) * g_ref[...]
    h = y * (1.0 + sc_ref[...]) + sh_ref[...]
    half = h.shape[1] // 2
    _store_row_tiles(h_ref, _pack_pairs(h[:, 0:half], h[:, half:2 * half]))
    ne = wr_ref.shape[0]
    dn = (((1,), (1,)), ((), ()))
    h1 = h.astype(BF16)
    h2 = (h - h1.astype(F32)).astype(BF16)
    w = wr_ref[...]
    w1 = w.astype(BF16)
    w2 = (w - w1.astype(F32)).astype(BF16)
    r1 = lax.dot_general(jnp.concatenate([w1, w2], axis=0), h1, dn, preferred_element_type=F32)
    r2 = lax.dot_general(w1, h2, dn, preferred_element_type=F32)
    logits = r1[0:ne] + r1[ne:2 * ne] + r2 + br_ref[...]
    ne, tm = logits.shape
    eidx = lax.broadcasted_iota(jnp.int32, (ne, tm), 0)
    vals, idxs = [], []
    cur = logits
    for _ in range(TOP_K):
        m = cur.max(axis=0, keepdims=True)
        sel = jnp.min(jnp.where(cur == m, eidx, ne), axis=0, keepdims=True)
        vals.append(m)
        idxs.append(sel)
        cur = jnp.where(eidx == sel, -jnp.inf, cur)
    es = [jnp.exp(v - vals[0]) for v in vals]
    den = es[0]
    for e in es[1:]:
        den = den + e
    zi = jnp.zeros((8 - TOP_K, tm), jnp.int32)
    zf = jnp.zeros((8 - TOP_K, tm), F32)
    idx_ref[...] = jnp.concatenate(idxs + [zi], axis=0)
    wgt_ref[...] = jnp.concatenate([e / den for e in es] + [zf], axis=0)


def norm_router(st, x, g, mod, k_shift, k_scale, w_router, b_router):
    nt, d = x.shape
    ne = w_router.shape[1]
    tm = _pick(st.tm, (256, 128))
    return pl.pallas_call(
        _router_kernel,
        grid=(nt // tm,),
        in_specs=[pl.BlockSpec((tm, d), lambda i: (i, 0)),
                  pl.BlockSpec((1, d), lambda i: (0, 0)),
                  _mod_spec(st, tm, k_shift, d, 1, 0),
                  _mod_spec(st, tm, k_scale, d, 1, 0),
                  pl.BlockSpec((ne, d), lambda i: (0, 0)),
                  pl.BlockSpec((ne, 1), lambda i: (0, 0))],
        out_specs=[pl.BlockSpec((tm * ROW_TILE, LANES), lambda i: (i, 0)),
                   pl.BlockSpec((8, tm), lambda i: (0, i)),
                   pl.BlockSpec((8, tm), lambda i: (0, i))],
        out_shape=[jax.ShapeDtypeStruct((nt * ROW_TILE, LANES), jnp.uint32),
                   jax.ShapeDtypeStruct((8, nt), jnp.int32),
                   jax.ShapeDtypeStruct((8, nt), F32)],
        compiler_params=_cparams("arbitrary"),
        name="norm_router",
    )(x, g.reshape(1, d), mod, mod, w_router.T, b_router.reshape(ne, 1))


def _route_plan_kernel(idx_ref, dest_ref, blk_ref, ends_ref, pre_ref, *, ne, tb, chunk):
    nt = idx_ref.shape[1]
    n_chunk = nt // chunk
    eidx = lax.broadcasted_iota(jnp.int32, (ne, chunk), 0)
    tri = (lax.broadcasted_iota(jnp.int32, (chunk, chunk), 0)
           < lax.broadcasted_iota(jnp.int32, (chunk, chunk), 1)).astype(BF16)

    def count(c, carry):
        c0 = pl.multiple_of(c * chunk, chunk)
        oh = jnp.zeros((ne, chunk), F32)
        for k in range(TOP_K):
            oh = oh + (eidx == idx_ref[k:k + 1, pl.ds(c0, chunk)]).astype(F32)
        pre_ref[:, pl.ds(c0, chunk)] = jnp.dot(oh.astype(BF16), tri, preferred_element_type=F32) + carry
        return carry + oh.sum(axis=1, keepdims=True)

    counts = lax.fori_loop(0, n_chunk, count, jnp.zeros((ne, 1), F32))
    padded = jnp.ceil(counts * (1.0 / tb)) * tb
    r_i = lax.broadcasted_iota(jnp.int32, (ne, ne), 0)
    c_i = lax.broadcasted_iota(jnp.int32, (ne, ne), 1)
    padded_row = jnp.sum(jnp.where(r_i == c_i, padded, 0.0), axis=0, keepdims=True)
    pstart = jnp.sum(jnp.where(c_i < r_i, padded_row, 0.0), axis=1, keepdims=True)
    pend = pstart + padded

    def place(c, carry):
        c0 = pl.multiple_of(c * chunk, chunk)
        base = pre_ref[:, pl.ds(c0, chunk)] + pstart
        rows = [jnp.sum(jnp.where(eidx == idx_ref[k:k + 1, pl.ds(c0, chunk)], base, 0.0), axis=0, keepdims=True)
                for k in range(TOP_K)]
        rows.append(jnp.zeros((8 - TOP_K, chunk), F32))
        dest_ref[:, pl.ds(c0, chunk)] = jnp.concatenate(rows, axis=0).astype(jnp.int32)
        return carry

    lax.fori_loop(0, n_chunk, place, 0)

    nbp = blk_ref.shape[1]
    starts = (lax.broadcasted_iota(jnp.int32, (ne, nbp), 1) * tb).astype(F32)
    blk_e = jnp.minimum(jnp.sum((pend <= starts).astype(F32), axis=0, keepdims=True), ne - 1.0)
    n_live = jnp.sum(padded, axis=0, keepdims=True) * (1.0 / tb)
    e_col = lax.broadcasted_iota(jnp.int32, (ne, nbp), 0).astype(F32)
    owns = jnp.where(padded > 0.0, 1.0, 0.0)
    later = jnp.where(e_col > blk_e, owns, 0.0)
    nxt = jnp.min(jnp.where(later > 0.0, e_col, float(ne)), axis=0, keepdims=True)
    nxt = jnp.where(nxt >= float(ne), -1.0, nxt)
    rank = jnp.sum(jnp.where(e_col < blk_e, owns, 0.0), axis=0, keepdims=True)
    slot = rank - 2.0 * jnp.floor(rank * 0.5)
    blk_ref[...] = jnp.concatenate([blk_e, jnp.broadcast_to(n_live, (1, nbp)), nxt, slot, jnp.zeros((4, nbp), F32)],
                                   axis=0).astype(jnp.int32)

    r_l = lax.broadcasted_iota(jnp.int32, (ne, LANES), 0)
    c_l = lax.broadcasted_iota(jnp.int32, (ne, LANES), 1)
    to_row = lambda col: jnp.sum(jnp.where(r_l == c_l, col, 0.0), axis=0, keepdims=True)
    ends_ref[...] = jnp.concatenate([to_row(pstart + counts), to_row(pend), jnp.zeros((6, LANES), F32)],
                                    axis=0).astype(jnp.int32)


def route_plan(idx8, ne, tb, n_blocks):
    nt = idx8.shape[1]
    chunk = _pick(nt, (256, 128))
    nbp = -(-n_blocks // LANES) * LANES
    return pl.pallas_call(
        functools.partial(_route_plan_kernel, ne=ne, tb=tb, chunk=chunk),
        out_shape=[jax.ShapeDtypeStruct((8, nt), jnp.int32), jax.ShapeDtypeStruct((8, nbp), jnp.int32),
                   jax.ShapeDtypeStruct((8, LANES), jnp.int32)],
        scratch_shapes=[pltpu.VMEM((ne, nt), F32)],
        compiler_params=pltpu.CompilerParams(vmem_limit_bytes=VMEM_LIMIT),
        name="route_plan",
    )(idx8)


def _dispatch_kernel(dest_ref, ends_ref, h_ref, xs_hbm, zrow, sem, *, rows, ne):
    i = pl.program_id(0)
    base = i * rows

    @pl.when(i == 0)
    def _():
        zrow[...] = jnp.zeros_like(zrow)

        def per_expert(e, carry, waiting):
            lo, hi = ends_ref[0, e], ends_ref[1, e]

            def put(r, c):
                dst = xs_hbm.at[pl.ds(pl.multiple_of(r * ROW_TILE, ROW_TILE), ROW_TILE)]
                pltpu.make_async_copy(zrow.at[pl.ds(0, ROW_TILE)], dst, sem.at[1]).start()
                return c

            def done(r, c):
                pltpu.make_async_copy(zrow.at[pl.ds(0, ROW_TILE)], xs_hbm.at[pl.ds(0, ROW_TILE)], sem.at[1]).wait()
                return c

            lax.fori_loop(lo, hi, done if waiting else put, 0)
            return carry

        for waiting in (False, True):
            lax.fori_loop(0, ne, functools.partial(per_expert, waiting=waiting), 0)

        blk = zrow.shape[0]
        tb = blk // ROW_TILE
        first_dead = ends_ref[1, ne - 1] // tb
        n_blk = xs_hbm.shape[0] // blk

        def put_blk(j, c):
            pltpu.make_async_copy(zrow, xs_hbm.at[pl.ds(pl.multiple_of(j * blk, blk), blk)], sem.at[1]).start()
            return c

        def done_blk(j, c):
            pltpu.make_async_copy(zrow, xs_hbm.at[pl.ds(0, blk)], sem.at[1]).wait()
            return c

        lax.fori_loop(first_dead, n_blk, put_blk, 0)
        lax.fori_loop(first_dead, n_blk, done_blk, 0)

    def issue(r, carry):
        src = h_ref.at[pl.ds(pl.multiple_of(r * ROW_TILE, ROW_TILE), ROW_TILE)]
        for k in range(TOP_K):
            d = dest_ref[(base + r) * TOP_K + k]
            dst = xs_hbm.at[pl.ds(pl.multiple_of(d * ROW_TILE, ROW_TILE), ROW_TILE)]
            pltpu.make_async_copy(src, dst, sem.at[0]).start(priority=k % 2)
        return carry

    lax.fori_loop(0, rows, issue, 0, unroll=4)
    for _ in range(TOP_K):
        pltpu.make_async_copy(h_ref, xs_hbm.at[pl.ds(0, rows * ROW_TILE)], sem.at[0]).wait()


def moe_dispatch(h_tiles, dest1, ends, cap, ne):
    nt = h_tiles.shape[0] // ROW_TILE
    rows = _pick(nt, (256, 128, 64, 32, 16, 8))
    return pl.pallas_call(
        functools.partial(_dispatch_kernel, rows=rows, ne=ne),
        grid=(nt // rows,),
        in_specs=[pl.BlockSpec(memory_space=pltpu.SMEM),
                  pl.BlockSpec(memory_space=pltpu.SMEM),
                  pl.BlockSpec((rows * ROW_TILE, LANES), lambda i: (i, 0))],
        out_specs=pl.BlockSpec(memory_space=pl.ANY),
        out_shape=jax.ShapeDtypeStruct((cap * ROW_TILE, LANES), h_tiles.dtype),
        scratch_shapes=[pltpu.VMEM((MOE_ROWS * ROW_TILE, LANES), h_tiles.dtype), pltpu.SemaphoreType.DMA((2,))],
        compiler_params=_cparams("arbitrary"),
        name="moe_dispatch",
    )(dest1, ends, h_tiles)


def _expert_kernel(be_ref, nb_ref, nxt_ref, slot_ref, x_ref, bg_ref, bu_ref, bd_ref, wg_hbm, wu_hbm, wd_hbm, o_ref,
                   wg_st, wu_st, wd_st, wg_bf, wu_bf, wd_bf, sem, *, layer):
    i = pl.program_id(0)
    e = be_ref[i]
    live = i < nb_ref[0]
    first = (i == 0) | (e != be_ref[jnp.maximum(i - 1, 0)])
    slot = slot_ref[i]
    half = wg_bf.shape[0] // 2

    def weight_copies(ex, s):
        return (pltpu.make_async_copy(wg_hbm.at[layer, ex], wg_st.at[s], sem.at[s, 0]),
                pltpu.make_async_copy(wu_hbm.at[layer, ex], wu_st.at[s], sem.at[s, 1]),
                pltpu.make_async_copy(wd_hbm.at[layer, ex], wd_st.at[s], sem.at[s, 2]))

    @pl.when(i == 0)
    def _():
        for cp in weight_copies(e, slot):
            cp.start(priority=1)

    @pl.when(live & first)
    def _():
        for cp in weight_copies(e, slot):
            cp.wait()

        @pl.when(nxt_ref[i] >= 0)
        def _():
            for cp in weight_copies(nxt_ref[i], 1 - slot):
                cp.start(priority=1)

        wg_bf[...] = wg_st[slot].astype(BF16)
        wu_bf[...] = wu_st[slot].astype(BF16)
        wd_bf[...] = wd_st[slot].astype(BF16)

    @pl.when(live)
    def _():
        tb = x_ref.shape[0] // ROW_TILE
        lo, hi = _unpack_pairs(jnp.concatenate(_load_row_tiles(x_ref, tb), axis=1))
        lo = lo.astype(BF16)
        hi = hi.astype(BF16)

        def proj(w_bf, b_ref):
            return (jnp.dot(lo, w_bf[0:half, :], preferred_element_type=F32)
                    + jnp.dot(hi, w_bf[half:2 * half, :], preferred_element_type=F32) + b_ref[...])

        g = jnp.minimum(proj(wg_bf, bg_ref), SWIGLU_LIMIT)
        u = jnp.clip(proj(wu_bf, bu_ref), -SWIGLU_LIMIT, SWIGLU_LIMIT)
        a = g * jax.nn.sigmoid(SWIGLU_ALPHA * g) * (u + 1.0)
        y = jnp.dot(a.astype(BF16), wd_bf[...], preferred_element_type=F32) + bd_ref[...]
        _store_row_tiles(o_ref, _pack_pairs(y[:, 0:half], y[:, half:2 * half]))

    @pl.when(jnp.logical_not(live))
    def _():
        o_ref[...] = jnp.zeros_like(o_ref)


def expert_ffn(x_sorted, blk, n_blocks, layer, w_gate, b_gate, w_up, b_up, w_down, b_down):
    depth, ne, d, f = w_gate.shape
    blk_rows = MOE_ROWS * ROW_TILE
    bias = lambda n: pl.BlockSpec((None, 1, n), lambda i, be, nb, nx, sl: (layer * ne + be[i], 0, 0))
    hbm = pl.BlockSpec(memory_space=pl.ANY)
    grid_spec = pltpu.PrefetchScalarGridSpec(
        num_scalar_prefetch=4,
        grid=(n_blocks,),
        in_specs=[pl.BlockSpec((blk_rows, LANES), lambda i, be, nb, nx, sl: (jnp.minimum(i, nb[0] - 1), 0)),
                  bias(f), bias(f), bias(d), hbm, hbm, hbm],
        out_specs=pl.BlockSpec((blk_rows, LANES), lambda i, be, nb, nx, sl: (i, 0)),
        scratch_shapes=[pltpu.VMEM((2, d, f), F32), pltpu.VMEM((2, d, f), F32), pltpu.VMEM((2, f, d), F32),
                        pltpu.VMEM((d, f), BF16), pltpu.VMEM((d, f), BF16), pltpu.VMEM((f, d), BF16),
                        pltpu.SemaphoreType.DMA((2, 3))],
    )
    return pl.pallas_call(
        functools.partial(_expert_kernel, layer=layer),
        grid_spec=grid_spec,
        out_shape=jax.ShapeDtypeStruct(x_sorted.shape, jnp.uint32),
        compiler_params=_cparams("arbitrary"),
        name="expert_ffn",
    )(blk[0, :n_blocks], blk[1, :1], blk[2, :n_blocks], blk[3, :n_blocks], x_sorted,
      b_gate.reshape(depth * ne, 1, f), b_up.reshape(depth * ne, 1, f), b_down.reshape(depth * ne, 1, d),
      w_gate, w_up, w_down)


def _combine_kernel(dest_ref, x_ref, w_ref, gate_ref, y_hbm, o_ref, buf, sem, *, tm):
    i = pl.program_id(0)
    n = pl.num_programs(0)
    slot = i % 2
    half = ROW_TILE * LANES

    def fetch(tile, s):
        def issue(r, carry):
            for k in range(TOP_K):
                d = dest_ref[(tile * tm + r) * TOP_K + k]
                src = y_hbm.at[pl.ds(pl.multiple_of(d * ROW_TILE, ROW_TILE), ROW_TILE)]
                dst = buf.at[s, k, pl.ds(pl.multiple_of(r * ROW_TILE, ROW_TILE), ROW_TILE)]
                pltpu.make_async_copy(src, dst, sem.at[s]).start(priority=k % 2)
            return carry
        lax.fori_loop(0, tm, issue, 0, unroll=4)

    @pl.when(i == 0)
    def _():
        fetch(0, 0)

    @pl.when(i + 1 < n)
    def _():
        fetch(i + 1, 1 - slot)

    for k in range(TOP_K):
        pltpu.make_async_copy(y_hbm.at[pl.ds(0, tm * ROW_TILE)], buf.at[slot, k], sem.at[slot]).wait()

    ws = [w_ref[:, k:k + 1] for k in range(TOP_K)]
    for j in range(ROW_TILE):
        acc_lo = None
        acc_hi = None
        for k in range(TOP_K):
            lo, hi = _unpack_pairs(buf[slot, k, pl.ds(j, tm, stride=ROW_TILE), :])
            acc_lo = ws[k] * lo if acc_lo is None else acc_lo + ws[k] * lo
            acc_hi = ws[k] * hi if acc_hi is None else acc_hi + ws[k] * hi
        c_lo = slice(j * LANES, (j + 1) * LANES)
        c_hi = slice(half + j * LANES, half + (j + 1) * LANES)
        o_ref[:, c_lo] = x_ref[:, c_lo] + gate_ref[:, c_lo] * acc_lo
        o_ref[:, c_hi] = x_ref[:, c_hi] + gate_ref[:, c_hi] * acc_hi


def moe_combine(st, x, y_sorted, dest1, w_tok, mod, k_gate):
    nt, d = x.shape
    assert d // 2 == ROW_TILE * LANES
    tm = _pick(st.tm, (128, 64, 32, 16, 8))
    return pl.pallas_call(
        functools.partial(_combine_kernel, tm=tm),
        grid=(nt // tm,),
        in_specs=[pl.BlockSpec(memory_space=pltpu.SMEM),
                  pl.BlockSpec((tm, d), lambda i: (i, 0)),
                  pl.BlockSpec((tm, TOP_K), lambda i: (i, 0)),
                  _mod_spec(st, tm, k_gate, d, 1, 0),
                  pl.BlockSpec(memory_space=pl.ANY)],
        out_specs=pl.BlockSpec((tm, d), lambda i: (i, 0)),
        out_shape=jax.ShapeDtypeStruct((nt, d), F32),
        scratch_shapes=[pltpu.VMEM((2, TOP_K, tm * ROW_TILE, LANES), jnp.uint32), pltpu.SemaphoreType.DMA((2,))],
        input_output_aliases={1: 0},
        compiler_params=_cparams("arbitrary"),
        name="moe_combine",
    )(dest1, x, w_tok, mod, y_sorted)


def moe_layer(st, x, g, mod, layer, w_router, b_router, w_gate, b_gate, w_up, b_up, w_down, b_down):
    nt, d = x.shape
    ne = w_router.shape[1]
    tb = MOE_ROWS
    h_packed, idx8, wgt8 = norm_router(st, x, g, mod, 3, 4, w_router, b_router)
    n_assign = nt * TOP_K
    cap = -(-(n_assign + ne * (tb - 1)) // tb) * tb
    n_blocks = cap // tb
    dest8, blk, ends = route_plan(idx8, ne, tb, n_blocks)
    dest1 = dest8[:TOP_K].T.reshape(-1)
    x_sorted = moe_dispatch(h_packed, dest1, ends, cap, ne)
    y_sorted = expert_ffn(x_sorted, blk, n_blocks, layer, w_gate, b_gate, w_up, b_up, w_down, b_down)
    return moe_combine(st, x, y_sorted, dest1, wgt8[:TOP_K].T, mod, 5)


def even_mixer(st, x, norm_g, mod, i, w_in, w_out, s5p, d_skip, glu_w, glu_b, fnet_w, tables):
    d = x.shape[1]
    fw = fnet_w.shape[0] * fnet_w.shape[1]
    sw = d - fw
    p = norm_matmul(st, x, norm_g, mod, 0, 1, w_in, i)
    b_bd, c_bd, lam8 = s5p
    y_f, y_b = s5_scan(st, p, sw, b_bd, c_bd, lam8)
    a = gelu_glu(st, y_f, y_b, p, d_skip, glu_w, i, glu_b)
    cd, sd, tab_s, tab_l = tables
    pcs = fnet_chan(p, sw // fw, fnet_w, cd, sd)
    fm = fnet_seq(st, pcs, tab_l, tab_s)
    return matmul([a, fm], w_out, i, mode="resid", extra=(x, mod, 2), st=st)


def odd_mixer(st, x, norm_g, mod, i, w_in, w_out, pool_w, pool_scale, q_gain, k_gain, rope):
    d = x.shape[1]
    pw = pool_w.shape[0] * pool_w.shape[1]
    n_q = (d - pw) // HEAD_DIM
    p = norm_matmul(st, x, norm_g, mod, 0, 1, w_in, i)
    cos_t, sin_t = rope
    gains = jnp.stack([q_gain, k_gain]).reshape(2, 1, HEAD_DIM)
    qk = qk_prep(st, p, gains, cos_t, sin_t, n_q // KV_HEADS, HEAD_DIM ** -0.5 * math.log2(math.e))
    v_col0 = (pw + n_q * HEAD_DIM + KV_HEADS * HEAD_DIM) // HEAD_DIM
    att = attention(st, qk, p, v_col0, n_q)
    pm = pool_mix(st, p, pool_w, pool_scale)
    return matmul([pm, att], w_out, i, mode="resid", extra=(x, mod, 2), st=st)


def kernel(x, c, ctx, c_ctx, ada_w, ada_b, norm_mix_g, norm_ffn_g, ev_w_in, ev_w_out, s5_lam_re, s5_lam_im, s5_log_dt, s5_b_re, s5_b_im, s5_c_re, s5_c_im, s5_d, s5_glu_w, s5_glu_b, fnet_w, od_w_in, od_w_out, pool_w, pool_scale, q_gain, k_gain, router_w, router_b, exp_w_gate, exp_b_gate, exp_w_up, exp_b_up, exp_w_down, exp_b_down, final_g):
    batch, seq, d = x.shape
    ctx_len = ctx.shape[1]
    depth = ada_w.shape[0]
    st = Stream(batch, seq, ctx_len)
    xs = jnp.concatenate([ctx, x], axis=1).reshape(st.nt, d)

    cond8 = jnp.concatenate([c, c_ctx[None], jnp.zeros((8 - batch - 1, d), F32)], axis=0)
    mod_all = adaln_all(cond8, ada_w, ada_b)
    mod_all = mod_all.reshape(depth, 8, 6, 1, d).transpose(0, 2, 1, 3, 4)

    dh = fnet_w.shape[2]
    cd, sd = dft_tables(dh)
    tab_s = tuple(t.astype(BF16) for t in dft_tables(seq))
    tab_l = tuple(t.astype(BF16) for t in dft_tables(ctx_len))
    tables = (cd, sd, tab_s, tab_l)
    rope = rope_tables(seq, st.tm)

    for layer in range(depth):
        i = layer // 2
        mod = mod_all[layer]
        if layer % 2 == 0:
            s5p = s5_tables(s5_lam_re[i], s5_lam_im[i], s5_log_dt[i], s5_b_re[i], s5_b_im[i], s5_c_re[i], s5_c_im[i])
            xs = even_mixer(st, xs, norm_mix_g[layer], mod, i, ev_w_in, ev_w_out, s5p, s5_d[i], s5_glu_w,
                            s5_glu_b[i], fnet_w[i], tables)
        else:
            xs = odd_mixer(st, xs, norm_mix_g[layer], mod, i, od_w_in, od_w_out, pool_w[i], pool_scale[i],
                           q_gain[i], k_gain[i], rope)
        xs = moe_layer(st, xs, norm_ffn_g[layer], mod, layer, router_w[layer], router_b[layer],
                       exp_w_gate, exp_b_gate, exp_w_up, exp_b_up, exp_w_down, exp_b_down)
    return final_norm(st, xs, final_g).reshape(batch, seq, d)
```

```python
import functools
import math

import jax
import jax.numpy as jnp
from jax import lax
from jax.experimental import pallas as pl
from jax.experimental.pallas import tpu as pltpu

F32 = jnp.float32
BF16 = jnp.bfloat16
EPS = 1e-6

GRID_W = 64
FNET_HEADS = 4
S5_GROUP = 16
S5_STATE = 64
POOL_WINDOWS = (2, 4, 8, 16)
HEAD_DIM = 128
KV_HEADS = 4
ROPE_THETA = 10000.0
TOP_K = 4
SWIGLU_LIMIT = 7.0
SWIGLU_ALPHA = 1.702

LANES = 128
S5_BLOCK_GROUPS = LANES // S5_GROUP
VMEM_LIMIT = 48 * 1024 * 1024
MOE_ROWS = 256


def _cparams(*sem):
    return pltpu.CompilerParams(dimension_semantics=sem, vmem_limit_bytes=VMEM_LIMIT)


def _pick(n, prefs):
    for p in prefs:
        if n % p == 0:
            return p
    return n


def _adaln_kernel(c_ref, w_ref, b_ref, o_ref):
    c = c_ref[...]
    a = (c * jax.nn.sigmoid(c)).astype(BF16)
    o_ref[...] = jnp.dot(a, w_ref[...].astype(BF16), preferred_element_type=F32) + b_ref[...]


def adaln_all(cond8, ada_w, ada_b):
    depth, d, n = ada_w.shape
    tn = _pick(n, (1024, 512, 256, 128))
    return pl.pallas_call(
        _adaln_kernel,
        grid=(depth, n // tn),
        in_specs=[pl.BlockSpec((8, d), lambda l, j: (0, 0)),
                  pl.BlockSpec((None, d, tn), lambda l, j: (l, 0, j)),
                  pl.BlockSpec((None, 1, tn), lambda l, j: (l, 0, j))],
        out_specs=pl.BlockSpec((None, 8, tn), lambda l, j: (l, 0, j)),
        out_shape=jax.ShapeDtypeStruct((depth, 8, n), F32),
        compiler_params=_cparams("arbitrary", "arbitrary"),
        name="adaln",
    )(cond8, ada_w, ada_b.reshape(depth, 1, n))


def _norm_kernel(x_ref, g_ref, o_ref):
    x = x_ref[...]
    o_ref[...] = (x * lax.rsqrt(jnp.mean(x * x, axis=-1, keepdims=True) + EPS) * g_ref[...]).astype(o_ref.dtype)


class Stream:
    def __init__(self, batch, seq, ctx_len):
        self.b, self.s, self.l = batch, seq, ctx_len
        self.t = seq + ctx_len
        self.nt = batch * self.t
        self.tm = _pick(math.gcd(seq, ctx_len), (256, 128, 64, 32, 16))

    def mod_row(self, tm):
        tpb, l, b = self.t // tm, self.l, self.b
        return lambda i: jnp.where((i % tpb) * tm < l, b, i // tpb)


def _mod_spec(st, tm, k, d, grid_rank, row_axis):
    mr = st.mod_row(tm)
    if grid_rank == 1:
        return pl.BlockSpec((None, None, 1, d), lambda i: (k, mr(i), 0, 0))
    if row_axis == 1:
        return pl.BlockSpec((None, None, 1, d), lambda j, i: (k, mr(i), 0, j))
    raise ValueError


def final_norm(st, x, g):
    d = x.shape[1]
    tm = st.tm
    tpb, lt, ns = st.t // tm, st.l // tm, st.s // tm
    return pl.pallas_call(
        _norm_kernel,
        grid=(st.b, ns),
        in_specs=[pl.BlockSpec((tm, d), lambda b, i: (b * tpb + lt + i, 0)),
                  pl.BlockSpec((1, d), lambda b, i: (0, 0))],
        out_specs=pl.BlockSpec((tm, d), lambda b, i: (b * ns + i, 0)),
        out_shape=jax.ShapeDtypeStruct((st.b * st.s, d), F32),
        compiler_params=_cparams("arbitrary", "arbitrary"),
        name="final_norm",
    )(x, g.reshape(1, d))


def _mm_kernel(*refs, n_a, k_offs, mode):
    a_refs = refs[:n_a]
    w_ref = refs[n_a]
    rest = refs[n_a + 1:]
    wbf_ref = rest[-1]
    o_ref = rest[-2]

    @pl.when(pl.program_id(1) == 0)
    def _():
        wbf_ref[...] = w_ref[...].astype(BF16)

    acc = None
    for a_ref, (k0, k1) in zip(a_refs, k_offs):
        part = jnp.dot(a_ref[...], wbf_ref[k0:k1, :], preferred_element_type=F32)
        acc = part if acc is None else acc + part
    if mode == "plain":
        o_ref[...] = acc.astype(o_ref.dtype)
    elif mode == "resid":
        x_ref, gate_ref = rest[0], rest[1]
        o_ref[...] = x_ref[...] + gate_ref[...] * acc
    elif mode == "glu":
        g_ref, b_ref = rest[0], rest[1]
        o_ref[...] = (g_ref[...].astype(F32) * jax.nn.sigmoid(acc + b_ref[...])).astype(o_ref.dtype)
    else:
        raise ValueError(mode)


def _norm_mm_kernel(x_ref, g_ref, sh_ref, sc_ref, w_ref, o_ref, wbf_ref):
    @pl.when(pl.program_id(1) == 0)
    def _():
        wbf_ref[...] = w_ref[...].astype(BF16)

    x = x_ref[...]
    y = x * lax.rsqrt(jnp.mean(x * x, axis=-1, keepdims=True) + EPS) * g_ref[...]
    h = (y * (1.0 + sc_ref[...]) + sh_ref[...]).astype(BF16)
    o_ref[...] = jnp.dot(h, wbf_ref[...], preferred_element_type=F32).astype(o_ref.dtype)


def norm_matmul(st, x, g, mod, k_shift, k_scale, w_stack, w_idx, tn_prefs=(2048, 1536, 1024, 512, 256, 128)):
    m, k = x.shape
    n = w_stack.shape[2]
    tm = st.tm
    tn = _pick(n, tn_prefs)
    mr = st.mod_row(tm)
    mod_spec = lambda kk: pl.BlockSpec((None, None, 1, k), lambda j, i: (kk, mr(i), 0, 0))
    return pl.pallas_call(
        _norm_mm_kernel,
        grid=(n // tn, m // tm),
        in_specs=[pl.BlockSpec((tm, k), lambda j, i: (i, 0)),
                  pl.BlockSpec((1, k), lambda j, i: (0, 0)),
                  mod_spec(k_shift), mod_spec(k_scale),
                  pl.BlockSpec((None, k, tn), lambda j, i: (w_idx, 0, j), pipeline_mode=pl.Buffered(1))],
        out_specs=pl.BlockSpec((tm, tn), lambda j, i: (i, j)),
        out_shape=jax.ShapeDtypeStruct((m, n), BF16),
        scratch_shapes=[pltpu.VMEM((k, tn), BF16)],
        compiler_params=_cparams("arbitrary", "arbitrary"),
        name="norm_mm",
    )(x, g.reshape(1, k), mod, mod, w_stack)


def _gelu_glu_kernel(yf_ref, yb_ref, u_ref, d_ref, w_ref, b_ref, o_ref, wbf_ref):
    @pl.when(pl.program_id(0) == 0)
    def _():
        wbf_ref[...] = w_ref[...].astype(BF16)

    y = yf_ref[...].astype(F32) + yb_ref[...].astype(F32) + d_ref[...] * u_ref[...].astype(F32)
    g = jax.nn.gelu(y).astype(BF16)
    acc = jnp.dot(g, wbf_ref[...], preferred_element_type=F32)
    o_ref[...] = (g.astype(F32) * jax.nn.sigmoid(acc + b_ref[...])).astype(o_ref.dtype)


def gelu_glu(st, y_f, y_b, p, d_skip, w_stack, w_idx, bias):
    m, width = y_f.shape
    tm = st.tm
    row = pl.BlockSpec((tm, width), lambda i: (i, 0))
    vec = pl.BlockSpec((1, width), lambda i: (0, 0))
    return pl.pallas_call(
        _gelu_glu_kernel,
        grid=(m // tm,),
        in_specs=[row, row, row, vec,
                  pl.BlockSpec((None, width, width), lambda i: (w_idx, 0, 0), pipeline_mode=pl.Buffered(1)),
                  vec],
        out_specs=row,
        out_shape=jax.ShapeDtypeStruct((m, width), BF16),
        scratch_shapes=[pltpu.VMEM((width, width), BF16)],
        compiler_params=_cparams("arbitrary"),
        name="gelu_glu",
    )(y_f, y_b, p, d_skip.reshape(1, width), w_stack, bias.reshape(1, width))


def matmul(a_parts, w_stack, w_idx, mode="plain", out_dtype=BF16, extra=(), st=None,
           tn_prefs=(2048, 1536, 1024, 512, 256, 128)):
    m = a_parts[0].shape[0]
    _, k, n = w_stack.shape
    tm = _pick(m if st is None else st.tm, (512, 256, 128, 64, 32, 16, 8))
    tn = _pick(n, tn_prefs)
    k_offs, off = [], 0
    in_specs, args = [], []
    for a in a_parts:
        ka = a.shape[1]
        k_offs.append((off, off + ka))
        off += ka
        in_specs.append(pl.BlockSpec((tm, ka), lambda j, i: (i, 0)))
        args.append(a)
    assert off == k
    in_specs.append(pl.BlockSpec((None, k, tn), lambda j, i: (w_idx, 0, j), pipeline_mode=pl.Buffered(1)))
    args.append(w_stack)
    io_alias = {}
    if mode == "resid":
        x, mod, k_gate = extra
        in_specs.append(pl.BlockSpec((tm, tn), lambda j, i: (i, j)))
        args.append(x)
        in_specs.append(_mod_spec(st, tm, k_gate, tn, 2, 1))
        args.append(mod)
        io_alias = {len(args) - 2: 0}
        out_dtype = F32
    elif mode == "glu":
        g, bias = extra
        in_specs.append(pl.BlockSpec((tm, tn), lambda j, i: (i, j)))
        args.append(g)
        in_specs.append(pl.BlockSpec((1, tn), lambda j, i: (0, j)))
        args.append(bias.reshape(1, n))
    return pl.pallas_call(
        functools.partial(_mm_kernel, n_a=len(a_parts), k_offs=tuple(k_offs), mode=mode),
        grid=(n // tn, m // tm),
        in_specs=in_specs,
        out_specs=pl.BlockSpec((tm, tn), lambda j, i: (i, j)),
        out_shape=jax.ShapeDtypeStruct((m, n), out_dtype),
        scratch_shapes=[pltpu.VMEM((k, tn), BF16)],
        input_output_aliases=io_alias,
        compiler_params=_cparams("arbitrary", "arbitrary"),
        name="mm_" + mode,
    )(*args)


def _s5_kernel(uf_ref, ub_ref, b_ref, c_ref, lam_ref, yf_ref, yb_ref, h_ref, bu_ref, u_scr, y_scr, *, tc, jb):
    half = b_ref.shape[2] // 2
    nbat = uf_ref.shape[0]

    @pl.when(pl.program_id(1) == 0)
    def _():
        h_ref[...] = jnp.zeros_like(h_ref)

    rows = tc * 8
    is_fwd = (lax.broadcasted_iota(jnp.int32, (rows, LANES), 0) % 8) < 4
    flip = (lax.broadcasted_iota(jnp.int32, (tc, tc), 0)
            + lax.broadcasted_iota(jnp.int32, (tc, tc), 1) == tc - 1).astype(BF16)
    for q in range(jb):
        lanes = slice(q * LANES, (q + 1) * LANES)
        for b in range(nbat):
            u_scr[q, pl.ds(b, tc, stride=8), :] = uf_ref[b, :, lanes].astype(F32)
            u_scr[q, pl.ds(nbat + b, tc, stride=8), :] = jnp.dot(flip, ub_ref[b, :, lanes],
                                                                 preferred_element_type=F32)
        u = u_scr[q]
        lhs = jnp.concatenate([jnp.where(is_fwd, u, 0.0), jnp.where(is_fwd, 0.0, u)], axis=1).astype(BF16)
        bu_ref[q] = jnp.dot(lhs, b_ref[q], preferred_element_type=F32)

    for q in range(jb):
        lam_r = lam_ref[q, 0]
        lam_i = lam_ref[q, 1]

        def step(t, carry, q=q, lam_r=lam_r, lam_i=lam_i):
            hr, hi = carry
            r0 = pl.multiple_of(t * 8, 8)
            nhr = lam_r * hr - lam_i * hi + bu_ref[q, pl.ds(r0, 8), 0:half]
            nhi = lam_r * hi + lam_i * hr + bu_ref[q, pl.ds(r0, 8), half:2 * half]
            bu_ref[q, pl.ds(r0, 8), 0:half] = nhr
            bu_ref[q, pl.ds(r0, 8), half:2 * half] = nhi
            return nhr, nhi

        hr, hi = lax.fori_loop(0, tc, step, (h_ref[q, :, 0:half], h_ref[q, :, half:2 * half]), unroll=True)
        h_ref[q, :, 0:half] = hr
        h_ref[q, :, half:2 * half] = hi

    for q in range(jb):
        lanes = slice(q * LANES, (q + 1) * LANES)
        y2 = jnp.dot(bu_ref[q].astype(BF16), c_ref[q], preferred_element_type=F32)
        y_scr[q] = jnp.where(is_fwd, y2[:, 0:LANES], y2[:, LANES:2 * LANES])
        for b in range(nbat):
            yf_ref[b, :, lanes] = y_scr[q, pl.ds(b, tc, stride=8), :].astype(yf_ref.dtype)
            yb = y_scr[q, pl.ds(nbat + b, tc, stride=8), :].astype(BF16)
            yb_ref[b, :, lanes] = jnp.dot(flip, yb, preferred_element_type=F32).astype(yb_ref.dtype)


def s5_scan(st, p, width, b_bd, c_bd, lam8):
    bsz, t_len, l_len = st.b, st.t, st.l
    assert 2 * bsz == 8
    d = p.shape[1]
    nb = width // LANES
    jb = _pick(nb, (2, 1))
    tc = _pick(math.gcd(st.s, l_len), (128, 64, 32, 16, 8))
    rows = tc * 8
    ns = b_bd.shape[2]
    n_l, n_c = l_len // tc, t_len // tc

    def bwd_chunk(c):
        return jnp.where(c < n_l, n_l - 1 - c, n_c + n_l - 1 - c)

    p3 = p.reshape(bsz, t_len, d)
    out = jax.ShapeDtypeStruct((bsz, t_len, width), BF16)
    y_f, y_b = pl.pallas_call(
        functools.partial(_s5_kernel, tc=tc, jb=jb),
        grid=(nb // jb, n_c),
        in_specs=[pl.BlockSpec((bsz, tc, jb * LANES), lambda j, c: (0, c, j)),
                  pl.BlockSpec((bsz, tc, jb * LANES), lambda j, c: (0, bwd_chunk(c), j)),
                  pl.BlockSpec((jb, 2 * LANES, ns), lambda j, c: (j, 0, 0)),
                  pl.BlockSpec((jb, ns, 2 * LANES), lambda j, c: (j, 0, 0)),
                  pl.BlockSpec((jb, 2, 8, ns // 2), lambda j, c: (j, 0, 0, 0))],
        out_specs=[pl.BlockSpec((bsz, tc, jb * LANES), lambda j, c: (0, c, j)),
                   pl.BlockSpec((bsz, tc, jb * LANES), lambda j, c: (0, bwd_chunk(c), j))],
        out_shape=[out, out],
        scratch_shapes=[pltpu.VMEM((jb, 8, ns), F32), pltpu.VMEM((jb, rows, ns), F32),
                        pltpu.VMEM((jb, rows, LANES), F32), pltpu.VMEM((jb, rows, LANES), F32)],
        compiler_params=_cparams("arbitrary", "arbitrary"),
        name="s5_scan",
    )(p3, p3, b_bd, c_bd, lam8)
    return y_f.reshape(bsz * t_len, width), y_b.reshape(bsz * t_len, width)


def s5_tables(lam_re, lam_im, log_dt, b_re, b_im, c_re, c_im):
    _, g, p = lam_re.shape
    c = b_re.shape[-1]
    nb = g // S5_BLOCK_GROUPS
    gb = S5_BLOCK_GROUPS
    lam = lax.complex(lam_re.astype(F32), lam_im.astype(F32))
    dt = jnp.exp(log_dt.astype(F32))[..., None]
    lam_bar = jnp.exp(lam * dt)
    b_bar = ((lam_bar - 1.0) / lam)[..., None] * lax.complex(b_re.astype(F32), b_im.astype(F32))
    eye = jnp.eye(gb, dtype=F32)
    bb = b_bar.reshape(2, nb, gb, p, c)

    def b_lay(z):
        return jnp.einsum('djgpc,gh->jdgchp', z, eye).reshape(nb, 2 * gb * c, gb * p)

    b_bd = jnp.concatenate([b_lay(bb.real), b_lay(bb.imag)], axis=-1).astype(BF16)
    cc = lax.complex(c_re.astype(F32), c_im.astype(F32)).reshape(2, nb, gb, c, p)

    def c_lay(z):
        return jnp.einsum('djgcp,gh->jgpdhc', z, eye).reshape(nb, gb * p, 2 * gb * c)

    c_bd = jnp.concatenate([c_lay(cc.real), -c_lay(cc.imag)], axis=1).astype(BF16)
    lb = lam_bar.reshape(2, nb, gb * p)
    lam8 = jnp.stack([lb.real, lb.imag], axis=0)
    lam8 = jnp.repeat(lam8.transpose(2, 0, 1, 3), 4, axis=2)
    return b_bd, c_bd, lam8


def _fnet_chan_kernel(f_ref, cd_ref, sd_ref, w_ref, o_ref, wc_ref):
    nh, dh = w_ref.shape[0], w_ref.shape[1]

    @pl.when(pl.program_id(0) == 0)
    def _():
        for h in range(nh):
            wh = w_ref[h]
            wc_ref[h, :, 0:dh] = jnp.dot(cd_ref[...], wh, preferred_element_type=F32,
                                         precision=lax.Precision.HIGHEST).astype(BF16)
            wc_ref[h, :, dh:2 * dh] = jnp.dot(sd_ref[...], wh, preferred_element_type=F32,
                                              precision=lax.Precision.HIGHEST).astype(BF16)

    for h in range(nh):
        r = jnp.dot(f_ref[:, h * dh:(h + 1) * dh], wc_ref[h], preferred_element_type=F32)
        o_ref[:, h * dh:(h + 1) * dh] = r[:, 0:dh].astype(o_ref.dtype)
        o_ref[:, (nh + h) * dh:(nh + h + 1) * dh] = r[:, dh:2 * dh].astype(o_ref.dtype)


def fnet_chan(p, col_block, fnet_w, cd, sd):
    nt = p.shape[0]
    nh, dh, _ = fnet_w.shape
    width = nh * dh
    tm = _pick(nt, (512, 256, 128, 64, 32, 16, 8))
    return pl.pallas_call(
        _fnet_chan_kernel,
        grid=(nt // tm,),
        in_specs=[pl.BlockSpec((tm, width), lambda i: (i, col_block)),
                  pl.BlockSpec((dh, dh), lambda i: (0, 0)),
                  pl.BlockSpec((dh, dh), lambda i: (0, 0)),
                  pl.BlockSpec((nh, dh, dh), lambda i: (0, 0, 0))],
        out_specs=pl.BlockSpec((tm, 2 * width), lambda i: (i, 0)),
        out_shape=jax.ShapeDtypeStruct((nt, 2 * width), BF16),
        scratch_shapes=[pltpu.VMEM((nh, dh, 2 * dh), BF16)],
        compiler_params=_cparams("arbitrary"),
        name="fnet_chan",
    )(p, cd, sd, fnet_w)


def _fnet_seq_kernel(cl_ref, sl_ref, cs_ref, ss_ref, p_ref, o_ref, *, l_len):
    w = o_ref.shape[1]
    tm = o_ref.shape[0]
    t_len = p_ref.shape[0]
    in_ctx = pl.program_id(1) * tm < l_len

    @pl.when(in_ctx)
    def _():
        o_ref[...] = (jnp.dot(cl_ref[...], p_ref[0:l_len, 0:w], preferred_element_type=F32)
                      - jnp.dot(sl_ref[...], p_ref[0:l_len, w:2 * w], preferred_element_type=F32)).astype(o_ref.dtype)

    @pl.when(jnp.logical_not(in_ctx))
    def _():
        o_ref[...] = (jnp.dot(cs_ref[...], p_ref[l_len:t_len, 0:w], preferred_element_type=F32)
                      - jnp.dot(ss_ref[...], p_ref[l_len:t_len, w:2 * w], preferred_element_type=F32)).astype(o_ref.dtype)


def fnet_seq(st, pcs, tab_l, tab_s):
    w2 = pcs.shape[1]
    tm = st.tm
    tpb, lt = st.t // tm, st.l // tm
    ctx_tile = lambda b, i: (jnp.minimum(i, lt - 1), 0)
    lat_tile = lambda b, i: (jnp.maximum(i - lt, 0), 0)
    return pl.pallas_call(
        functools.partial(_fnet_seq_kernel, l_len=st.l),
        grid=(st.b, tpb),
        in_specs=[pl.BlockSpec((tm, st.l), ctx_tile), pl.BlockSpec((tm, st.l), ctx_tile),
                  pl.BlockSpec((tm, st.s), lat_tile), pl.BlockSpec((tm, st.s), lat_tile),
                  pl.BlockSpec((st.t, w2), lambda b, i: (b, 0))],
        out_specs=pl.BlockSpec((tm, w2 // 2), lambda b, i: (b * tpb + i, 0)),
        out_shape=jax.ShapeDtypeStruct((st.nt, w2 // 2), BF16),
        compiler_params=_cparams("arbitrary", "arbitrary"),
        name="fnet_seq",
    )(tab_l[0], tab_l[1], tab_s[0], tab_s[1], pcs)


def dft_tables(n):
    k = jnp.arange(n, dtype=jnp.int32)
    kn = (k[:, None] * k[None, :]) % n
    ang = kn.astype(F32) * (2.0 * math.pi / n)
    scale = 1.0 / math.sqrt(n)
    return jnp.cos(ang) * scale, jnp.sin(ang) * scale


def _pool_kernel(v_ref, w_ref, sc_ref, o_ref, *, windows, l_len):
    dh = w_ref.shape[1]
    for r0, r1 in ((0, l_len), (l_len, v_ref.shape[0])):
        t_len = r1 - r0
        t = lax.broadcasted_iota(jnp.int32, (t_len, dh), 0)
        for g, win in enumerate(windows):
            v = v_ref[r0:r1, g * dh:(g + 1) * dh].astype(F32)
            lo = win // 2
            acc = jnp.zeros_like(v)
            for j in range(-lo, win - lo):
                src = t + j
                shifted = v if j == 0 else pltpu.roll(v, (-j) % t_len, 0)
                acc = acc + jnp.where((src >= 0) & (src < t_len), shifted, 0.0)
            cnt = jnp.clip(t + (win - lo), 0, t_len) - jnp.clip(t - lo, 0, t_len)
            pooled = acc / cnt.astype(F32) - v
            y = jnp.dot(pooled.astype(BF16), w_ref[g].astype(BF16), preferred_element_type=F32)
            o_ref[r0:r1, g * dh:(g + 1) * dh] = (y * sc_ref[:, g * dh:(g + 1) * dh]).astype(o_ref.dtype)


def pool_mix(st, p, pool_w, pool_scale):
    ng, dh, _ = pool_w.shape
    width = ng * dh
    return pl.pallas_call(
        functools.partial(_pool_kernel, windows=POOL_WINDOWS, l_len=st.l),
        grid=(st.b,),
        in_specs=[pl.BlockSpec((st.t, width), lambda b: (b, 0)),
                  pl.BlockSpec((ng, dh, dh), lambda b: (0, 0, 0)),
                  pl.BlockSpec((1, width), lambda b: (0, 0))],
        out_specs=pl.BlockSpec((st.t, width), lambda b: (b, 0)),
        out_shape=jax.ShapeDtypeStruct((st.nt, width), BF16),
        compiler_params=_cparams("arbitrary"),
        name="pool_mix",
    )(p, pool_w, pool_scale.reshape(1, width))


def rope_tables(s, l_len):
    tm = l_len
    rows = s // GRID_W
    row = jnp.repeat(jnp.arange(rows), GRID_W)
    col = jnp.tile(jnp.arange(GRID_W), rows)
    nf = HEAD_DIM // 4
    inv = jnp.power(ROPE_THETA, -jnp.arange(nf, dtype=F32) / nf)
    a_row = row.astype(F32)[:, None] * inv
    a_col = col.astype(F32)[:, None] * inv
    ang = jnp.concatenate([a_row, a_row, a_col, a_col], axis=1)
    sign = jnp.tile(jnp.concatenate([-jnp.ones((nf,), F32), jnp.ones((nf,), F32)]), 2)
    cos_t = jnp.concatenate([jnp.ones((tm, HEAD_DIM), F32), jnp.cos(ang)], axis=0)
    sin_t = jnp.concatenate([jnp.zeros((tm, HEAD_DIM), F32), jnp.sin(ang) * sign], axis=0)
    return cos_t, sin_t


def _norm_rope(x, gain, cos, sin):
    lane = lax.broadcasted_iota(jnp.int32, x.shape, 1)
    first = (lane % (HEAD_DIM // 2)) < (HEAD_DIM // 4)
    n = x * lax.rsqrt(jnp.mean(x * x, axis=-1, keepdims=True) + EPS) * gain
    partner = jnp.where(first, pltpu.roll(n, HEAD_DIM - HEAD_DIM // 4, 1), pltpu.roll(n, HEAD_DIM // 4, 1))
    return n * cos + partner * sin


def _attn_kernel(*refs, q_per_kv, q_scale, l_len):
    q_refs = refs[:q_per_kv]
    kraw_ref, v_ref, g_ref, cq_ref, sq_ref, ck_ref, sk_ref, o_ref, k_scr = refs[q_per_kv:]
    tq = o_ref.shape[0]
    dn = (((1,), (1,)), ((), ()))

    @pl.when(pl.program_id(2) == 0)
    def _():
        k_scr[...] = _norm_rope(kraw_ref[...].astype(F32), g_ref[1], ck_ref[...], sk_ref[...]).astype(BF16)

    def run(n_keys):
        q_gain = g_ref[0] * q_scale
        qs = [_norm_rope(q_ref[...].astype(F32), q_gain, cq_ref[...], sq_ref[...]).astype(BF16) for q_ref in q_refs]
        k = k_scr[0:n_keys, :]
        v = v_ref[0:n_keys, :]
        for g in range(q_per_kv):
            q = qs[g]
            s_ = lax.dot_general(q, k, dn, preferred_element_type=F32)
            e = jnp.exp2(s_ - s_.max(axis=-1, keepdims=True))
            acc = jnp.dot(e.astype(BF16), v, preferred_element_type=F32)
            o_ref[:, g * HEAD_DIM:(g + 1) * HEAD_DIM] = (acc / e.sum(axis=-1, keepdims=True)).astype(o_ref.dtype)

    in_ctx = pl.program_id(2) * tq < l_len

    @pl.when(in_ctx)
    def _():
        run(l_len)

    @pl.when(jnp.logical_not(in_ctx))
    def _():
        run(k_scr.shape[0])


def attention(st, p, gains, cos_t, sin_t, q_col0, n_q_heads, q_scale):
    q_per_kv = n_q_heads // KV_HEADS
    qw = q_per_kv * HEAD_DIM
    tq = st.tm
    tpb = st.t // tq
    k_col0 = q_col0 + n_q_heads
    v_col0 = k_col0 + KV_HEADS
    q_specs = [pl.BlockSpec((tq, HEAD_DIM), lambda b, h, i, g=g: (b * tpb + i, q_col0 + h * q_per_kv + g))
               for g in range(q_per_kv)]
    tab_q = pl.BlockSpec((tq, HEAD_DIM), lambda b, h, i: (i, 0))
    tab_k = pl.BlockSpec((st.t, HEAD_DIM), lambda b, h, i: (0, 0))
    return pl.pallas_call(
        functools.partial(_attn_kernel, q_per_kv=q_per_kv, q_scale=q_scale, l_len=st.l),
        grid=(st.b, KV_HEADS, tpb),
        in_specs=q_specs + [pl.BlockSpec((st.t, HEAD_DIM), lambda b, h, i: (b, k_col0 + h)),
                            pl.BlockSpec((st.t, HEAD_DIM), lambda b, h, i: (b, v_col0 + h)),
                            pl.BlockSpec((2, 1, HEAD_DIM), lambda b, h, i: (0, 0, 0)),
                            tab_q, tab_q, tab_k, tab_k],
        out_specs=pl.BlockSpec((tq, qw), lambda b, h, i: (b * tpb + i, h)),
        out_shape=jax.ShapeDtypeStruct((st.nt, n_q_heads * HEAD_DIM), BF16),
        scratch_shapes=[pltpu.VMEM((st.t, HEAD_DIM), BF16)],
        compiler_params=_cparams("arbitrary", "arbitrary", "arbitrary"),
        name="attention",
    )(*([p] * q_per_kv), p, p, gains, cos_t, sin_t, cos_t, sin_t)


def _pack_pairs(lo, hi):
    lo_b = lax.bitcast_convert_type(lo.astype(BF16).astype(F32), jnp.uint32)
    hi_b = lax.bitcast_convert_type(hi.astype(BF16).astype(F32), jnp.uint32)
    return (lo_b >> 16) | (hi_b & jnp.uint32(0xFFFF0000))


def _unpack_pairs(word):
    lo = lax.bitcast_convert_type(word << 16, F32)
    hi = lax.bitcast_convert_type(word & jnp.uint32(0xFFFF0000), F32)
    return lo, hi


ROW_TILE = 8


def _store_row_tiles(ref, word):
    rows = word.shape[0]
    for j in range(ROW_TILE):
        ref[pl.ds(j, rows, stride=ROW_TILE), :] = word[:, j * LANES:(j + 1) * LANES]


def _load_row_tiles(ref, rows):
    return [ref[pl.ds(j, rows, stride=ROW_TILE), :] for j in range(ROW_TILE)]


def _router_kernel(x_ref, g_ref, sh_ref, sc_ref, wr_ref, br_ref, h_ref, idx_ref, wgt_ref):
    x = x_ref[...]
    y = x * lax.rsqrt(jnp.mean(x * x, axis=-1, keepdims=True) + EPS) * g_ref[...]
    h = y * (1.0 + sc_ref[...]) + sh_ref[...]
    half = h.shape[1] // 2
    _store_row_tiles(h_ref, _pack_pairs(h[:, 0:half], h[:, half:2 * half]))
    ne = wr_ref.shape[0]
    dn = (((1,), (1,)), ((), ()))
    h1 = h.astype(BF16)
    h2 = (h - h1.astype(F32)).astype(BF16)
    w = wr_ref[...]
    w1 = w.astype(BF16)
    w2 = (w - w1.astype(F32)).astype(BF16)
    r1 = lax.dot_general(jnp.concatenate([w1, w2], axis=0), h1, dn, preferred_element_type=F32)
    r2 = lax.dot_general(w1, h2, dn, preferred_element_type=F32)
    logits = r1[0:ne] + r1[ne:2 * ne] + r2 + br_ref[...]
    ne, tm = logits.shape
    eidx = lax.broadcasted_iota(jnp.int32, (ne, tm), 0)
    vals, idxs = [], []
    cur = logits
    for _ in range(TOP_K):
        m = cur.max(axis=0, keepdims=True)
        sel = jnp.min(jnp.where(cur == m, eidx, ne), axis=0, keepdims=True)
        vals.append(m)
        idxs.append(sel)
        cur = jnp.where(eidx == sel, -jnp.inf, cur)
    es = [jnp.exp(v - vals[0]) for v in vals]
    den = es[0]
    for e in es[1:]:
        den = den + e
    zi = jnp.zeros((8 - TOP_K, tm), jnp.int32)
    zf = jnp.zeros((8 - TOP_K, tm), F32)
    idx_ref[...] = jnp.concatenate(idxs + [zi], axis=0)
    wgt_ref[...] = jnp.concatenate([e / den for e in es] + [zf], axis=0)


def norm_router(st, x, g, mod, k_shift, k_scale, w_router, b_router):
    nt, d = x.shape
    ne = w_router.shape[1]
    tm = _pick(st.tm, (256, 128))
    return pl.pallas_call(
        _router_kernel,
        grid=(nt // tm,),
        in_specs=[pl.BlockSpec((tm, d), lambda i: (i, 0)),
                  pl.BlockSpec((1, d), lambda i: (0, 0)),
                  _mod_spec(st, tm, k_shift, d, 1, 0),
                  _mod_spec(st, tm, k_scale, d, 1, 0),
                  pl.BlockSpec((ne, d), lambda i: (0, 0)),
                  pl.BlockSpec((ne, 1), lambda i: (0, 0))],
        out_specs=[pl.BlockSpec((tm * ROW_TILE, LANES), lambda i: (i, 0)),
                   pl.BlockSpec((8, tm), lambda i: (0, i)),
                   pl.BlockSpec((8, tm), lambda i: (0, i))],
        out_shape=[jax.ShapeDtypeStruct((nt * ROW_TILE, LANES), jnp.uint32),
                   jax.ShapeDtypeStruct((8, nt), jnp.int32),
                   jax.ShapeDtypeStruct((8, nt), F32)],
        compiler_params=_cparams("arbitrary"),
        name="norm_router",
    )(x, g.reshape(1, d), mod, mod, w_router.T, b_router.reshape(ne, 1))


def _route_plan_kernel(idx_ref, dest_ref, blk_ref, ends_ref, pre_ref, *, ne, tb, chunk):
    nt = idx_ref.shape[1]
    n_chunk = nt // chunk
    eidx = lax.broadcasted_iota(jnp.int32, (ne, chunk), 0)
    tri = (lax.broadcasted_iota(jnp.int32, (chunk, chunk), 0)
           < lax.broadcasted_iota(jnp.int32, (chunk, chunk), 1)).astype(BF16)

    def count(c, carry):
        c0 = pl.multiple_of(c * chunk, chunk)
        oh = jnp.zeros((ne, chunk), F32)
        for k in range(TOP_K):
            oh = oh + (eidx == idx_ref[k:k + 1, pl.ds(c0, chunk)]).astype(F32)
        pre_ref[:, pl.ds(c0, chunk)] = jnp.dot(oh.astype(BF16), tri, preferred_element_type=F32) + carry
        return carry + oh.sum(axis=1, keepdims=True)

    counts = lax.fori_loop(0, n_chunk, count, jnp.zeros((ne, 1), F32))
    padded = jnp.ceil(counts * (1.0 / tb)) * tb
    r_i = lax.broadcasted_iota(jnp.int32, (ne, ne), 0)
    c_i = lax.broadcasted_iota(jnp.int32, (ne, ne), 1)
    padded_row = jnp.sum(jnp.where(r_i == c_i, padded, 0.0), axis=0, keepdims=True)
    pstart = jnp.sum(jnp.where(c_i < r_i, padded_row, 0.0), axis=1, keepdims=True)
    pend = pstart + padded

    def place(c, carry):
        c0 = pl.multiple_of(c * chunk, chunk)
        base = pre_ref[:, pl.ds(c0, chunk)] + pstart
        rows = [jnp.sum(jnp.where(eidx == idx_ref[k:k + 1, pl.ds(c0, chunk)], base, 0.0), axis=0, keepdims=True)
                for k in range(TOP_K)]
        rows.append(jnp.zeros((8 - TOP_K, chunk), F32))
        dest_ref[:, pl.ds(c0, chunk)] = jnp.concatenate(rows, axis=0).astype(jnp.int32)
        return carry

    lax.fori_loop(0, n_chunk, place, 0)

    nbp = blk_ref.shape[1]
    starts = (lax.broadcasted_iota(jnp.int32, (ne, nbp), 1) * tb).astype(F32)
    blk_e = jnp.minimum(jnp.sum((pend <= starts).astype(F32), axis=0, keepdims=True), ne - 1.0)
    n_live = jnp.sum(padded, axis=0, keepdims=True) * (1.0 / tb)
    e_col = lax.broadcasted_iota(jnp.int32, (ne, nbp), 0).astype(F32)
    owns = jnp.where(padded > 0.0, 1.0, 0.0)
    later = jnp.where(e_col > blk_e, owns, 0.0)
    nxt = jnp.min(jnp.where(later > 0.0, e_col, float(ne)), axis=0, keepdims=True)
    nxt = jnp.where(nxt >= float(ne), -1.0, nxt)
    rank = jnp.sum(jnp.where(e_col < blk_e, owns, 0.0), axis=0, keepdims=True)
    slot = rank - 2.0 * jnp.floor(rank * 0.5)
    blk_ref[...] = jnp.concatenate([blk_e, jnp.broadcast_to(n_live, (1, nbp)), nxt, slot, jnp.zeros((4, nbp), F32)],
                                   axis=0).astype(jnp.int32)

    r_l = lax.broadcasted_iota(jnp.int32, (ne, LANES), 0)
    c_l = lax.broadcasted_iota(jnp.int32, (ne, LANES), 1)
    to_row = lambda col: jnp.sum(jnp.where(r_l == c_l, col, 0.0), axis=0, keepdims=True)
    ends_ref[...] = jnp.concatenate([to_row(pstart + counts), to_row(pend), jnp.zeros((6, LANES), F32)],
                                    axis=0).astype(jnp.int32)


def route_plan(idx8, ne, tb, n_blocks):
    nt = idx8.shape[1]
    chunk = _pick(nt, (256, 128))
    nbp = -(-n_blocks // LANES) * LANES
    return pl.pallas_call(
        functools.partial(_route_plan_kernel, ne=ne, tb=tb, chunk=chunk),
        out_shape=[jax.ShapeDtypeStruct((8, nt), jnp.int32), jax.ShapeDtypeStruct((8, nbp), jnp.int32),
                   jax.ShapeDtypeStruct((8, LANES), jnp.int32)],
        scratch_shapes=[pltpu.VMEM((ne, nt), F32)],
        compiler_params=pltpu.CompilerParams(vmem_limit_bytes=VMEM_LIMIT),
        name="route_plan",
    )(idx8)


def _dispatch_kernel(dest_ref, ends_ref, h_ref, xs_hbm, zrow, sem, *, rows, ne, nt):
    i = pl.program_id(0)
    base = i * rows

    @pl.when(i == 0)
    def _():
        zrow[...] = jnp.zeros_like(zrow)

        def per_expert(e, carry, waiting):
            lo, hi = ends_ref[0, e], ends_ref[1, e]

            def put(r, c):
                dst = xs_hbm.at[pl.ds(pl.multiple_of(r * ROW_TILE, ROW_TILE), ROW_TILE)]
                pltpu.make_async_copy(zrow.at[pl.ds(0, ROW_TILE)], dst, sem.at[1]).start()
                return c

            def done(r, c):
                pltpu.make_async_copy(zrow.at[pl.ds(0, ROW_TILE)], xs_hbm.at[pl.ds(0, ROW_TILE)], sem.at[1]).wait()
                return c

            lax.fori_loop(lo, hi, done if waiting else put, 0)
            return carry

        for waiting in (False, True):
            lax.fori_loop(0, ne, functools.partial(per_expert, waiting=waiting), 0)

        blk = zrow.shape[0]
        tb = blk // ROW_TILE
        first_dead = ends_ref[1, ne - 1] // tb
        n_blk = xs_hbm.shape[0] // blk

        def put_blk(j, c):
            pltpu.make_async_copy(zrow, xs_hbm.at[pl.ds(pl.multiple_of(j * blk, blk), blk)], sem.at[1]).start()
            return c

        def done_blk(j, c):
            pltpu.make_async_copy(zrow, xs_hbm.at[pl.ds(0, blk)], sem.at[1]).wait()
            return c

        lax.fori_loop(first_dead, n_blk, put_blk, 0)
        lax.fori_loop(first_dead, n_blk, done_blk, 0)

    def issue(r, carry):
        src = h_ref.at[pl.ds(pl.multiple_of(r * ROW_TILE, ROW_TILE), ROW_TILE)]
        for k in range(TOP_K):
            d = dest_ref[k * nt + base + r]
            dst = xs_hbm.at[pl.ds(pl.multiple_of(d * ROW_TILE, ROW_TILE), ROW_TILE)]
            pltpu.make_async_copy(src, dst, sem.at[0]).start(priority=k % 2)
        return carry

    lax.fori_loop(0, rows, issue, 0, unroll=4)
    for _ in range(TOP_K):
        pltpu.make_async_copy(h_ref, xs_hbm.at[pl.ds(0, rows * ROW_TILE)], sem.at[0]).wait()


def moe_dispatch(h_tiles, dest1, ends, cap, ne):
    nt = h_tiles.shape[0] // ROW_TILE
    rows = _pick(nt, (256, 128, 64, 32, 16, 8))
    return pl.pallas_call(
        functools.partial(_dispatch_kernel, rows=rows, ne=ne, nt=nt),
        grid=(nt // rows,),
        in_specs=[pl.BlockSpec(memory_space=pltpu.SMEM),
                  pl.BlockSpec(memory_space=pltpu.SMEM),
                  pl.BlockSpec((rows * ROW_TILE, LANES), lambda i: (i, 0))],
        out_specs=pl.BlockSpec(memory_space=pl.ANY),
        out_shape=jax.ShapeDtypeStruct((cap * ROW_TILE, LANES), h_tiles.dtype),
        scratch_shapes=[pltpu.VMEM((MOE_ROWS * ROW_TILE, LANES), h_tiles.dtype), pltpu.SemaphoreType.DMA((2,))],
        compiler_params=_cparams("arbitrary"),
        name="moe_dispatch",
    )(dest1, ends, h_tiles)


def _expert_kernel(be_ref, nb_ref, nxt_ref, slot_ref, x_ref, bg_ref, bu_ref, bd_ref, wg_hbm, wu_hbm, wd_hbm, o_ref,
                   wg_st, wu_st, wd_st, wg_bf, wu_bf, wd_bf, sem, *, layer):
    i = pl.program_id(0)
    e = be_ref[i]
    live = i < nb_ref[0]
    first = (i == 0) | (e != be_ref[jnp.maximum(i - 1, 0)])
    slot = slot_ref[i]
    half = wg_bf.shape[0] // 2

    def weight_copies(ex, s):
        return (pltpu.make_async_copy(wg_hbm.at[layer, ex], wg_st.at[s], sem.at[s, 0]),
                pltpu.make_async_copy(wu_hbm.at[layer, ex], wu_st.at[s], sem.at[s, 1]),
                pltpu.make_async_copy(wd_hbm.at[layer, ex], wd_st.at[s], sem.at[s, 2]))

    @pl.when(i == 0)
    def _():
        for cp in weight_copies(e, slot):
            cp.start(priority=1)

    @pl.when(live & first)
    def _():
        for cp in weight_copies(e, slot):
            cp.wait()

        @pl.when(nxt_ref[i] >= 0)
        def _():
            for cp in weight_copies(nxt_ref[i], 1 - slot):
                cp.start(priority=1)

        wg_bf[...] = wg_st[slot].astype(BF16)
        wu_bf[...] = wu_st[slot].astype(BF16)
        wd_bf[...] = wd_st[slot].astype(BF16)

    @pl.when(live)
    def _():
        tb = x_ref.shape[0] // ROW_TILE
        lo, hi = _unpack_pairs(jnp.concatenate(_load_row_tiles(x_ref, tb), axis=1))
        lo = lo.astype(BF16)
        hi = hi.astype(BF16)

        def proj(w_bf, b_ref):
            return (jnp.dot(lo, w_bf[0:half, :], preferred_element_type=F32)
                    + jnp.dot(hi, w_bf[half:2 * half, :], preferred_element_type=F32) + b_ref[...])

        g = jnp.minimum(proj(wg_bf, bg_ref), SWIGLU_LIMIT)
        u = jnp.clip(proj(wu_bf, bu_ref), -SWIGLU_LIMIT, SWIGLU_LIMIT)
        a = g * jax.nn.sigmoid(SWIGLU_ALPHA * g) * (u + 1.0)
        y = jnp.dot(a.astype(BF16), wd_bf[...], preferred_element_type=F32) + bd_ref[...]
        _store_row_tiles(o_ref, _pack_pairs(y[:, 0:half], y[:, half:2 * half]))

    @pl.when(jnp.logical_not(live))
    def _():
        o_ref[...] = jnp.zeros_like(o_ref)


def expert_ffn(x_sorted, blk, n_blocks, layer, w_gate, b_gate, w_up, b_up, w_down, b_down):
    depth, ne, d, f = w_gate.shape
    blk_rows = MOE_ROWS * ROW_TILE
    bias = lambda n: pl.BlockSpec((None, 1, n), lambda i, be, nb, nx, sl: (layer * ne + be[i], 0, 0))
    hbm = pl.BlockSpec(memory_space=pl.ANY)
    grid_spec = pltpu.PrefetchScalarGridSpec(
        num_scalar_prefetch=4,
        grid=(n_blocks,),
        in_specs=[pl.BlockSpec((blk_rows, LANES), lambda i, be, nb, nx, sl: (jnp.minimum(i, nb[0] - 1), 0)),
                  bias(f), bias(f), bias(d), hbm, hbm, hbm],
        out_specs=pl.BlockSpec((blk_rows, LANES), lambda i, be, nb, nx, sl: (i, 0)),
        scratch_shapes=[pltpu.VMEM((2, d, f), F32), pltpu.VMEM((2, d, f), F32), pltpu.VMEM((2, f, d), F32),
                        pltpu.VMEM((d, f), BF16), pltpu.VMEM((d, f), BF16), pltpu.VMEM((f, d), BF16),
                        pltpu.SemaphoreType.DMA((2, 3))],
    )
    return pl.pallas_call(
        functools.partial(_expert_kernel, layer=layer),
        grid_spec=grid_spec,
        out_shape=jax.ShapeDtypeStruct(x_sorted.shape, jnp.uint32),
        compiler_params=_cparams("arbitrary"),
        name="expert_ffn",
    )(blk[0, :n_blocks], blk[1, :1], blk[2, :n_blocks], blk[3, :n_blocks], x_sorted,
      b_gate.reshape(depth * ne, 1, f), b_up.reshape(depth * ne, 1, f), b_down.reshape(depth * ne, 1, d),
      w_gate, w_up, w_down)


def _combine_kernel(dest_ref, x_ref, w_ref, gate_ref, y_hbm, o_ref, buf, sem, *, tm, nt):
    i = pl.program_id(0)
    n = pl.num_programs(0)
    slot = i % 2
    half = ROW_TILE * LANES

    def fetch(tile, s):
        def issue(r, carry):
            for k in range(TOP_K):
                d = dest_ref[k * nt + tile * tm + r]
                src = y_hbm.at[pl.ds(pl.multiple_of(d * ROW_TILE, ROW_TILE), ROW_TILE)]
                dst = buf.at[s, k, pl.ds(pl.multiple_of(r * ROW_TILE, ROW_TILE), ROW_TILE)]
                pltpu.make_async_copy(src, dst, sem.at[s]).start(priority=k % 2)
            return carry
        lax.fori_loop(0, tm, issue, 0, unroll=4)

    @pl.when(i == 0)
    def _():
        fetch(0, 0)

    @pl.when(i + 1 < n)
    def _():
        fetch(i + 1, 1 - slot)

    for k in range(TOP_K):
        pltpu.make_async_copy(y_hbm.at[pl.ds(0, tm * ROW_TILE)], buf.at[slot, k], sem.at[slot]).wait()

    ws = [w_ref[:, k:k + 1] for k in range(TOP_K)]
    for j in range(ROW_TILE):
        acc_lo = None
        acc_hi = None
        for k in range(TOP_K):
            lo, hi = _unpack_pairs(buf[slot, k, pl.ds(j, tm, stride=ROW_TILE), :])
            acc_lo = ws[k] * lo if acc_lo is None else acc_lo + ws[k] * lo
            acc_hi = ws[k] * hi if acc_hi is None else acc_hi + ws[k] * hi
        c_lo = slice(j * LANES, (j + 1) * LANES)
        c_hi = slice(half + j * LANES, half + (j + 1) * LANES)
        o_ref[:, c_lo] = x_ref[:, c_lo] + gate_ref[:, c_lo] * acc_lo
        o_ref[:, c_hi] = x_ref[:, c_hi] + gate_ref[:, c_hi] * acc_hi


def moe_combine(st, x, y_sorted, dest1, w_tok, mod, k_gate):
    nt, d = x.shape
    assert d // 2 == ROW_TILE * LANES
    tm = _pick(st.tm, (128, 64, 32, 16, 8))
    return pl.pallas_call(
        functools.partial(_combine_kernel, tm=tm, nt=nt),
        grid=(nt // tm,),
        in_specs=[pl.BlockSpec(memory_space=pltpu.SMEM),
                  pl.BlockSpec((tm, d), lambda i: (i, 0)),
                  pl.BlockSpec((tm, TOP_K), lambda i: (i, 0)),
                  _mod_spec(st, tm, k_gate, d, 1, 0),
                  pl.BlockSpec(memory_space=pl.ANY)],
        out_specs=pl.BlockSpec((tm, d), lambda i: (i, 0)),
        out_shape=jax.ShapeDtypeStruct((nt, d), F32),
        scratch_shapes=[pltpu.VMEM((2, TOP_K, tm * ROW_TILE, LANES), jnp.uint32), pltpu.SemaphoreType.DMA((2,))],
        input_output_aliases={1: 0},
        compiler_params=_cparams("arbitrary"),
        name="moe_combine",
    )(dest1, x, w_tok, mod, y_sorted)


def moe_layer(st, x, g, mod, layer, w_router, b_router, w_gate, b_gate, w_up, b_up, w_down, b_down):
    nt, d = x.shape
    ne = w_router.shape[1]
    tb = MOE_ROWS
    h_packed, idx8, wgt8 = norm_router(st, x, g, mod, 3, 4, w_router, b_router)
    n_assign = nt * TOP_K
    cap = -(-(n_assign + ne * (tb - 1)) // tb) * tb
    n_blocks = cap // tb
    dest8, blk, ends = route_plan(idx8, ne, tb, n_blocks)
    dest1 = dest8[:TOP_K].reshape(-1)
    x_sorted = moe_dispatch(h_packed, dest1, ends, cap, ne)
    y_sorted = expert_ffn(x_sorted, blk, n_blocks, layer, w_gate, b_gate, w_up, b_up, w_down, b_down)
    return moe_combine(st, x, y_sorted, dest1, wgt8[:TOP_K].T, mod, 5)


def even_mixer(st, x, norm_g, mod, i, w_in, w_out, s5p, d_skip, glu_w, glu_b, fnet_w, tables):
    d = x.shape[1]
    fw = fnet_w.shape[0] * fnet_w.shape[1]
    sw = d - fw
    p = norm_matmul(st, x, norm_g, mod, 0, 1, w_in, i)
    b_bd, c_bd, lam8 = s5p
    y_f, y_b = s5_scan(st, p, sw, b_bd, c_bd, lam8)
    a = gelu_glu(st, y_f, y_b, p, d_skip, glu_w, i, glu_b)
    cd, sd, tab_s, tab_l = tables
    pcs = fnet_chan(p, sw // fw, fnet_w, cd, sd)
    fm = fnet_seq(st, pcs, tab_l, tab_s)
    return matmul([a, fm], w_out, i, mode="resid", extra=(x, mod, 2), st=st)


def odd_mixer(st, x, norm_g, mod, i, w_in, w_out, pool_w, pool_scale, q_gain, k_gain, rope):
    d = x.shape[1]
    pw = pool_w.shape[0] * pool_w.shape[1]
    n_q = (d - pw) // HEAD_DIM
    p = norm_matmul(st, x, norm_g, mod, 0, 1, w_in, i)
    cos_t, sin_t = rope
    gains = jnp.stack([q_gain, k_gain]).reshape(2, 1, HEAD_DIM)
    att = attention(st, p, gains, cos_t, sin_t, pw // HEAD_DIM, n_q, HEAD_DIM ** -0.5 * math.log2(math.e))
    pm = pool_mix(st, p, pool_w, pool_scale)
    return matmul([pm, att], w_out, i, mode="resid", extra=(x, mod, 2), st=st)


def kernel(x, c, ctx, c_ctx, ada_w, ada_b, norm_mix_g, norm_ffn_g, ev_w_in, ev_w_out, s5_lam_re, s5_lam_im, s5_log_dt, s5_b_re, s5_b_im, s5_c_re, s5_c_im, s5_d, s5_glu_w, s5_glu_b, fnet_w, od_w_in, od_w_out, pool_w, pool_scale, q_gain, k_gain, router_w, router_b, exp_w_gate, exp_b_gate, exp_w_up, exp_b_up, exp_w_down, exp_b_down, final_g):
    batch, seq, d = x.shape
    ctx_len = ctx.shape[1]
    depth = ada_w.shape[0]
    st = Stream(batch, seq, ctx_len)
    xs = jnp.concatenate([ctx, x], axis=1).reshape(st.nt, d)

    cond8 = jnp.concatenate([c, c_ctx[None], jnp.zeros((8 - batch - 1, d), F32)], axis=0)
    mod_all = adaln_all(cond8, ada_w, ada_b)
    mod_all = mod_all.reshape(depth, 8, 6, 1, d).transpose(0, 2, 1, 3, 4)

    dh = fnet_w.shape[2]
    cd, sd = dft_tables(dh)
    tab_s = tuple(t.astype(BF16) for t in dft_tables(seq))
    tab_l = tuple(t.astype(BF16) for t in dft_tables(ctx_len))
    tables = (cd, sd, tab_s, tab_l)
    rope = rope_tables(seq, ctx_len)

    for layer in range(depth):
        i = layer // 2
        mod = mod_all[layer]
        if layer % 2 == 0:
            s5p = s5_tables(s5_lam_re[i], s5_lam_im[i], s5_log_dt[i], s5_b_re[i], s5_b_im[i], s5_c_re[i], s5_c_im[i])
            xs = even_mixer(st, xs, norm_mix_g[layer], mod, i, ev_w_in, ev_w_out, s5p, s5_d[i], s5_glu_w,
                            s5_glu_b[i], fnet_w[i], tables)
        else:
            xs = odd_mixer(st, xs, norm_mix_g[layer], mod, i, od_w_in, od_w_out, pool_w[i], pool_scale[i],
                           q_gain[i], k_gain[i], rope)
        xs = moe_layer(st, xs, norm_ffn_g[layer], mod, layer, router_w[layer], router_b[layer],
                       exp_w_gate, exp_b_gate, exp_w_up, exp_b_up, exp_w_down, exp_b_down)
    return final_norm(st, xs, final_g).reshape(batch, seq, d)
```

```python
import functools
import math

import jax
import jax.numpy as jnp
from jax import lax
from jax.experimental import pallas as pl
from jax.experimental.pallas import tpu as pltpu

F32 = jnp.float32
BF16 = jnp.bfloat16
EPS = 1e-6

GRID_W = 64
FNET_HEADS = 4
S5_GROUP = 16
S5_STATE = 64
POOL_WINDOWS = (2, 4, 8, 16)
HEAD_DIM = 128
KV_HEADS = 4
ROPE_THETA = 10000.0
TOP_K = 4
SWIGLU_LIMIT = 7.0
SWIGLU_ALPHA = 1.702

LANES = 128
S5_BLOCK_GROUPS = LANES // S5_GROUP
VMEM_LIMIT = 48 * 1024 * 1024
MOE_ROWS = 256


def _cparams(*sem):
    return pltpu.CompilerParams(dimension_semantics=sem, vmem_limit_bytes=VMEM_LIMIT)


def _pick(n, prefs):
    for p in prefs:
        if n % p == 0:
            return p
    return n


def _adaln_kernel(c_ref, w_ref, b_ref, o_ref):
    c = c_ref[...]
    a = (c * jax.nn.sigmoid(c)).astype(BF16)
    o_ref[...] = jnp.dot(a, w_ref[...].astype(BF16), preferred_element_type=F32) + b_ref[...]


def adaln_all(cond8, ada_w, ada_b):
    depth, d, n = ada_w.shape
    tn = _pick(n, (1024, 512, 256, 128))
    return pl.pallas_call(
        _adaln_kernel,
        grid=(depth, n // tn),
        in_specs=[pl.BlockSpec((8, d), lambda l, j: (0, 0)),
                  pl.BlockSpec((None, d, tn), lambda l, j: (l, 0, j)),
                  pl.BlockSpec((None, 1, tn), lambda l, j: (l, 0, j))],
        out_specs=pl.BlockSpec((None, 8, tn), lambda l, j: (l, 0, j)),
        out_shape=jax.ShapeDtypeStruct((depth, 8, n), F32),
        compiler_params=_cparams("arbitrary", "arbitrary"),
        name="adaln",
    )(cond8, ada_w, ada_b.reshape(depth, 1, n))


class Stream:
    def __init__(self, batch, seq, ctx_len):
        self.b, self.s, self.l = batch, seq, ctx_len
        self.t = seq + ctx_len
        self.nt = batch * self.t
        self.tm = _pick(math.gcd(seq, ctx_len), (256, 128, 64, 32, 16))

    def tiles(self, tm, lat_only):
        if not lat_only:
            return self.nt // tm, (lambda i: i), self.mod_row(tm)
        tpb, lt, ns = self.t // tm, self.l // tm, self.s // tm
        return self.b * ns, (lambda i: (i // ns) * tpb + lt + i % ns), (lambda i: i // ns)

    def mod_row(self, tm):
        tpb, l, b = self.t // tm, self.l, self.b
        return lambda i: jnp.where((i % tpb) * tm < l, b, i // tpb)


def _mod_spec(st, tm, k, d, grid_rank, row_axis):
    mr = st.mod_row(tm)
    if grid_rank == 1:
        return pl.BlockSpec((None, None, 1, d), lambda i: (k, mr(i), 0, 0))
    if row_axis == 1:
        return pl.BlockSpec((None, None, 1, d), lambda j, i: (k, mr(i), 0, j))
    raise ValueError


def _mm_kernel(*refs, n_a, k_offs, mode):
    a_refs = refs[:n_a]
    w_ref = refs[n_a]
    rest = refs[n_a + 1:]
    wbf_ref = rest[-1]
    o_ref = rest[-2]

    @pl.when(pl.program_id(1) == 0)
    def _():
        wbf_ref[...] = w_ref[...].astype(BF16)

    acc = None
    for a_ref, (k0, k1) in zip(a_refs, k_offs):
        part = jnp.dot(a_ref[...], wbf_ref[k0:k1, :], preferred_element_type=F32)
        acc = part if acc is None else acc + part
    if mode == "plain":
        o_ref[...] = acc.astype(o_ref.dtype)
    elif mode == "resid":
        x_ref, gate_ref = rest[0], rest[1]
        o_ref[...] = x_ref[...] + gate_ref[...] * acc
    elif mode == "glu":
        g_ref, b_ref = rest[0], rest[1]
        o_ref[...] = (g_ref[...].astype(F32) * jax.nn.sigmoid(acc + b_ref[...])).astype(o_ref.dtype)
    else:
        raise ValueError(mode)


def _norm_mm_kernel(x_ref, g_ref, sh_ref, sc_ref, w_ref, o_ref, wbf_ref):
    @pl.when(pl.program_id(1) == 0)
    def _():
        wbf_ref[...] = w_ref[...].astype(BF16)

    x = x_ref[...]
    y = x * lax.rsqrt(jnp.mean(x * x, axis=-1, keepdims=True) + EPS) * g_ref[...]
    h = (y * (1.0 + sc_ref[...]) + sh_ref[...]).astype(BF16)
    o_ref[...] = jnp.dot(h, wbf_ref[...], preferred_element_type=F32).astype(o_ref.dtype)


def norm_matmul(st, x, g, mod, k_shift, k_scale, w_stack, w_idx, tn_prefs=(2048, 1536, 1024, 512, 256, 128)):
    m, k = x.shape
    n = w_stack.shape[2]
    tm = st.tm
    tn = _pick(n, tn_prefs)
    mr = st.mod_row(tm)
    mod_spec = lambda kk: pl.BlockSpec((None, None, 1, k), lambda j, i: (kk, mr(i), 0, 0))
    return pl.pallas_call(
        _norm_mm_kernel,
        grid=(n // tn, m // tm),
        in_specs=[pl.BlockSpec((tm, k), lambda j, i: (i, 0)),
                  pl.BlockSpec((1, k), lambda j, i: (0, 0)),
                  mod_spec(k_shift), mod_spec(k_scale),
                  pl.BlockSpec((None, k, tn), lambda j, i: (w_idx, 0, j), pipeline_mode=pl.Buffered(1))],
        out_specs=pl.BlockSpec((tm, tn), lambda j, i: (i, j)),
        out_shape=jax.ShapeDtypeStruct((m, n), BF16),
        scratch_shapes=[pltpu.VMEM((k, tn), BF16)],
        compiler_params=_cparams("arbitrary", "arbitrary"),
        name="norm_mm",
    )(x, g.reshape(1, k), mod, mod, w_stack)


def _gelu_glu_kernel(yf_ref, yb_ref, u_ref, d_ref, w_ref, b_ref, o_ref, wbf_ref):
    @pl.when(pl.program_id(0) == 0)
    def _():
        wbf_ref[...] = w_ref[...].astype(BF16)

    y = yf_ref[...].astype(F32) + yb_ref[...].astype(F32) + d_ref[...] * u_ref[...].astype(F32)
    g = jax.nn.gelu(y).astype(BF16)
    acc = jnp.dot(g, wbf_ref[...], preferred_element_type=F32)
    o_ref[...] = (g.astype(F32) * jax.nn.sigmoid(acc + b_ref[...])).astype(o_ref.dtype)


def gelu_glu(st, y_f, y_b, p, d_skip, w_stack, w_idx, bias):
    m, width = y_f.shape
    tm = st.tm
    row = pl.BlockSpec((tm, width), lambda i: (i, 0))
    vec = pl.BlockSpec((1, width), lambda i: (0, 0))
    return pl.pallas_call(
        _gelu_glu_kernel,
        grid=(m // tm,),
        in_specs=[row, row, row, vec,
                  pl.BlockSpec((None, width, width), lambda i: (w_idx, 0, 0), pipeline_mode=pl.Buffered(1)),
                  vec],
        out_specs=row,
        out_shape=jax.ShapeDtypeStruct((m, width), BF16),
        scratch_shapes=[pltpu.VMEM((width, width), BF16)],
        compiler_params=_cparams("arbitrary"),
        name="gelu_glu",
    )(y_f, y_b, p, d_skip.reshape(1, width), w_stack, bias.reshape(1, width))


def matmul(a_parts, w_stack, w_idx, mode="plain", out_dtype=BF16, extra=(), st=None,
           tn_prefs=(2048, 1536, 1024, 512, 256, 128)):
    m = a_parts[0].shape[0]
    _, k, n = w_stack.shape
    tm = _pick(m if st is None else st.tm, (512, 256, 128, 64, 32, 16, 8))
    tn = _pick(n, tn_prefs)
    k_offs, off = [], 0
    in_specs, args = [], []
    for a in a_parts:
        ka = a.shape[1]
        k_offs.append((off, off + ka))
        off += ka
        in_specs.append(pl.BlockSpec((tm, ka), lambda j, i: (i, 0)))
        args.append(a)
    assert off == k
    in_specs.append(pl.BlockSpec((None, k, tn), lambda j, i: (w_idx, 0, j), pipeline_mode=pl.Buffered(1)))
    args.append(w_stack)
    io_alias = {}
    if mode == "resid":
        x, mod, k_gate = extra
        in_specs.append(pl.BlockSpec((tm, tn), lambda j, i: (i, j)))
        args.append(x)
        in_specs.append(_mod_spec(st, tm, k_gate, tn, 2, 1))
        args.append(mod)
        io_alias = {len(args) - 2: 0}
        out_dtype = F32
    elif mode == "glu":
        g, bias = extra
        in_specs.append(pl.BlockSpec((tm, tn), lambda j, i: (i, j)))
        args.append(g)
        in_specs.append(pl.BlockSpec((1, tn), lambda j, i: (0, j)))
        args.append(bias.reshape(1, n))
    return pl.pallas_call(
        functools.partial(_mm_kernel, n_a=len(a_parts), k_offs=tuple(k_offs), mode=mode),
        grid=(n // tn, m // tm),
        in_specs=in_specs,
        out_specs=pl.BlockSpec((tm, tn), lambda j, i: (i, j)),
        out_shape=jax.ShapeDtypeStruct((m, n), out_dtype),
        scratch_shapes=[pltpu.VMEM((k, tn), BF16)],
        input_output_aliases=io_alias,
        compiler_params=_cparams("arbitrary", "arbitrary"),
        name="mm_" + mode,
    )(*args)


def _s5_kernel(uf_ref, ub_ref, b_ref, c_ref, lam_ref, yf_ref, yb_ref, h_ref, bu_ref, u_scr, y_scr, *, tc, jb):
    half = b_ref.shape[2] // 2
    nbat = uf_ref.shape[0]

    @pl.when(pl.program_id(1) == 0)
    def _():
        h_ref[...] = jnp.zeros_like(h_ref)

    rows = tc * 8
    is_fwd = (lax.broadcasted_iota(jnp.int32, (rows, LANES), 0) % 8) < 4
    flip = (lax.broadcasted_iota(jnp.int32, (tc, tc), 0)
            + lax.broadcasted_iota(jnp.int32, (tc, tc), 1) == tc - 1).astype(BF16)
    for q in range(jb):
        lanes = slice(q * LANES, (q + 1) * LANES)
        for b in range(nbat):
            u_scr[q, pl.ds(b, tc, stride=8), :] = uf_ref[b, :, lanes].astype(F32)
            u_scr[q, pl.ds(nbat + b, tc, stride=8), :] = jnp.dot(flip, ub_ref[b, :, lanes],
                                                                 preferred_element_type=F32)
        u = u_scr[q]
        lhs = jnp.concatenate([jnp.where(is_fwd, u, 0.0), jnp.where(is_fwd, 0.0, u)], axis=1).astype(BF16)
        bu_ref[q] = jnp.dot(lhs, b_ref[q], preferred_element_type=F32)

    for q in range(jb):
        lam_r = lam_ref[q, 0]
        lam_i = lam_ref[q, 1]

        def step(t, carry, q=q, lam_r=lam_r, lam_i=lam_i):
            hr, hi = carry
            r0 = pl.multiple_of(t * 8, 8)
            nhr = lam_r * hr - lam_i * hi + bu_ref[q, pl.ds(r0, 8), 0:half]
            nhi = lam_r * hi + lam_i * hr + bu_ref[q, pl.ds(r0, 8), half:2 * half]
            bu_ref[q, pl.ds(r0, 8), 0:half] = nhr
            bu_ref[q, pl.ds(r0, 8), half:2 * half] = nhi
            return nhr, nhi

        hr, hi = lax.fori_loop(0, tc, step, (h_ref[q, :, 0:half], h_ref[q, :, half:2 * half]), unroll=True)
        h_ref[q, :, 0:half] = hr
        h_ref[q, :, half:2 * half] = hi

    for q in range(jb):
        lanes = slice(q * LANES, (q + 1) * LANES)
        y2 = jnp.dot(bu_ref[q].astype(BF16), c_ref[q], preferred_element_type=F32)
        y_scr[q] = jnp.where(is_fwd, y2[:, 0:LANES], y2[:, LANES:2 * LANES])
        for b in range(nbat):
            yf_ref[b, :, lanes] = y_scr[q, pl.ds(b, tc, stride=8), :].astype(yf_ref.dtype)
            yb = y_scr[q, pl.ds(nbat + b, tc, stride=8), :].astype(BF16)
            yb_ref[b, :, lanes] = jnp.dot(flip, yb, preferred_element_type=F32).astype(yb_ref.dtype)


def s5_scan(st, p, width, b_bd, c_bd, lam8):
    bsz, t_len, l_len = st.b, st.t, st.l
    assert 2 * bsz == 8
    d = p.shape[1]
    nb = width // LANES
    jb = _pick(nb, (2, 1))
    tc = _pick(math.gcd(st.s, l_len), (128, 64, 32, 16, 8))
    rows = tc * 8
    ns = b_bd.shape[2]
    n_l, n_c = l_len // tc, t_len // tc

    def bwd_chunk(c):
        return jnp.where(c < n_l, n_l - 1 - c, n_c + n_l - 1 - c)

    p3 = p.reshape(bsz, t_len, d)
    out = jax.ShapeDtypeStruct((bsz, t_len, width), BF16)
    y_f, y_b = pl.pallas_call(
        functools.partial(_s5_kernel, tc=tc, jb=jb),
        grid=(nb // jb, n_c),
        in_specs=[pl.BlockSpec((bsz, tc, jb * LANES), lambda j, c: (0, c, j)),
                  pl.BlockSpec((bsz, tc, jb * LANES), lambda j, c: (0, bwd_chunk(c), j)),
                  pl.BlockSpec((jb, 2 * LANES, ns), lambda j, c: (j, 0, 0)),
                  pl.BlockSpec((jb, ns, 2 * LANES), lambda j, c: (j, 0, 0)),
                  pl.BlockSpec((jb, 2, 8, ns // 2), lambda j, c: (j, 0, 0, 0))],
        out_specs=[pl.BlockSpec((bsz, tc, jb * LANES), lambda j, c: (0, c, j)),
                   pl.BlockSpec((bsz, tc, jb * LANES), lambda j, c: (0, bwd_chunk(c), j))],
        out_shape=[out, out],
        scratch_shapes=[pltpu.VMEM((jb, 8, ns), F32), pltpu.VMEM((jb, rows, ns), F32),
                        pltpu.VMEM((jb, rows, LANES), F32), pltpu.VMEM((jb, rows, LANES), F32)],
        compiler_params=_cparams("arbitrary", "arbitrary"),
        name="s5_scan",
    )(p3, p3, b_bd, c_bd, lam8)
    return y_f.reshape(bsz * t_len, width), y_b.reshape(bsz * t_len, width)


def s5_tables(lam_re, lam_im, log_dt, b_re, b_im, c_re, c_im):
    _, g, p = lam_re.shape
    c = b_re.shape[-1]
    nb = g // S5_BLOCK_GROUPS
    gb = S5_BLOCK_GROUPS
    lam = lax.complex(lam_re.astype(F32), lam_im.astype(F32))
    dt = jnp.exp(log_dt.astype(F32))[..., None]
    lam_bar = jnp.exp(lam * dt)
    b_bar = ((lam_bar - 1.0) / lam)[..., None] * lax.complex(b_re.astype(F32), b_im.astype(F32))
    eye = jnp.eye(gb, dtype=F32)
    bb = b_bar.reshape(2, nb, gb, p, c)

    def b_lay(z):
        return jnp.einsum('djgpc,gh->jdgchp', z, eye).reshape(nb, 2 * gb * c, gb * p)

    b_bd = jnp.concatenate([b_lay(bb.real), b_lay(bb.imag)], axis=-1).astype(BF16)
    cc = lax.complex(c_re.astype(F32), c_im.astype(F32)).reshape(2, nb, gb, c, p)

    def c_lay(z):
        return jnp.einsum('djgcp,gh->jgpdhc', z, eye).reshape(nb, gb * p, 2 * gb * c)

    c_bd = jnp.concatenate([c_lay(cc.real), -c_lay(cc.imag)], axis=1).astype(BF16)
    lb = lam_bar.reshape(2, nb, gb * p)
    lam8 = jnp.stack([lb.real, lb.imag], axis=0)
    lam8 = jnp.repeat(lam8.transpose(2, 0, 1, 3), 4, axis=2)
    return b_bd, c_bd, lam8


def _fnet_chan_kernel(f_ref, cd_ref, sd_ref, w_ref, o_ref, wc_ref):
    nh, dh = w_ref.shape[0], w_ref.shape[1]

    @pl.when(pl.program_id(0) == 0)
    def _():
        for h in range(nh):
            wh = w_ref[h]
            wc_ref[h, :, 0:dh] = jnp.dot(cd_ref[...], wh, preferred_element_type=F32,
                                         precision=lax.Precision.HIGHEST).astype(BF16)
            wc_ref[h, :, dh:2 * dh] = jnp.dot(sd_ref[...], wh, preferred_element_type=F32,
                                              precision=lax.Precision.HIGHEST).astype(BF16)

    for h in range(nh):
        r = jnp.dot(f_ref[:, h * dh:(h + 1) * dh], wc_ref[h], preferred_element_type=F32)
        o_ref[:, h * dh:(h + 1) * dh] = r[:, 0:dh].astype(o_ref.dtype)
        o_ref[:, (nh + h) * dh:(nh + h + 1) * dh] = r[:, dh:2 * dh].astype(o_ref.dtype)


def fnet_chan(p, col_block, fnet_w, cd, sd):
    nt = p.shape[0]
    nh, dh, _ = fnet_w.shape
    width = nh * dh
    tm = _pick(nt, (512, 256, 128, 64, 32, 16, 8))
    return pl.pallas_call(
        _fnet_chan_kernel,
        grid=(nt // tm,),
        in_specs=[pl.BlockSpec((tm, width), lambda i: (i, col_block)),
                  pl.BlockSpec((dh, dh), lambda i: (0, 0)),
                  pl.BlockSpec((dh, dh), lambda i: (0, 0)),
                  pl.BlockSpec((nh, dh, dh), lambda i: (0, 0, 0))],
        out_specs=pl.BlockSpec((tm, 2 * width), lambda i: (i, 0)),
        out_shape=jax.ShapeDtypeStruct((nt, 2 * width), BF16),
        scratch_shapes=[pltpu.VMEM((nh, dh, 2 * dh), BF16)],
        compiler_params=_cparams("arbitrary"),
        name="fnet_chan",
    )(p, cd, sd, fnet_w)


def _fnet_seq_kernel(cl_ref, sl_ref, cs_ref, ss_ref, p_ref, o_ref, *, l_len):
    w = o_ref.shape[1]
    tm = o_ref.shape[0]
    t_len = p_ref.shape[0]
    in_ctx = pl.program_id(1) * tm < l_len

    @pl.when(in_ctx)
    def _():
        o_ref[...] = (jnp.dot(cl_ref[...], p_ref[0:l_len, 0:w], preferred_element_type=F32)
                      - jnp.dot(sl_ref[...], p_ref[0:l_len, w:2 * w], preferred_element_type=F32)).astype(o_ref.dtype)

    @pl.when(jnp.logical_not(in_ctx))
    def _():
        o_ref[...] = (jnp.dot(cs_ref[...], p_ref[l_len:t_len, 0:w], preferred_element_type=F32)
                      - jnp.dot(ss_ref[...], p_ref[l_len:t_len, w:2 * w], preferred_element_type=F32)).astype(o_ref.dtype)


def fnet_seq(st, pcs, tab_l, tab_s):
    w2 = pcs.shape[1]
    tm = st.tm
    tpb, lt = st.t // tm, st.l // tm
    ctx_tile = lambda b, i: (jnp.minimum(i, lt - 1), 0)
    lat_tile = lambda b, i: (jnp.maximum(i - lt, 0), 0)
    return pl.pallas_call(
        functools.partial(_fnet_seq_kernel, l_len=st.l),
        grid=(st.b, tpb),
        in_specs=[pl.BlockSpec((tm, st.l), ctx_tile), pl.BlockSpec((tm, st.l), ctx_tile),
                  pl.BlockSpec((tm, st.s), lat_tile), pl.BlockSpec((tm, st.s), lat_tile),
                  pl.BlockSpec((st.t, w2), lambda b, i: (b, 0))],
        out_specs=pl.BlockSpec((tm, w2 // 2), lambda b, i: (b * tpb + i, 0)),
        out_shape=jax.ShapeDtypeStruct((st.nt, w2 // 2), BF16),
        compiler_params=_cparams("arbitrary", "arbitrary"),
        name="fnet_seq",
    )(tab_l[0], tab_l[1], tab_s[0], tab_s[1], pcs)


def dft_tables(n):
    k = jnp.arange(n, dtype=jnp.int32)
    kn = (k[:, None] * k[None, :]) % n
    ang = kn.astype(F32) * (2.0 * math.pi / n)
    scale = 1.0 / math.sqrt(n)
    return jnp.cos(ang) * scale, jnp.sin(ang) * scale


def _pool_kernel(v_ref, w_ref, sc_ref, o_ref, *, windows, l_len):
    dh = w_ref.shape[1]
    for r0, r1 in ((0, l_len), (l_len, v_ref.shape[0])):
        t_len = r1 - r0
        t = lax.broadcasted_iota(jnp.int32, (t_len, dh), 0)
        for g, win in enumerate(windows):
            v = v_ref[r0:r1, g * dh:(g + 1) * dh].astype(F32)
            lo = win // 2
            acc = jnp.zeros_like(v)
            for j in range(-lo, win - lo):
                src = t + j
                shifted = v if j == 0 else pltpu.roll(v, (-j) % t_len, 0)
                acc = acc + jnp.where((src >= 0) & (src < t_len), shifted, 0.0)
            cnt = jnp.clip(t + (win - lo), 0, t_len) - jnp.clip(t - lo, 0, t_len)
            pooled = acc / cnt.astype(F32) - v
            y = jnp.dot(pooled.astype(BF16), w_ref[g].astype(BF16), preferred_element_type=F32)
            o_ref[r0:r1, g * dh:(g + 1) * dh] = (y * sc_ref[:, g * dh:(g + 1) * dh]).astype(o_ref.dtype)


def pool_mix(st, p, pool_w, pool_scale):
    ng, dh, _ = pool_w.shape
    width = ng * dh
    return pl.pallas_call(
        functools.partial(_pool_kernel, windows=POOL_WINDOWS, l_len=st.l),
        grid=(st.b,),
        in_specs=[pl.BlockSpec((st.t, width), lambda b: (b, 0)),
                  pl.BlockSpec((ng, dh, dh), lambda b: (0, 0, 0)),
                  pl.BlockSpec((1, width), lambda b: (0, 0))],
        out_specs=pl.BlockSpec((st.t, width), lambda b: (b, 0)),
        out_shape=jax.ShapeDtypeStruct((st.nt, width), BF16),
        compiler_params=_cparams("arbitrary"),
        name="pool_mix",
    )(p, pool_w, pool_scale.reshape(1, width))


def rope_tables(s, l_len):
    tm = l_len
    rows = s // GRID_W
    row = jnp.repeat(jnp.arange(rows), GRID_W)
    col = jnp.tile(jnp.arange(GRID_W), rows)
    nf = HEAD_DIM // 4
    inv = jnp.power(ROPE_THETA, -jnp.arange(nf, dtype=F32) / nf)
    a_row = row.astype(F32)[:, None] * inv
    a_col = col.astype(F32)[:, None] * inv
    ang = jnp.concatenate([a_row, a_row, a_col, a_col], axis=1)
    sign = jnp.tile(jnp.concatenate([-jnp.ones((nf,), F32), jnp.ones((nf,), F32)]), 2)
    cos_t = jnp.concatenate([jnp.ones((tm, HEAD_DIM), F32), jnp.cos(ang)], axis=0)
    sin_t = jnp.concatenate([jnp.zeros((tm, HEAD_DIM), F32), jnp.sin(ang) * sign], axis=0)
    return cos_t, sin_t


def _norm_rope(x, gain, cos, sin):
    lane = lax.broadcasted_iota(jnp.int32, x.shape, 1)
    first = (lane % (HEAD_DIM // 2)) < (HEAD_DIM // 4)
    n = x * lax.rsqrt(jnp.mean(x * x, axis=-1, keepdims=True) + EPS) * gain
    partner = jnp.where(first, pltpu.roll(n, HEAD_DIM - HEAD_DIM // 4, 1), pltpu.roll(n, HEAD_DIM // 4, 1))
    return n * cos + partner * sin


def _attn_kernel(*refs, q_per_kv, q_scale, l_len):
    q_refs = refs[:q_per_kv]
    kraw_ref, v_ref, g_ref, cq_ref, sq_ref, ck_ref, sk_ref, o_ref, k_scr = refs[q_per_kv:]
    tq = o_ref.shape[0]
    dn = (((1,), (1,)), ((), ()))

    @pl.when(pl.program_id(2) == 0)
    def _():
        k_scr[...] = _norm_rope(kraw_ref[...].astype(F32), g_ref[1], ck_ref[...], sk_ref[...]).astype(BF16)

    def run(n_keys):
        q_gain = g_ref[0] * q_scale
        qs = [_norm_rope(q_ref[...].astype(F32), q_gain, cq_ref[...], sq_ref[...]).astype(BF16) for q_ref in q_refs]
        k = k_scr[0:n_keys, :]
        v = v_ref[0:n_keys, :]
        for g in range(q_per_kv):
            q = qs[g]
            s_ = lax.dot_general(q, k, dn, preferred_element_type=F32)
            e = jnp.exp2(s_ - s_.max(axis=-1, keepdims=True))
            acc = jnp.dot(e.astype(BF16), v, preferred_element_type=F32)
            o_ref[:, g * HEAD_DIM:(g + 1) * HEAD_DIM] = (acc / e.sum(axis=-1, keepdims=True)).astype(o_ref.dtype)

    in_ctx = pl.program_id(2) * tq < l_len

    @pl.when(in_ctx)
    def _():
        run(l_len)

    @pl.when(jnp.logical_not(in_ctx))
    def _():
        run(k_scr.shape[0])


def attention(st, p, gains, cos_t, sin_t, q_col0, n_q_heads, q_scale):
    q_per_kv = n_q_heads // KV_HEADS
    qw = q_per_kv * HEAD_DIM
    tq = st.tm
    tpb = st.t // tq
    k_col0 = q_col0 + n_q_heads
    v_col0 = k_col0 + KV_HEADS
    q_specs = [pl.BlockSpec((tq, HEAD_DIM), lambda b, h, i, g=g: (b * tpb + i, q_col0 + h * q_per_kv + g))
               for g in range(q_per_kv)]
    tab_q = pl.BlockSpec((tq, HEAD_DIM), lambda b, h, i: (i, 0))
    tab_k = pl.BlockSpec((st.t, HEAD_DIM), lambda b, h, i: (0, 0))
    return pl.pallas_call(
        functools.partial(_attn_kernel, q_per_kv=q_per_kv, q_scale=q_scale, l_len=st.l),
        grid=(st.b, KV_HEADS, tpb),
        in_specs=q_specs + [pl.BlockSpec((st.t, HEAD_DIM), lambda b, h, i: (b, k_col0 + h)),
                            pl.BlockSpec((st.t, HEAD_DIM), lambda b, h, i: (b, v_col0 + h)),
                            pl.BlockSpec((2, 1, HEAD_DIM), lambda b, h, i: (0, 0, 0)),
                            tab_q, tab_q, tab_k, tab_k],
        out_specs=pl.BlockSpec((tq, qw), lambda b, h, i: (b * tpb + i, h)),
        out_shape=jax.ShapeDtypeStruct((st.nt, n_q_heads * HEAD_DIM), BF16),
        scratch_shapes=[pltpu.VMEM((st.t, HEAD_DIM), BF16)],
        compiler_params=_cparams("arbitrary", "arbitrary", "arbitrary"),
        name="attention",
    )(*([p] * q_per_kv), p, p, gains, cos_t, sin_t, cos_t, sin_t)


def _pack_pairs(lo, hi):
    lo_b = lax.bitcast_convert_type(lo.astype(BF16).astype(F32), jnp.uint32)
    hi_b = lax.bitcast_convert_type(hi.astype(BF16).astype(F32), jnp.uint32)
    return (lo_b >> 16) | (hi_b & jnp.uint32(0xFFFF0000))


def _unpack_pairs(word):
    lo = lax.bitcast_convert_type(word << 16, F32)
    hi = lax.bitcast_convert_type(word & jnp.uint32(0xFFFF0000), F32)
    return lo, hi


ROW_TILE = 8


def _store_row_tiles(ref, word):
    rows = word.shape[0]
    for j in range(ROW_TILE):
        ref[pl.ds(j, rows, stride=ROW_TILE), :] = word[:, j * LANES:(j + 1) * LANES]


def _load_row_tiles(ref, rows):
    return [ref[pl.ds(j, rows, stride=ROW_TILE), :] for j in range(ROW_TILE)]


def _router_kernel(x_ref, g_ref, sh_ref, sc_ref, wr_ref, br_ref, h_ref, idx_ref, wgt_ref):
    x = x_ref[...]
    y = x * lax.rsqrt(jnp.mean(x * x, axis=-1, keepdims=True) + EPS) * g_ref[...]
    h = y * (1.0 + sc_ref[...]) + sh_ref[...]
    half = h.shape[1] // 2
    _store_row_tiles(h_ref, _pack_pairs(h[:, 0:half], h[:, half:2 * half]))
    ne = wr_ref.shape[0]
    dn = (((1,), (1,)), ((), ()))
    h1 = h.astype(BF16)
    h2 = (h - h1.astype(F32)).astype(BF16)
    w = wr_ref[...]
    w1 = w.astype(BF16)
    w2 = (w - w1.astype(F32)).astype(BF16)
    r1 = lax.dot_general(jnp.concatenate([w1, w2], axis=0), h1, dn, preferred_element_type=F32)
    r2 = lax.dot_general(w1, h2, dn, preferred_element_type=F32)
    logits = r1[0:ne] + r1[ne:2 * ne] + r2 + br_ref[...]
    ne, tm = logits.shape
    eidx = lax.broadcasted_iota(jnp.int32, (ne, tm), 0)
    vals, idxs = [], []
    cur = logits
    for _ in range(TOP_K):
        m = cur.max(axis=0, keepdims=True)
        sel = jnp.min(jnp.where(cur == m, eidx, ne), axis=0, keepdims=True)
        vals.append(m)
        idxs.append(sel)
        cur = jnp.where(eidx == sel, -jnp.inf, cur)
    es = [jnp.exp(v - vals[0]) for v in vals]
    den = es[0]
    for e in es[1:]:
        den = den + e
    zi = jnp.zeros((8 - TOP_K, tm), jnp.int32)
    zf = jnp.zeros((8 - TOP_K, tm), F32)
    idx_ref[...] = jnp.concatenate(idxs + [zi], axis=0)
    wgt_ref[...] = jnp.concatenate([e / den for e in es] + [zf], axis=0)


def norm_router(st, x, g, mod, k_shift, k_scale, w_router, b_router, lat_only):
    d = x.shape[1]
    ne = w_router.shape[1]
    tm = _pick(st.tm, (256, 128))
    n_tiles, row_blk, mod_row = st.tiles(tm, lat_only)
    nt = n_tiles * tm
    mod_spec = lambda k: pl.BlockSpec((None, None, 1, d), lambda i: (k, mod_row(i), 0, 0))
    return pl.pallas_call(
        _router_kernel,
        grid=(n_tiles,),
        in_specs=[pl.BlockSpec((tm, d), lambda i: (row_blk(i), 0)),
                  pl.BlockSpec((1, d), lambda i: (0, 0)),
                  mod_spec(k_shift), mod_spec(k_scale),
                  pl.BlockSpec((ne, d), lambda i: (0, 0)),
                  pl.BlockSpec((ne, 1), lambda i: (0, 0))],
        out_specs=[pl.BlockSpec((tm * ROW_TILE, LANES), lambda i: (i, 0)),
                   pl.BlockSpec((8, tm), lambda i: (0, i)),
                   pl.BlockSpec((8, tm), lambda i: (0, i))],
        out_shape=[jax.ShapeDtypeStruct((nt * ROW_TILE, LANES), jnp.uint32),
                   jax.ShapeDtypeStruct((8, nt), jnp.int32),
                   jax.ShapeDtypeStruct((8, nt), F32)],
        compiler_params=_cparams("arbitrary"),
        name="norm_router",
    )(x, g.reshape(1, d), mod, mod, w_router.T, b_router.reshape(ne, 1))


def _route_plan_kernel(idx_ref, dest_ref, blk_ref, ends_ref, pre_ref, *, ne, tb, chunk):
    nt = idx_ref.shape[1]
    n_chunk = nt // chunk
    eidx = lax.broadcasted_iota(jnp.int32, (ne, chunk), 0)
    tri = (lax.broadcasted_iota(jnp.int32, (chunk, chunk), 0)
           < lax.broadcasted_iota(jnp.int32, (chunk, chunk), 1)).astype(BF16)

    def count(c, carry):
        c0 = pl.multiple_of(c * chunk, chunk)
        oh = jnp.zeros((ne, chunk), F32)
        for k in range(TOP_K):
            oh = oh + (eidx == idx_ref[k:k + 1, pl.ds(c0, chunk)]).astype(F32)
        pre_ref[:, pl.ds(c0, chunk)] = jnp.dot(oh.astype(BF16), tri, preferred_element_type=F32) + carry
        return carry + oh.sum(axis=1, keepdims=True)

    counts = lax.fori_loop(0, n_chunk, count, jnp.zeros((ne, 1), F32))
    padded = jnp.ceil(counts * (1.0 / tb)) * tb
    r_i = lax.broadcasted_iota(jnp.int32, (ne, ne), 0)
    c_i = lax.broadcasted_iota(jnp.int32, (ne, ne), 1)
    padded_row = jnp.sum(jnp.where(r_i == c_i, padded, 0.0), axis=0, keepdims=True)
    pstart = jnp.sum(jnp.where(c_i < r_i, padded_row, 0.0), axis=1, keepdims=True)
    pend = pstart + padded

    def place(c, carry):
        c0 = pl.multiple_of(c * chunk, chunk)
        base = pre_ref[:, pl.ds(c0, chunk)] + pstart
        rows = [jnp.sum(jnp.where(eidx == idx_ref[k:k + 1, pl.ds(c0, chunk)], base, 0.0), axis=0, keepdims=True)
                for k in range(TOP_K)]
        rows.append(jnp.zeros((8 - TOP_K, chunk), F32))
        dest_ref[:, pl.ds(c0, chunk)] = jnp.concatenate(rows, axis=0).astype(jnp.int32)
        return carry

    lax.fori_loop(0, n_chunk, place, 0)

    nbp = blk_ref.shape[1]
    starts = (lax.broadcasted_iota(jnp.int32, (ne, nbp), 1) * tb).astype(F32)
    blk_e = jnp.minimum(jnp.sum((pend <= starts).astype(F32), axis=0, keepdims=True), ne - 1.0)
    n_live = jnp.sum(padded, axis=0, keepdims=True) * (1.0 / tb)
    e_col = lax.broadcasted_iota(jnp.int32, (ne, nbp), 0).astype(F32)
    owns = jnp.where(padded > 0.0, 1.0, 0.0)
    later = jnp.where(e_col > blk_e, owns, 0.0)
    nxt = jnp.min(jnp.where(later > 0.0, e_col, float(ne)), axis=0, keepdims=True)
    nxt = jnp.where(nxt >= float(ne), -1.0, nxt)
    rank = jnp.sum(jnp.where(e_col < blk_e, owns, 0.0), axis=0, keepdims=True)
    slot = rank - 2.0 * jnp.floor(rank * 0.5)
    blk_ref[...] = jnp.concatenate([blk_e, jnp.broadcast_to(n_live, (1, nbp)), nxt, slot, jnp.zeros((4, nbp), F32)],
                                   axis=0).astype(jnp.int32)

    r_l = lax.broadcasted_iota(jnp.int32, (ne, LANES), 0)
    c_l = lax.broadcasted_iota(jnp.int32, (ne, LANES), 1)
    to_row = lambda col: jnp.sum(jnp.where(r_l == c_l, col, 0.0), axis=0, keepdims=True)
    ends_ref[...] = jnp.concatenate([to_row(pstart + counts), to_row(pend), jnp.zeros((6, LANES), F32)],
                                    axis=0).astype(jnp.int32)


def route_plan(idx8, ne, tb, n_blocks):
    nt = idx8.shape[1]
    chunk = _pick(nt, (256, 128))
    nbp = -(-n_blocks // LANES) * LANES
    return pl.pallas_call(
        functools.partial(_route_plan_kernel, ne=ne, tb=tb, chunk=chunk),
        out_shape=[jax.ShapeDtypeStruct((8, nt), jnp.int32), jax.ShapeDtypeStruct((8, nbp), jnp.int32),
                   jax.ShapeDtypeStruct((8, LANES), jnp.int32)],
        scratch_shapes=[pltpu.VMEM((ne, nt), F32)],
        compiler_params=pltpu.CompilerParams(vmem_limit_bytes=VMEM_LIMIT),
        name="route_plan",
    )(idx8)


def _dispatch_kernel(dest_ref, ends_ref, h_ref, xs_hbm, zrow, sem, *, rows, ne, nt):
    i = pl.program_id(0)
    base = i * rows

    @pl.when(i == 0)
    def _():
        zrow[...] = jnp.zeros_like(zrow)

        def per_expert(e, carry, waiting):
            lo, hi = ends_ref[0, e], ends_ref[1, e]

            def put(r, c):
                dst = xs_hbm.at[pl.ds(pl.multiple_of(r * ROW_TILE, ROW_TILE), ROW_TILE)]
                pltpu.make_async_copy(zrow.at[pl.ds(0, ROW_TILE)], dst, sem.at[1]).start()
                return c

            def done(r, c):
                pltpu.make_async_copy(zrow.at[pl.ds(0, ROW_TILE)], xs_hbm.at[pl.ds(0, ROW_TILE)], sem.at[1]).wait()
                return c

            lax.fori_loop(lo, hi, done if waiting else put, 0)
            return carry

        for waiting in (False, True):
            lax.fori_loop(0, ne, functools.partial(per_expert, waiting=waiting), 0)

        blk = zrow.shape[0]
        tb = blk // ROW_TILE
        first_dead = ends_ref[1, ne - 1] // tb
        n_blk = xs_hbm.shape[0] // blk

        def put_blk(j, c):
            pltpu.make_async_copy(zrow, xs_hbm.at[pl.ds(pl.multiple_of(j * blk, blk), blk)], sem.at[1]).start()
            return c

        def done_blk(j, c):
            pltpu.make_async_copy(zrow, xs_hbm.at[pl.ds(0, blk)], sem.at[1]).wait()
            return c

        lax.fori_loop(first_dead, n_blk, put_blk, 0)
        lax.fori_loop(first_dead, n_blk, done_blk, 0)

    def issue(r, carry):
        src = h_ref.at[pl.ds(pl.multiple_of(r * ROW_TILE, ROW_TILE), ROW_TILE)]
        for k in range(TOP_K):
            d = dest_ref[k * nt + base + r]
            dst = xs_hbm.at[pl.ds(pl.multiple_of(d * ROW_TILE, ROW_TILE), ROW_TILE)]
            pltpu.make_async_copy(src, dst, sem.at[0]).start(priority=k % 2)
        return carry

    lax.fori_loop(0, rows, issue, 0, unroll=4)
    for _ in range(TOP_K):
        pltpu.make_async_copy(h_ref, xs_hbm.at[pl.ds(0, rows * ROW_TILE)], sem.at[0]).wait()


def moe_dispatch(h_tiles, dest1, ends, cap, ne):
    nt = h_tiles.shape[0] // ROW_TILE
    rows = _pick(nt, (256, 128, 64, 32, 16, 8))
    return pl.pallas_call(
        functools.partial(_dispatch_kernel, rows=rows, ne=ne, nt=nt),
        grid=(nt // rows,),
        in_specs=[pl.BlockSpec(memory_space=pltpu.SMEM),
                  pl.BlockSpec(memory_space=pltpu.SMEM),
                  pl.BlockSpec((rows * ROW_TILE, LANES), lambda i: (i, 0))],
        out_specs=pl.BlockSpec(memory_space=pl.ANY),
        out_shape=jax.ShapeDtypeStruct((cap * ROW_TILE, LANES), h_tiles.dtype),
        scratch_shapes=[pltpu.VMEM((MOE_ROWS * ROW_TILE, LANES), h_tiles.dtype), pltpu.SemaphoreType.DMA((2,))],
        compiler_params=_cparams("arbitrary"),
        name="moe_dispatch",
    )(dest1, ends, h_tiles)


def _expert_kernel(be_ref, nb_ref, nxt_ref, slot_ref, x_ref, bg_ref, bu_ref, bd_ref, wg_hbm, wu_hbm, wd_hbm, o_ref,
                   wg_st, wu_st, wd_st, wg_bf, wu_bf, wd_bf, sem, *, layer):
    i = pl.program_id(0)
    e = be_ref[i]
    live = i < nb_ref[0]
    first = (i == 0) | (e != be_ref[jnp.maximum(i - 1, 0)])
    slot = slot_ref[i]
    half = wg_bf.shape[0] // 2

    def weight_copies(ex, s):
        return (pltpu.make_async_copy(wg_hbm.at[layer, ex], wg_st.at[s], sem.at[s, 0]),
                pltpu.make_async_copy(wu_hbm.at[layer, ex], wu_st.at[s], sem.at[s, 1]),
                pltpu.make_async_copy(wd_hbm.at[layer, ex], wd_st.at[s], sem.at[s, 2]))

    @pl.when(i == 0)
    def _():
        for cp in weight_copies(e, slot):
            cp.start(priority=1)

    @pl.when(live & first)
    def _():
        for cp in weight_copies(e, slot):
            cp.wait()

        @pl.when(nxt_ref[i] >= 0)
        def _():
            for cp in weight_copies(nxt_ref[i], 1 - slot):
                cp.start(priority=1)

        wg_bf[...] = wg_st[slot].astype(BF16)
        wu_bf[...] = wu_st[slot].astype(BF16)
        wd_bf[...] = wd_st[slot].astype(BF16)

    @pl.when(live)
    def _():
        tb = x_ref.shape[0] // ROW_TILE
        lo, hi = _unpack_pairs(jnp.concatenate(_load_row_tiles(x_ref, tb), axis=1))
        lo = lo.astype(BF16)
        hi = hi.astype(BF16)

        def proj(w_bf, b_ref):
            return (jnp.dot(lo, w_bf[0:half, :], preferred_element_type=F32)
                    + jnp.dot(hi, w_bf[half:2 * half, :], preferred_element_type=F32) + b_ref[...])

        g = jnp.minimum(proj(wg_bf, bg_ref), SWIGLU_LIMIT)
        u = jnp.clip(proj(wu_bf, bu_ref), -SWIGLU_LIMIT, SWIGLU_LIMIT)
        a = g * jax.nn.sigmoid(SWIGLU_ALPHA * g) * (u + 1.0)
        y = jnp.dot(a.astype(BF16), wd_bf[...], preferred_element_type=F32) + bd_ref[...]
        _store_row_tiles(o_ref, _pack_pairs(y[:, 0:half], y[:, half:2 * half]))

    @pl.when(jnp.logical_not(live))
    def _():
        o_ref[...] = jnp.zeros_like(o_ref)


def expert_ffn(x_sorted, blk, n_blocks, layer, w_gate, b_gate, w_up, b_up, w_down, b_down):
    depth, ne, d, f = w_gate.shape
    blk_rows = MOE_ROWS * ROW_TILE
    bias = lambda n: pl.BlockSpec((None, 1, n), lambda i, be, nb, nx, sl: (layer * ne + be[i], 0, 0))
    hbm = pl.BlockSpec(memory_space=pl.ANY)
    grid_spec = pltpu.PrefetchScalarGridSpec(
        num_scalar_prefetch=4,
        grid=(n_blocks,),
        in_specs=[pl.BlockSpec((blk_rows, LANES), lambda i, be, nb, nx, sl: (jnp.minimum(i, nb[0] - 1), 0)),
                  bias(f), bias(f), bias(d), hbm, hbm, hbm],
        out_specs=pl.BlockSpec((blk_rows, LANES), lambda i, be, nb, nx, sl: (i, 0)),
        scratch_shapes=[pltpu.VMEM((2, d, f), F32), pltpu.VMEM((2, d, f), F32), pltpu.VMEM((2, f, d), F32),
                        pltpu.VMEM((d, f), BF16), pltpu.VMEM((d, f), BF16), pltpu.VMEM((f, d), BF16),
                        pltpu.SemaphoreType.DMA((2, 3))],
    )
    return pl.pallas_call(
        functools.partial(_expert_kernel, layer=layer),
        grid_spec=grid_spec,
        out_shape=jax.ShapeDtypeStruct(x_sorted.shape, jnp.uint32),
        compiler_params=_cparams("arbitrary"),
        name="expert_ffn",
    )(blk[0, :n_blocks], blk[1, :1], blk[2, :n_blocks], blk[3, :n_blocks], x_sorted,
      b_gate.reshape(depth * ne, 1, f), b_up.reshape(depth * ne, 1, f), b_down.reshape(depth * ne, 1, d),
      w_gate, w_up, w_down)


def _combine_kernel(dest_ref, x_ref, w_ref, gate_ref, y_hbm, o_ref, buf, sem, *, tm, nt, fg_ref):
    i = pl.program_id(0)
    n = pl.num_programs(0)
    slot = i % 2
    half = ROW_TILE * LANES

    def fetch(tile, s):
        def issue(r, carry):
            for k in range(TOP_K):
                d = dest_ref[k * nt + tile * tm + r]
                src = y_hbm.at[pl.ds(pl.multiple_of(d * ROW_TILE, ROW_TILE), ROW_TILE)]
                dst = buf.at[s, k, pl.ds(pl.multiple_of(r * ROW_TILE, ROW_TILE), ROW_TILE)]
                pltpu.make_async_copy(src, dst, sem.at[s]).start(priority=k % 2)
            return carry
        lax.fori_loop(0, tm, issue, 0, unroll=4)

    @pl.when(i == 0)
    def _():
        fetch(0, 0)

    @pl.when(i + 1 < n)
    def _():
        fetch(i + 1, 1 - slot)

    for k in range(TOP_K):
        pltpu.make_async_copy(y_hbm.at[pl.ds(0, tm * ROW_TILE)], buf.at[slot, k], sem.at[slot]).wait()

    ws = [w_ref[:, k:k + 1] for k in range(TOP_K)]
    for j in range(ROW_TILE):
        acc_lo = None
        acc_hi = None
        for k in range(TOP_K):
            lo, hi = _unpack_pairs(buf[slot, k, pl.ds(j, tm, stride=ROW_TILE), :])
            acc_lo = ws[k] * lo if acc_lo is None else acc_lo + ws[k] * lo
            acc_hi = ws[k] * hi if acc_hi is None else acc_hi + ws[k] * hi
        c_lo = slice(j * LANES, (j + 1) * LANES)
        c_hi = slice(half + j * LANES, half + (j + 1) * LANES)
        o_ref[:, c_lo] = x_ref[:, c_lo] + gate_ref[:, c_lo] * acc_lo
        o_ref[:, c_hi] = x_ref[:, c_hi] + gate_ref[:, c_hi] * acc_hi
    if fg_ref is not None:
        y = o_ref[...]
        o_ref[...] = y * lax.rsqrt(jnp.mean(y * y, axis=-1, keepdims=True) + EPS) * fg_ref[...]


def _combine_final_kernel(dest_ref, x_ref, w_ref, gate_ref, fg_ref, y_hbm, o_ref, buf, sem, **kw):
    _combine_kernel(dest_ref, x_ref, w_ref, gate_ref, y_hbm, o_ref, buf, sem, fg_ref=fg_ref, **kw)


def moe_combine(st, x, y_sorted, dest1, w_tok, mod, k_gate, final_g=None):
    d = x.shape[1]
    assert d // 2 == ROW_TILE * LANES
    tm = _pick(st.tm, (128, 64, 32, 16, 8))
    lat_only = final_g is not None
    n_tiles, row_blk, mod_row = st.tiles(tm, lat_only)
    nt = n_tiles * tm
    in_specs = [pl.BlockSpec(memory_space=pltpu.SMEM),
                pl.BlockSpec((tm, d), lambda i: (row_blk(i), 0)),
                pl.BlockSpec((tm, TOP_K), lambda i: (i, 0)),
                pl.BlockSpec((None, None, 1, d), lambda i: (k_gate, mod_row(i), 0, 0))]
    args = [dest1, x, w_tok, mod]
    if lat_only:
        in_specs.append(pl.BlockSpec((1, d), lambda i: (0, 0)))
        args.append(final_g.reshape(1, d))
    in_specs.append(pl.BlockSpec(memory_space=pl.ANY))
    args.append(y_sorted)
    body = _combine_final_kernel if lat_only else functools.partial(_combine_kernel, fg_ref=None)
    return pl.pallas_call(
        functools.partial(body, tm=tm, nt=nt),
        grid=(n_tiles,),
        in_specs=in_specs,
        out_specs=pl.BlockSpec((tm, d), lambda i: (i, 0)),
        out_shape=jax.ShapeDtypeStruct((nt, d), F32),
        scratch_shapes=[pltpu.VMEM((2, TOP_K, tm * ROW_TILE, LANES), jnp.uint32), pltpu.SemaphoreType.DMA((2,))],
        input_output_aliases={} if lat_only else {1: 0},
        compiler_params=_cparams("arbitrary"),
        name="moe_combine",
    )(*args)


def moe_layer(st, x, g, mod, layer, w_router, b_router, w_gate, b_gate, w_up, b_up, w_down, b_down, final_g=None):
    ne = w_router.shape[1]
    tb = MOE_ROWS
    h_packed, idx8, wgt8 = norm_router(st, x, g, mod, 3, 4, w_router, b_router, final_g is not None)
    nt = idx8.shape[1]
    n_assign = nt * TOP_K
    cap = -(-(n_assign + ne * (tb - 1)) // tb) * tb
    n_blocks = cap // tb
    dest8, blk, ends = route_plan(idx8, ne, tb, n_blocks)
    dest1 = dest8[:TOP_K].reshape(-1)
    x_sorted = moe_dispatch(h_packed, dest1, ends, cap, ne)
    y_sorted = expert_ffn(x_sorted, blk, n_blocks, layer, w_gate, b_gate, w_up, b_up, w_down, b_down)
    return moe_combine(st, x, y_sorted, dest1, wgt8[:TOP_K].T, mod, 5, final_g)


def even_mixer(st, x, norm_g, mod, i, w_in, w_out, s5p, d_skip, glu_w, glu_b, fnet_w, tables):
    d = x.shape[1]
    fw = fnet_w.shape[0] * fnet_w.shape[1]
    sw = d - fw
    p = norm_matmul(st, x, norm_g, mod, 0, 1, w_in, i)
    b_bd, c_bd, lam8 = s5p
    y_f, y_b = s5_scan(st, p, sw, b_bd, c_bd, lam8)
    a = gelu_glu(st, y_f, y_b, p, d_skip, glu_w, i, glu_b)
    cd, sd, tab_s, tab_l = tables
    pcs = fnet_chan(p, sw // fw, fnet_w, cd, sd)
    fm = fnet_seq(st, pcs, tab_l, tab_s)
    return matmul([a, fm], w_out, i, mode="resid", extra=(x, mod, 2), st=st)


def odd_mixer(st, x, norm_g, mod, i, w_in, w_out, pool_w, pool_scale, q_gain, k_gain, rope):
    d = x.shape[1]
    pw = pool_w.shape[0] * pool_w.shape[1]
    n_q = (d - pw) // HEAD_DIM
    p = norm_matmul(st, x, norm_g, mod, 0, 1, w_in, i)
    cos_t, sin_t = rope
    gains = jnp.stack([q_gain, k_gain]).reshape(2, 1, HEAD_DIM)
    att = attention(st, p, gains, cos_t, sin_t, pw // HEAD_DIM, n_q, HEAD_DIM ** -0.5 * math.log2(math.e))
    pm = pool_mix(st, p, pool_w, pool_scale)
    return matmul([pm, att], w_out, i, mode="resid", extra=(x, mod, 2), st=st)


def kernel(x, c, ctx, c_ctx, ada_w, ada_b, norm_mix_g, norm_ffn_g, ev_w_in, ev_w_out, s5_lam_re, s5_lam_im, s5_log_dt, s5_b_re, s5_b_im, s5_c_re, s5_c_im, s5_d, s5_glu_w, s5_glu_b, fnet_w, od_w_in, od_w_out, pool_w, pool_scale, q_gain, k_gain, router_w, router_b, exp_w_gate, exp_b_gate, exp_w_up, exp_b_up, exp_w_down, exp_b_down, final_g):
    batch, seq, d = x.shape
    ctx_len = ctx.shape[1]
    depth = ada_w.shape[0]
    st = Stream(batch, seq, ctx_len)
    xs = jnp.concatenate([ctx, x], axis=1).reshape(st.nt, d)

    cond8 = jnp.concatenate([c, c_ctx[None], jnp.zeros((8 - batch - 1, d), F32)], axis=0)
    mod_all = adaln_all(cond8, ada_w, ada_b)
    mod_all = mod_all.reshape(depth, 8, 6, 1, d).transpose(0, 2, 1, 3, 4)

    dh = fnet_w.shape[2]
    cd, sd = dft_tables(dh)
    tab_s = tuple(t.astype(BF16) for t in dft_tables(seq))
    tab_l = tuple(t.astype(BF16) for t in dft_tables(ctx_len))
    tables = (cd, sd, tab_s, tab_l)
    rope = rope_tables(seq, ctx_len)

    for layer in range(depth):
        i = layer // 2
        mod = mod_all[layer]
        if layer % 2 == 0:
            s5p = s5_tables(s5_lam_re[i], s5_lam_im[i], s5_log_dt[i], s5_b_re[i], s5_b_im[i], s5_c_re[i], s5_c_im[i])
            xs = even_mixer(st, xs, norm_mix_g[layer], mod, i, ev_w_in, ev_w_out, s5p, s5_d[i], s5_glu_w,
                            s5_glu_b[i], fnet_w[i], tables)
        else:
            xs = odd_mixer(st, xs, norm_mix_g[layer], mod, i, od_w_in, od_w_out, pool_w[i], pool_scale[i],
                           q_gain[i], k_gain[i], rope)
        xs = moe_layer(st, xs, norm_ffn_g[layer], mod, layer, router_w[layer], router_b[layer],
                       exp_w_gate, exp_b_gate, exp_w_up, exp_b_up, exp_w_down, exp_b_down,
                       final_g if layer == depth - 1 else None)
    return xs.reshape(batch, seq, d)
```

```python
import functools
import math

import jax
import jax.numpy as jnp
from jax import lax
from jax.experimental import pallas as pl
from jax.experimental.pallas import tpu as pltpu

F32 = jnp.float32
BF16 = jnp.bfloat16
EPS = 1e-6

GRID_W = 64
FNET_HEADS = 4
S5_GROUP = 16
S5_STATE = 64
POOL_WINDOWS = (2, 4, 8, 16)
HEAD_DIM = 128
KV_HEADS = 4
ROPE_THETA = 10000.0
TOP_K = 4
SWIGLU_LIMIT = 7.0
SWIGLU_ALPHA = 1.702

LANES = 128
S5_BLOCK_GROUPS = LANES // S5_GROUP
VMEM_LIMIT = 48 * 1024 * 1024
MOE_ROWS = 256


def _cparams(*sem):
    return pltpu.CompilerParams(dimension_semantics=sem, vmem_limit_bytes=VMEM_LIMIT)


def _pick(n, prefs):
    for p in prefs:
        if n % p == 0:
            return p
    return n


def _adaln_kernel(c_ref, w_ref, b_ref, o_ref):
    c = c_ref[...]
    a = (c * jax.nn.sigmoid(c)).astype(BF16)
    o_ref[...] = jnp.dot(a, w_ref[...].astype(BF16), preferred_element_type=F32) + b_ref[...]


def adaln_all(cond8, ada_w, ada_b):
    depth, d, n = ada_w.shape
    tn = _pick(n, (1024, 512, 256, 128))
    return pl.pallas_call(
        _adaln_kernel,
        grid=(depth, n // tn),
        in_specs=[pl.BlockSpec((8, d), lambda l, j: (0, 0)),
                  pl.BlockSpec((None, d, tn), lambda l, j: (l, 0, j)),
                  pl.BlockSpec((None, 1, tn), lambda l, j: (l, 0, j))],
        out_specs=pl.BlockSpec((None, 8, tn), lambda l, j: (l, 0, j)),
        out_shape=jax.ShapeDtypeStruct((depth, 8, n), F32),
        compiler_params=_cparams("arbitrary", "arbitrary"),
        name="adaln",
    )(cond8, ada_w, ada_b.reshape(depth, 1, n))


class Stream:
    def __init__(self, batch, seq, ctx_len):
        self.b, self.s, self.l = batch, seq, ctx_len
        self.t = seq + ctx_len
        self.nt = batch * self.t
        self.tm = _pick(math.gcd(seq, ctx_len), (256, 128, 64, 32, 16))

    def tiles(self, tm, lat_only):
        if not lat_only:
            return self.nt // tm, (lambda i: i), self.mod_row(tm)
        tpb, lt, ns = self.t // tm, self.l // tm, self.s // tm
        return self.b * ns, (lambda i: (i // ns) * tpb + lt + i % ns), (lambda i: i // ns)

    def mod_row(self, tm):
        tpb, l, b = self.t // tm, self.l, self.b
        return lambda i: jnp.where((i % tpb) * tm < l, b, i // tpb)


def _mod_spec(st, tm, k, d, grid_rank, row_axis):
    mr = st.mod_row(tm)
    if grid_rank == 1:
        return pl.BlockSpec((None, None, 1, d), lambda i: (k, mr(i), 0, 0))
    if row_axis == 1:
        return pl.BlockSpec((None, None, 1, d), lambda j, i: (k, mr(i), 0, j))
    raise ValueError


def _mm_kernel(*refs, n_a, k_offs, mode):
    a_refs = refs[:n_a]
    w_ref = refs[n_a]
    rest = refs[n_a + 1:]
    wbf_ref = rest[-1]
    o_ref = rest[-2]

    @pl.when(pl.program_id(1) == 0)
    def _():
        wbf_ref[...] = w_ref[...].astype(BF16)

    acc = None
    for a_ref, (k0, k1) in zip(a_refs, k_offs):
        part = jnp.dot(a_ref[...], wbf_ref[k0:k1, :], preferred_element_type=F32)
        acc = part if acc is None else acc + part
    if mode == "plain":
        o_ref[...] = acc.astype(o_ref.dtype)
    elif mode == "resid":
        x_ref, gate_ref = rest[0], rest[1]
        o_ref[...] = x_ref[...] + gate_ref[...] * acc
    elif mode == "glu":
        g_ref, b_ref = rest[0], rest[1]
        o_ref[...] = (g_ref[...].astype(F32) * jax.nn.sigmoid(acc + b_ref[...])).astype(o_ref.dtype)
    else:
        raise ValueError(mode)


def _norm_mm_kernel(x_ref, g_ref, sh_ref, sc_ref, w_ref, o_ref, wbf_ref):
    @pl.when(pl.program_id(1) == 0)
    def _():
        wbf_ref[...] = w_ref[...].astype(BF16)

    x = x_ref[...]
    y = x * lax.rsqrt(jnp.mean(x * x, axis=-1, keepdims=True) + EPS) * g_ref[...]
    h = (y * (1.0 + sc_ref[...]) + sh_ref[...]).astype(BF16)
    o_ref[...] = jnp.dot(h, wbf_ref[...], preferred_element_type=F32).astype(o_ref.dtype)


def norm_matmul(st, x, g, mod, k_shift, k_scale, w_stack, w_idx, tn_prefs=(2048, 1536, 1024, 512, 256, 128)):
    m, k = x.shape
    n = w_stack.shape[2]
    tm = st.tm
    tn = _pick(n, tn_prefs)
    mr = st.mod_row(tm)
    mod_spec = lambda kk: pl.BlockSpec((None, None, 1, k), lambda j, i: (kk, mr(i), 0, 0))
    return pl.pallas_call(
        _norm_mm_kernel,
        grid=(n // tn, m // tm),
        in_specs=[pl.BlockSpec((tm, k), lambda j, i: (i, 0)),
                  pl.BlockSpec((1, k), lambda j, i: (0, 0)),
                  mod_spec(k_shift), mod_spec(k_scale),
                  pl.BlockSpec((None, k, tn), lambda j, i: (w_idx, 0, j), pipeline_mode=pl.Buffered(1))],
        out_specs=pl.BlockSpec((tm, tn), lambda j, i: (i, j)),
        out_shape=jax.ShapeDtypeStruct((m, n), BF16),
        scratch_shapes=[pltpu.VMEM((k, tn), BF16)],
        compiler_params=_cparams("arbitrary", "arbitrary"),
        name="norm_mm",
    )(x, g.reshape(1, k), mod, mod, w_stack)


def _gelu_glu_kernel(yf_ref, yb_ref, u_ref, d_ref, w_ref, b_ref, o_ref, wbf_ref):
    @pl.when(pl.program_id(0) == 0)
    def _():
        wbf_ref[...] = w_ref[...].astype(BF16)

    y = yf_ref[...].astype(F32) + yb_ref[...].astype(F32) + d_ref[...] * u_ref[...].astype(F32)
    g = jax.nn.gelu(y).astype(BF16)
    acc = jnp.dot(g, wbf_ref[...], preferred_element_type=F32)
    o_ref[...] = (g.astype(F32) * jax.nn.sigmoid(acc + b_ref[...])).astype(o_ref.dtype)


def gelu_glu(st, y_f, y_b, p, d_skip, w_stack, w_idx, bias):
    m, width = y_f.shape
    tm = st.tm
    row = pl.BlockSpec((tm, width), lambda i: (i, 0))
    vec = pl.BlockSpec((1, width), lambda i: (0, 0))
    return pl.pallas_call(
        _gelu_glu_kernel,
        grid=(m // tm,),
        in_specs=[row, row, row, vec,
                  pl.BlockSpec((None, width, width), lambda i: (w_idx, 0, 0), pipeline_mode=pl.Buffered(1)),
                  vec],
        out_specs=row,
        out_shape=jax.ShapeDtypeStruct((m, width), BF16),
        scratch_shapes=[pltpu.VMEM((width, width), BF16)],
        compiler_params=_cparams("arbitrary"),
        name="gelu_glu",
    )(y_f, y_b, p, d_skip.reshape(1, width), w_stack, bias.reshape(1, width))


def matmul(a_parts, w_stack, w_idx, mode="plain", out_dtype=BF16, extra=(), st=None,
           tn_prefs=(2048, 1536, 1024, 512, 256, 128)):
    m = a_parts[0].shape[0]
    _, k, n = w_stack.shape
    tm = _pick(m if st is None else st.tm, (512, 256, 128, 64, 32, 16, 8))
    tn = _pick(n, tn_prefs)
    k_offs, off = [], 0
    in_specs, args = [], []
    for a in a_parts:
        ka = a.shape[1]
        k_offs.append((off, off + ka))
        off += ka
        in_specs.append(pl.BlockSpec((tm, ka), lambda j, i: (i, 0)))
        args.append(a)
    assert off == k
    in_specs.append(pl.BlockSpec((None, k, tn), lambda j, i: (w_idx, 0, j), pipeline_mode=pl.Buffered(1)))
    args.append(w_stack)
    io_alias = {}
    if mode == "resid":
        x, mod, k_gate = extra
        in_specs.append(pl.BlockSpec((tm, tn), lambda j, i: (i, j)))
        args.append(x)
        in_specs.append(_mod_spec(st, tm, k_gate, tn, 2, 1))
        args.append(mod)
        io_alias = {len(args) - 2: 0}
        out_dtype = F32
    elif mode == "glu":
        g, bias = extra
        in_specs.append(pl.BlockSpec((tm, tn), lambda j, i: (i, j)))
        args.append(g)
        in_specs.append(pl.BlockSpec((1, tn), lambda j, i: (0, j)))
        args.append(bias.reshape(1, n))
    return pl.pallas_call(
        functools.partial(_mm_kernel, n_a=len(a_parts), k_offs=tuple(k_offs), mode=mode),
        grid=(n // tn, m // tm),
        in_specs=in_specs,
        out_specs=pl.BlockSpec((tm, tn), lambda j, i: (i, j)),
        out_shape=jax.ShapeDtypeStruct((m, n), out_dtype),
        scratch_shapes=[pltpu.VMEM((k, tn), BF16)],
        input_output_aliases=io_alias,
        compiler_params=_cparams("arbitrary", "arbitrary"),
        name="mm_" + mode,
    )(*args)


def _s5_kernel(uf_ref, ub_ref, b_ref, c_ref, lam_ref, yf_ref, yb_ref, h_ref, bu_ref, u_scr, y_scr, *, tc, jb):
    half = b_ref.shape[2] // 2
    nbat = uf_ref.shape[0]

    @pl.when(pl.program_id(1) == 0)
    def _():
        h_ref[...] = jnp.zeros_like(h_ref)

    rows = tc * 8
    is_fwd = (lax.broadcasted_iota(jnp.int32, (rows, LANES), 0) % 8) < 4
    flip = (lax.broadcasted_iota(jnp.int32, (tc, tc), 0)
            + lax.broadcasted_iota(jnp.int32, (tc, tc), 1) == tc - 1).astype(BF16)
    for q in range(jb):
        lanes = slice(q * LANES, (q + 1) * LANES)
        for b in range(nbat):
            u_scr[q, pl.ds(b, tc, stride=8), :] = uf_ref[b, :, lanes].astype(F32)
            u_scr[q, pl.ds(nbat + b, tc, stride=8), :] = jnp.dot(flip, ub_ref[b, :, lanes],
                                                                 preferred_element_type=F32)
        u = u_scr[q]
        lhs = jnp.concatenate([jnp.where(is_fwd, u, 0.0), jnp.where(is_fwd, 0.0, u)], axis=1).astype(BF16)
        bu_ref[q] = jnp.dot(lhs, b_ref[q], preferred_element_type=F32)

    for q in range(jb):
        lam_r = lam_ref[q, 0]
        lam_i = lam_ref[q, 1]

        def step(t, carry, q=q, lam_r=lam_r, lam_i=lam_i):
            hr, hi = carry
            r0 = pl.multiple_of(t * 8, 8)
            nhr = lam_r * hr - lam_i * hi + bu_ref[q, pl.ds(r0, 8), 0:half]
            nhi = lam_r * hi + lam_i * hr + bu_ref[q, pl.ds(r0, 8), half:2 * half]
            bu_ref[q, pl.ds(r0, 8), 0:half] = nhr
            bu_ref[q, pl.ds(r0, 8), half:2 * half] = nhi
            return nhr, nhi

        hr, hi = lax.fori_loop(0, tc, step, (h_ref[q, :, 0:half], h_ref[q, :, half:2 * half]), unroll=True)
        h_ref[q, :, 0:half] = hr
        h_ref[q, :, half:2 * half] = hi

    for q in range(jb):
        lanes = slice(q * LANES, (q + 1) * LANES)
        y2 = jnp.dot(bu_ref[q].astype(BF16), c_ref[q], preferred_element_type=F32)
        y_scr[q] = jnp.where(is_fwd, y2[:, 0:LANES], y2[:, LANES:2 * LANES])
        for b in range(nbat):
            yf_ref[b, :, lanes] = y_scr[q, pl.ds(b, tc, stride=8), :].astype(yf_ref.dtype)
            yb = y_scr[q, pl.ds(nbat + b, tc, stride=8), :].astype(BF16)
            yb_ref[b, :, lanes] = jnp.dot(flip, yb, preferred_element_type=F32).astype(yb_ref.dtype)


def s5_scan(st, p, width, b_bd, c_bd, lam8):
    bsz, t_len, l_len = st.b, st.t, st.l
    assert 2 * bsz == 8
    d = p.shape[1]
    nb = width // LANES
    jb = _pick(nb, (2, 1))
    tc = _pick(math.gcd(st.s, l_len), (128, 64, 32, 16, 8))
    rows = tc * 8
    ns = b_bd.shape[2]
    n_l, n_c = l_len // tc, t_len // tc

    def bwd_chunk(c):
        return jnp.where(c < n_l, n_l - 1 - c, n_c + n_l - 1 - c)

    p3 = p.reshape(bsz, t_len, d)
    out = jax.ShapeDtypeStruct((bsz, t_len, width), BF16)
    y_f, y_b = pl.pallas_call(
        functools.partial(_s5_kernel, tc=tc, jb=jb),
        grid=(nb // jb, n_c),
        in_specs=[pl.BlockSpec((bsz, tc, jb * LANES), lambda j, c: (0, c, j)),
                  pl.BlockSpec((bsz, tc, jb * LANES), lambda j, c: (0, bwd_chunk(c), j)),
                  pl.BlockSpec((jb, 2 * LANES, ns), lambda j, c: (j, 0, 0)),
                  pl.BlockSpec((jb, ns, 2 * LANES), lambda j, c: (j, 0, 0)),
                  pl.BlockSpec((jb, 2, 8, ns // 2), lambda j, c: (j, 0, 0, 0))],
        out_specs=[pl.BlockSpec((bsz, tc, jb * LANES), lambda j, c: (0, c, j)),
                   pl.BlockSpec((bsz, tc, jb * LANES), lambda j, c: (0, bwd_chunk(c), j))],
        out_shape=[out, out],
        scratch_shapes=[pltpu.VMEM((jb, 8, ns), F32), pltpu.VMEM((jb, rows, ns), F32),
                        pltpu.VMEM((jb, rows, LANES), F32), pltpu.VMEM((jb, rows, LANES), F32)],
        compiler_params=_cparams("arbitrary", "arbitrary"),
        name="s5_scan",
    )(p3, p3, b_bd, c_bd, lam8)
    return y_f.reshape(bsz * t_len, width), y_b.reshape(bsz * t_len, width)


def s5_tables(lam_re, lam_im, log_dt, b_re, b_im, c_re, c_im):
    _, g, p = lam_re.shape
    c = b_re.shape[-1]
    nb = g // S5_BLOCK_GROUPS
    gb = S5_BLOCK_GROUPS
    lam = lax.complex(lam_re.astype(F32), lam_im.astype(F32))
    dt = jnp.exp(log_dt.astype(F32))[..., None]
    lam_bar = jnp.exp(lam * dt)
    b_bar = ((lam_bar - 1.0) / lam)[..., None] * lax.complex(b_re.astype(F32), b_im.astype(F32))
    eye = jnp.eye(gb, dtype=F32)
    bb = b_bar.reshape(2, nb, gb, p, c)

    def b_lay(z):
        return jnp.einsum('djgpc,gh->jdgchp', z, eye).reshape(nb, 2 * gb * c, gb * p)

    b_bd = jnp.concatenate([b_lay(bb.real), b_lay(bb.imag)], axis=-1).astype(BF16)
    cc = lax.complex(c_re.astype(F32), c_im.astype(F32)).reshape(2, nb, gb, c, p)

    def c_lay(z):
        return jnp.einsum('djgcp,gh->jgpdhc', z, eye).reshape(nb, gb * p, 2 * gb * c)

    c_bd = jnp.concatenate([c_lay(cc.real), -c_lay(cc.imag)], axis=1).astype(BF16)
    lb = lam_bar.reshape(2, nb, gb * p)
    lam8 = jnp.stack([lb.real, lb.imag], axis=0)
    lam8 = jnp.repeat(lam8.transpose(2, 0, 1, 3), 4, axis=2)
    return b_bd, c_bd, lam8


def _fnet_chan_kernel(f_ref, cd_ref, sd_ref, w_ref, o_ref, wc_ref):
    nh, dh = w_ref.shape[0], w_ref.shape[1]

    @pl.when(pl.program_id(0) == 0)
    def _():
        for h in range(nh):
            wh = w_ref[h]
            wc_ref[h, :, 0:dh] = jnp.dot(cd_ref[...], wh, preferred_element_type=F32,
                                         precision=lax.Precision.HIGHEST).astype(BF16)
            wc_ref[h, :, dh:2 * dh] = jnp.dot(sd_ref[...], wh, preferred_element_type=F32,
                                              precision=lax.Precision.HIGHEST).astype(BF16)

    for h in range(nh):
        r = jnp.dot(f_ref[:, h * dh:(h + 1) * dh], wc_ref[h], preferred_element_type=F32)
        o_ref[:, h * dh:(h + 1) * dh] = r[:, 0:dh].astype(o_ref.dtype)
        o_ref[:, (nh + h) * dh:(nh + h + 1) * dh] = r[:, dh:2 * dh].astype(o_ref.dtype)


def fnet_chan(p, col_block, fnet_w, cd, sd):
    nt = p.shape[0]
    nh, dh, _ = fnet_w.shape
    width = nh * dh
    tm = _pick(nt, (512, 256, 128, 64, 32, 16, 8))
    return pl.pallas_call(
        _fnet_chan_kernel,
        grid=(nt // tm,),
        in_specs=[pl.BlockSpec((tm, width), lambda i: (i, col_block)),
                  pl.BlockSpec((dh, dh), lambda i: (0, 0)),
                  pl.BlockSpec((dh, dh), lambda i: (0, 0)),
                  pl.BlockSpec((nh, dh, dh), lambda i: (0, 0, 0))],
        out_specs=pl.BlockSpec((tm, 2 * width), lambda i: (i, 0)),
        out_shape=jax.ShapeDtypeStruct((nt, 2 * width), BF16),
        scratch_shapes=[pltpu.VMEM((nh, dh, 2 * dh), BF16)],
        compiler_params=_cparams("arbitrary"),
        name="fnet_chan",
    )(p, cd, sd, fnet_w)


def _fnet_seq_kernel(cl_ref, sl_ref, cs_ref, ss_ref, p_ref, o_ref, *, l_len):
    w = o_ref.shape[1]
    tm = o_ref.shape[0]
    t_len = p_ref.shape[0]
    in_ctx = pl.program_id(1) * tm < l_len

    @pl.when(in_ctx)
    def _():
        o_ref[...] = (jnp.dot(cl_ref[...], p_ref[0:l_len, 0:w], preferred_element_type=F32)
                      - jnp.dot(sl_ref[...], p_ref[0:l_len, w:2 * w], preferred_element_type=F32)).astype(o_ref.dtype)

    @pl.when(jnp.logical_not(in_ctx))
    def _():
        o_ref[...] = (jnp.dot(cs_ref[...], p_ref[l_len:t_len, 0:w], preferred_element_type=F32)
                      - jnp.dot(ss_ref[...], p_ref[l_len:t_len, w:2 * w], preferred_element_type=F32)).astype(o_ref.dtype)


def fnet_seq(st, pcs, tab_l, tab_s):
    w2 = pcs.shape[1]
    tm = st.tm
    tpb, lt = st.t // tm, st.l // tm
    ctx_tile = lambda b, i: (jnp.minimum(i, lt - 1), 0)
    lat_tile = lambda b, i: (jnp.maximum(i - lt, 0), 0)
    return pl.pallas_call(
        functools.partial(_fnet_seq_kernel, l_len=st.l),
        grid=(st.b, tpb),
        in_specs=[pl.BlockSpec((tm, st.l), ctx_tile), pl.BlockSpec((tm, st.l), ctx_tile),
                  pl.BlockSpec((tm, st.s), lat_tile), pl.BlockSpec((tm, st.s), lat_tile),
                  pl.BlockSpec((st.t, w2), lambda b, i: (b, 0))],
        out_specs=pl.BlockSpec((tm, w2 // 2), lambda b, i: (b * tpb + i, 0)),
        out_shape=jax.ShapeDtypeStruct((st.nt, w2 // 2), BF16),
        compiler_params=_cparams("arbitrary", "arbitrary"),
        name="fnet_seq",
    )(tab_l[0], tab_l[1], tab_s[0], tab_s[1], pcs)


def dft_tables(n):
    scale = 1.0 / math.sqrt(n)
    step = 2.0 * math.pi / n
    t = jnp.arange(n, dtype=jnp.int32)
    split = 64
    if n % split or n <= split:
        ang = ((t[:, None] * t[None, :]) % n).astype(F32) * step
        return jnp.cos(ang) * scale, jnp.sin(ang) * scale
    a = jnp.arange(n // split, dtype=jnp.int32) * split
    b = jnp.arange(split, dtype=jnp.int32)
    ang_a = ((a[:, None] * t[None, :]) % n).astype(F32) * step
    ang_b = ((b[:, None] * t[None, :]) % n).astype(F32) * step
    ca, sa = jnp.cos(ang_a)[:, None, :], jnp.sin(ang_a)[:, None, :]
    cb, sb = jnp.cos(ang_b)[None, :, :] * scale, jnp.sin(ang_b)[None, :, :] * scale
    return (ca * cb - sa * sb).reshape(n, n), (sa * cb + ca * sb).reshape(n, n)


def _pool_kernel(v_ref, w_ref, sc_ref, o_ref, *, windows, l_len):
    dh = w_ref.shape[1]
    for r0, r1 in ((0, l_len), (l_len, v_ref.shape[0])):
        t_len = r1 - r0
        t = lax.broadcasted_iota(jnp.int32, (t_len, dh), 0)
        for g, win in enumerate(windows):
            v = v_ref[r0:r1, g * dh:(g + 1) * dh].astype(F32)
            lo = win // 2
            acc = jnp.zeros_like(v)
            for j in range(-lo, win - lo):
                src = t + j
                shifted = v if j == 0 else pltpu.roll(v, (-j) % t_len, 0)
                acc = acc + jnp.where((src >= 0) & (src < t_len), shifted, 0.0)
            cnt = jnp.clip(t + (win - lo), 0, t_len) - jnp.clip(t - lo, 0, t_len)
            pooled = acc / cnt.astype(F32) - v
            y = jnp.dot(pooled.astype(BF16), w_ref[g].astype(BF16), preferred_element_type=F32)
            o_ref[r0:r1, g * dh:(g + 1) * dh] = (y * sc_ref[:, g * dh:(g + 1) * dh]).astype(o_ref.dtype)


def pool_mix(st, p, pool_w, pool_scale):
    ng, dh, _ = pool_w.shape
    width = ng * dh
    return pl.pallas_call(
        functools.partial(_pool_kernel, windows=POOL_WINDOWS, l_len=st.l),
        grid=(st.b,),
        in_specs=[pl.BlockSpec((st.t, width), lambda b: (b, 0)),
                  pl.BlockSpec((ng, dh, dh), lambda b: (0, 0, 0)),
                  pl.BlockSpec((1, width), lambda b: (0, 0))],
        out_specs=pl.BlockSpec((st.t, width), lambda b: (b, 0)),
        out_shape=jax.ShapeDtypeStruct((st.nt, width), BF16),
        compiler_params=_cparams("arbitrary"),
        name="pool_mix",
    )(p, pool_w, pool_scale.reshape(1, width))


def rope_tables(s, l_len):
    tm = l_len
    rows = s // GRID_W
    row = jnp.repeat(jnp.arange(rows), GRID_W)
    col = jnp.tile(jnp.arange(GRID_W), rows)
    nf = HEAD_DIM // 4
    inv = jnp.power(ROPE_THETA, -jnp.arange(nf, dtype=F32) / nf)
    a_row = row.astype(F32)[:, None] * inv
    a_col = col.astype(F32)[:, None] * inv
    ang = jnp.concatenate([a_row, a_row, a_col, a_col], axis=1)
    sign = jnp.tile(jnp.concatenate([-jnp.ones((nf,), F32), jnp.ones((nf,), F32)]), 2)
    cos_t = jnp.concatenate([jnp.ones((tm, HEAD_DIM), F32), jnp.cos(ang)], axis=0)
    sin_t = jnp.concatenate([jnp.zeros((tm, HEAD_DIM), F32), jnp.sin(ang) * sign], axis=0)
    return cos_t, sin_t


def _norm_rope(x, gain, cos, sin):
    lane = lax.broadcasted_iota(jnp.int32, x.shape, 1)
    first = (lane % (HEAD_DIM // 2)) < (HEAD_DIM // 4)
    n = x * lax.rsqrt(jnp.mean(x * x, axis=-1, keepdims=True) + EPS) * gain
    partner = jnp.where(first, pltpu.roll(n, HEAD_DIM - HEAD_DIM // 4, 1), pltpu.roll(n, HEAD_DIM // 4, 1))
    return n * cos + partner * sin


def _attn_kernel(*refs, q_per_kv, q_scale, l_len):
    q_refs = refs[:q_per_kv]
    kraw_ref, v_ref, g_ref, cq_ref, sq_ref, ck_ref, sk_ref, o_ref, k_scr = refs[q_per_kv:]
    tq = o_ref.shape[0]
    dn = (((1,), (1,)), ((), ()))

    @pl.when(pl.program_id(2) == 0)
    def _():
        k_scr[...] = _norm_rope(kraw_ref[...].astype(F32), g_ref[1], ck_ref[...], sk_ref[...]).astype(BF16)

    def run(n_keys):
        q_gain = g_ref[0] * q_scale
        qs = [_norm_rope(q_ref[...].astype(F32), q_gain, cq_ref[...], sq_ref[...]).astype(BF16) for q_ref in q_refs]
        k = k_scr[0:n_keys, :]
        v = v_ref[0:n_keys, :]
        for g in range(q_per_kv):
            q = qs[g]
            s_ = lax.dot_general(q, k, dn, preferred_element_type=F32)
            e = jnp.exp2(s_ - s_.max(axis=-1, keepdims=True))
            acc = jnp.dot(e.astype(BF16), v, preferred_element_type=F32)
            o_ref[:, g * HEAD_DIM:(g + 1) * HEAD_DIM] = (acc / e.sum(axis=-1, keepdims=True)).astype(o_ref.dtype)

    in_ctx = pl.program_id(2) * tq < l_len

    @pl.when(in_ctx)
    def _():
        run(l_len)

    @pl.when(jnp.logical_not(in_ctx))
    def _():
        run(k_scr.shape[0])


def attention(st, p, gains, cos_t, sin_t, q_col0, n_q_heads, q_scale):
    q_per_kv = n_q_heads // KV_HEADS
    qw = q_per_kv * HEAD_DIM
    tq = st.tm
    tpb = st.t // tq
    k_col0 = q_col0 + n_q_heads
    v_col0 = k_col0 + KV_HEADS
    q_specs = [pl.BlockSpec((tq, HEAD_DIM), lambda b, h, i, g=g: (b * tpb + i, q_col0 + h * q_per_kv + g))
               for g in range(q_per_kv)]
    tab_q = pl.BlockSpec((tq, HEAD_DIM), lambda b, h, i: (i, 0))
    tab_k = pl.BlockSpec((st.t, HEAD_DIM), lambda b, h, i: (0, 0))
    return pl.pallas_call(
        functools.partial(_attn_kernel, q_per_kv=q_per_kv, q_scale=q_scale, l_len=st.l),
        grid=(st.b, KV_HEADS, tpb),
        in_specs=q_specs + [pl.BlockSpec((st.t, HEAD_DIM), lambda b, h, i: (b, k_col0 + h)),
                            pl.BlockSpec((st.t, HEAD_DIM), lambda b, h, i: (b, v_col0 + h)),
                            pl.BlockSpec((2, 1, HEAD_DIM), lambda b, h, i: (0, 0, 0)),
                            tab_q, tab_q, tab_k, tab_k],
        out_specs=pl.BlockSpec((tq, qw), lambda b, h, i: (b * tpb + i, h)),
        out_shape=jax.ShapeDtypeStruct((st.nt, n_q_heads * HEAD_DIM), BF16),
        scratch_shapes=[pltpu.VMEM((st.t, HEAD_DIM), BF16)],
        compiler_params=_cparams("arbitrary", "arbitrary", "arbitrary"),
        name="attention",
    )(*([p] * q_per_kv), p, p, gains, cos_t, sin_t, cos_t, sin_t)


def _pack_pairs(lo, hi):
    lo_b = lax.bitcast_convert_type(lo.astype(BF16).astype(F32), jnp.uint32)
    hi_b = lax.bitcast_convert_type(hi.astype(BF16).astype(F32), jnp.uint32)
    return (lo_b >> 16) | (hi_b & jnp.uint32(0xFFFF0000))


def _unpack_pairs(word):
    lo = lax.bitcast_convert_type(word << 16, F32)
    hi = lax.bitcast_convert_type(word & jnp.uint32(0xFFFF0000), F32)
    return lo, hi


ROW_TILE = 8


def _store_row_tiles(ref, word):
    rows = word.shape[0]
    for j in range(ROW_TILE):
        ref[pl.ds(j, rows, stride=ROW_TILE), :] = word[:, j * LANES:(j + 1) * LANES]


def _load_row_tiles(ref, rows):
    return [ref[pl.ds(j, rows, stride=ROW_TILE), :] for j in range(ROW_TILE)]


def _router_kernel(x_ref, g_ref, sh_ref, sc_ref, wr_ref, br_ref, h_ref, idx_ref, wgt_ref):
    x = x_ref[...]
    y = x * lax.rsqrt(jnp.mean(x * x, axis=-1, keepdims=True) + EPS) * g_ref[...]
    h = y * (1.0 + sc_ref[...]) + sh_ref[...]
    half = h.shape[1] // 2
    _store_row_tiles(h_ref, _pack_pairs(h[:, 0:half], h[:, half:2 * half]))
    ne = wr_ref.shape[0]
    dn = (((1,), (1,)), ((), ()))
    h1 = h.astype(BF16)
    h2 = (h - h1.astype(F32)).astype(BF16)
    w = wr_ref[...]
    w1 = w.astype(BF16)
    w2 = (w - w1.astype(F32)).astype(BF16)
    r1 = lax.dot_general(jnp.concatenate([w1, w2], axis=0), h1, dn, preferred_element_type=F32)
    r2 = lax.dot_general(w1, h2, dn, preferred_element_type=F32)
    logits = r1[0:ne] + r1[ne:2 * ne] + r2 + br_ref[...]
    ne, tm = logits.shape
    eidx = lax.broadcasted_iota(jnp.int32, (ne, tm), 0)
    vals, idxs = [], []
    cur = logits
    for _ in range(TOP_K):
        m = cur.max(axis=0, keepdims=True)
        sel = jnp.min(jnp.where(cur == m, eidx, ne), axis=0, keepdims=True)
        vals.append(m)
        idxs.append(sel)
        cur = jnp.where(eidx == sel, -jnp.inf, cur)
    es = [jnp.exp(v - vals[0]) for v in vals]
    den = es[0]
    for e in es[1:]:
        den = den + e
    zi = jnp.zeros((8 - TOP_K, tm), jnp.int32)
    zf = jnp.zeros((8 - TOP_K, tm), F32)
    idx_ref[...] = jnp.concatenate(idxs + [zi], axis=0)
    wgt_ref[...] = jnp.concatenate([e / den for e in es] + [zf], axis=0)


def norm_router(st, x, g, mod, k_shift, k_scale, w_router, b_router, lat_only):
    d = x.shape[1]
    ne = w_router.shape[1]
    tm = _pick(st.tm, (256, 128))
    n_tiles, row_blk, mod_row = st.tiles(tm, lat_only)
    nt = n_tiles * tm
    mod_spec = lambda k: pl.BlockSpec((None, None, 1, d), lambda i: (k, mod_row(i), 0, 0))
    return pl.pallas_call(
        _router_kernel,
        grid=(n_tiles,),
        in_specs=[pl.BlockSpec((tm, d), lambda i: (row_blk(i), 0)),
                  pl.BlockSpec((1, d), lambda i: (0, 0)),
                  mod_spec(k_shift), mod_spec(k_scale),
                  pl.BlockSpec((ne, d), lambda i: (0, 0)),
                  pl.BlockSpec((ne, 1), lambda i: (0, 0))],
        out_specs=[pl.BlockSpec((tm * ROW_TILE, LANES), lambda i: (i, 0)),
                   pl.BlockSpec((8, tm), lambda i: (0, i)),
                   pl.BlockSpec((8, tm), lambda i: (0, i))],
        out_shape=[jax.ShapeDtypeStruct((nt * ROW_TILE, LANES), jnp.uint32),
                   jax.ShapeDtypeStruct((8, nt), jnp.int32),
                   jax.ShapeDtypeStruct((8, nt), F32)],
        compiler_params=_cparams("arbitrary"),
        name="norm_router",
    )(x, g.reshape(1, d), mod, mod, w_router.T, b_router.reshape(ne, 1))


def _route_plan_kernel(idx_ref, dest_ref, blk_ref, ends_ref, pre_ref, *, ne, tb, chunk):
    nt = idx_ref.shape[1]
    n_chunk = nt // chunk
    eidx = lax.broadcasted_iota(jnp.int32, (ne, chunk), 0)
    tri = (lax.broadcasted_iota(jnp.int32, (chunk, chunk), 0)
           < lax.broadcasted_iota(jnp.int32, (chunk, chunk), 1)).astype(BF16)

    def count(c, carry):
        c0 = pl.multiple_of(c * chunk, chunk)
        oh = jnp.zeros((ne, chunk), F32)
        for k in range(TOP_K):
            oh = oh + (eidx == idx_ref[k:k + 1, pl.ds(c0, chunk)]).astype(F32)
        pre_ref[:, pl.ds(c0, chunk)] = jnp.dot(oh.astype(BF16), tri, preferred_element_type=F32) + carry
        return carry + oh.sum(axis=1, keepdims=True)

    counts = lax.fori_loop(0, n_chunk, count, jnp.zeros((ne, 1), F32))
    padded = jnp.ceil(counts * (1.0 / tb)) * tb
    r_i = lax.broadcasted_iota(jnp.int32, (ne, ne), 0)
    c_i = lax.broadcasted_iota(jnp.int32, (ne, ne), 1)
    padded_row = jnp.sum(jnp.where(r_i == c_i, padded, 0.0), axis=0, keepdims=True)
    pstart = jnp.sum(jnp.where(c_i < r_i, padded_row, 0.0), axis=1, keepdims=True)
    pend = pstart + padded

    def place(c, carry):
        c0 = pl.multiple_of(c * chunk, chunk)
        base = pre_ref[:, pl.ds(c0, chunk)] + pstart
        for k in range(TOP_K):
            row = jnp.sum(jnp.where(eidx == idx_ref[k:k + 1, pl.ds(c0, chunk)], base, 0.0), axis=0, keepdims=True)
            row = row.astype(jnp.int32)
            for hh in range(chunk // LANES):
                r = k * (nt // LANES) + c * (chunk // LANES) + hh
                dest_ref[pl.ds(r, 1), :] = row[:, hh * LANES:(hh + 1) * LANES]
        return carry

    lax.fori_loop(0, n_chunk, place, 0)

    nbp = blk_ref.shape[1]
    starts = (lax.broadcasted_iota(jnp.int32, (ne, nbp), 1) * tb).astype(F32)
    blk_e = jnp.minimum(jnp.sum((pend <= starts).astype(F32), axis=0, keepdims=True), ne - 1.0)
    n_live = jnp.sum(padded, axis=0, keepdims=True) * (1.0 / tb)
    e_col = lax.broadcasted_iota(jnp.int32, (ne, nbp), 0).astype(F32)
    owns = jnp.where(padded > 0.0, 1.0, 0.0)
    later = jnp.where(e_col > blk_e, owns, 0.0)
    nxt = jnp.min(jnp.where(later > 0.0, e_col, float(ne)), axis=0, keepdims=True)
    nxt = jnp.where(nxt >= float(ne), -1.0, nxt)
    rank = jnp.sum(jnp.where(e_col < blk_e, owns, 0.0), axis=0, keepdims=True)
    slot = rank - 2.0 * jnp.floor(rank * 0.5)
    blk_ref[...] = jnp.concatenate([blk_e, jnp.broadcast_to(n_live, (1, nbp)), nxt, slot, jnp.zeros((4, nbp), F32)],
                                   axis=0).astype(jnp.int32)

    r_l = lax.broadcasted_iota(jnp.int32, (ne, LANES), 0)
    c_l = lax.broadcasted_iota(jnp.int32, (ne, LANES), 1)
    to_row = lambda col: jnp.sum(jnp.where(r_l == c_l, col, 0.0), axis=0, keepdims=True)
    ends_ref[...] = jnp.concatenate([to_row(pstart + counts), to_row(pend), jnp.zeros((6, LANES), F32)],
                                    axis=0).astype(jnp.int32)


def route_plan(idx8, ne, tb, n_blocks):
    nt = idx8.shape[1]
    chunk = _pick(nt, (256, 128))
    nbp = -(-n_blocks // LANES) * LANES
    return pl.pallas_call(
        functools.partial(_route_plan_kernel, ne=ne, tb=tb, chunk=chunk),
        out_shape=[jax.ShapeDtypeStruct((TOP_K * nt // LANES, LANES), jnp.int32),
                   jax.ShapeDtypeStruct((8, nbp), jnp.int32),
                   jax.ShapeDtypeStruct((8, LANES), jnp.int32)],
        scratch_shapes=[pltpu.VMEM((ne, nt), F32)],
        compiler_params=pltpu.CompilerParams(vmem_limit_bytes=VMEM_LIMIT),
        name="route_plan",
    )(idx8)


def _dispatch_kernel(dest_ref, ends_ref, h_ref, xs_hbm, zrow, sem, *, rows, ne, nt):
    i = pl.program_id(0)
    base = i * rows

    @pl.when(i == 0)
    def _():
        zrow[...] = jnp.zeros_like(zrow)

        def per_expert(e, carry, waiting):
            lo, hi = ends_ref[0, e], ends_ref[1, e]

            def put(r, c):
                dst = xs_hbm.at[pl.ds(pl.multiple_of(r * ROW_TILE, ROW_TILE), ROW_TILE)]
                pltpu.make_async_copy(zrow.at[pl.ds(0, ROW_TILE)], dst, sem.at[1]).start()
                return c

            def done(r, c):
                pltpu.make_async_copy(zrow.at[pl.ds(0, ROW_TILE)], xs_hbm.at[pl.ds(0, ROW_TILE)], sem.at[1]).wait()
                return c

            lax.fori_loop(lo, hi, done if waiting else put, 0)
            return carry

        for waiting in (False, True):
            lax.fori_loop(0, ne, functools.partial(per_expert, waiting=waiting), 0)

        blk = zrow.shape[0]
        tb = blk // ROW_TILE
        first_dead = ends_ref[1, ne - 1] // tb
        n_blk = xs_hbm.shape[0] // blk

        def put_blk(j, c):
            pltpu.make_async_copy(zrow, xs_hbm.at[pl.ds(pl.multiple_of(j * blk, blk), blk)], sem.at[1]).start()
            return c

        def done_blk(j, c):
            pltpu.make_async_copy(zrow, xs_hbm.at[pl.ds(0, blk)], sem.at[1]).wait()
            return c

        lax.fori_loop(first_dead, n_blk, put_blk, 0)
        lax.fori_loop(first_dead, n_blk, done_blk, 0)

    def issue(r, carry):
        src = h_ref.at[pl.ds(pl.multiple_of(r * ROW_TILE, ROW_TILE), ROW_TILE)]
        for k in range(TOP_K):
            d = dest_ref[k * nt + base + r]
            dst = xs_hbm.at[pl.ds(pl.multiple_of(d * ROW_TILE, ROW_TILE), ROW_TILE)]
            pltpu.make_async_copy(src, dst, sem.at[0]).start(priority=k % 2)
        return carry

    lax.fori_loop(0, rows, issue, 0, unroll=4)
    for _ in range(TOP_K):
        pltpu.make_async_copy(h_ref, xs_hbm.at[pl.ds(0, rows * ROW_TILE)], sem.at[0]).wait()


def moe_dispatch(h_tiles, dest1, ends, cap, ne):
    nt = h_tiles.shape[0] // ROW_TILE
    rows = _pick(nt, (256, 128, 64, 32, 16, 8))
    return pl.pallas_call(
        functools.partial(_dispatch_kernel, rows=rows, ne=ne, nt=nt),
        grid=(nt // rows,),
        in_specs=[pl.BlockSpec(memory_space=pltpu.SMEM),
                  pl.BlockSpec(memory_space=pltpu.SMEM),
                  pl.BlockSpec((rows * ROW_TILE, LANES), lambda i: (i, 0))],
        out_specs=pl.BlockSpec(memory_space=pl.ANY),
        out_shape=jax.ShapeDtypeStruct((cap * ROW_TILE, LANES), h_tiles.dtype),
        scratch_shapes=[pltpu.VMEM((MOE_ROWS * ROW_TILE, LANES), h_tiles.dtype), pltpu.SemaphoreType.DMA((2,))],
        compiler_params=_cparams("arbitrary"),
        name="moe_dispatch",
    )(dest1, ends, h_tiles)


def _expert_kernel(be_ref, nb_ref, nxt_ref, slot_ref, x_ref, bg_ref, bu_ref, bd_ref, wg_hbm, wu_hbm, wd_hbm, o_ref,
                   wg_st, wu_st, wd_st, wg_bf, wu_bf, wd_bf, sem, *, layer):
    i = pl.program_id(0)
    e = be_ref[i]
    live = i < nb_ref[0]
    first = (i == 0) | (e != be_ref[jnp.maximum(i - 1, 0)])
    slot = slot_ref[i]
    half = wg_bf.shape[0] // 2

    def weight_copies(ex, s):
        return (pltpu.make_async_copy(wg_hbm.at[layer, ex], wg_st.at[s], sem.at[s, 0]),
                pltpu.make_async_copy(wu_hbm.at[layer, ex], wu_st.at[s], sem.at[s, 1]),
                pltpu.make_async_copy(wd_hbm.at[layer, ex], wd_st.at[s], sem.at[s, 2]))

    @pl.when(i == 0)
    def _():
        for cp in weight_copies(e, slot):
            cp.start(priority=1)

    @pl.when(live & first)
    def _():
        for cp in weight_copies(e, slot):
            cp.wait()

        @pl.when(nxt_ref[i] >= 0)
        def _():
            for cp in weight_copies(nxt_ref[i], 1 - slot):
                cp.start(priority=1)

        wg_bf[...] = wg_st[slot].astype(BF16)
        wu_bf[...] = wu_st[slot].astype(BF16)
        wd_bf[...] = wd_st[slot].astype(BF16)

    @pl.when(live)
    def _():
        tb = x_ref.shape[0] // ROW_TILE
        lo, hi = _unpack_pairs(jnp.concatenate(_load_row_tiles(x_ref, tb), axis=1))
        lo = lo.astype(BF16)
        hi = hi.astype(BF16)

        def proj(w_bf, b_ref):
            return (jnp.dot(lo, w_bf[0:half, :], preferred_element_type=F32)
                    + jnp.dot(hi, w_bf[half:2 * half, :], preferred_element_type=F32) + b_ref[...])

        g = jnp.minimum(proj(wg_bf, bg_ref), SWIGLU_LIMIT)
        u = jnp.clip(proj(wu_bf, bu_ref), -SWIGLU_LIMIT, SWIGLU_LIMIT)
        a = g * jax.nn.sigmoid(SWIGLU_ALPHA * g) * (u + 1.0)
        y = jnp.dot(a.astype(BF16), wd_bf[...], preferred_element_type=F32) + bd_ref[...]
        _store_row_tiles(o_ref, _pack_pairs(y[:, 0:half], y[:, half:2 * half]))

    @pl.when(jnp.logical_not(live))
    def _():
        o_ref[...] = jnp.zeros_like(o_ref)


def expert_ffn(x_sorted, blk, n_blocks, layer, w_gate, b_gate, w_up, b_up, w_down, b_down):
    depth, ne, d, f = w_gate.shape
    blk_rows = MOE_ROWS * ROW_TILE
    bias = lambda n: pl.BlockSpec((None, 1, n), lambda i, be, nb, nx, sl: (layer * ne + be[i], 0, 0))
    hbm = pl.BlockSpec(memory_space=pl.ANY)
    grid_spec = pltpu.PrefetchScalarGridSpec(
        num_scalar_prefetch=4,
        grid=(n_blocks,),
        in_specs=[pl.BlockSpec((blk_rows, LANES), lambda i, be, nb, nx, sl: (jnp.minimum(i, nb[0] - 1), 0)),
                  bias(f), bias(f), bias(d), hbm, hbm, hbm],
        out_specs=pl.BlockSpec((blk_rows, LANES), lambda i, be, nb, nx, sl: (i, 0)),
        scratch_shapes=[pltpu.VMEM((2, d, f), F32), pltpu.VMEM((2, d, f), F32), pltpu.VMEM((2, f, d), F32),
                        pltpu.VMEM((d, f), BF16), pltpu.VMEM((d, f), BF16), pltpu.VMEM((f, d), BF16),
                        pltpu.SemaphoreType.DMA((2, 3))],
    )
    return pl.pallas_call(
        functools.partial(_expert_kernel, layer=layer),
        grid_spec=grid_spec,
        out_shape=jax.ShapeDtypeStruct(x_sorted.shape, jnp.uint32),
        compiler_params=_cparams("arbitrary"),
        name="expert_ffn",
    )(blk[0, :n_blocks], blk[1, :1], blk[2, :n_blocks], blk[3, :n_blocks], x_sorted,
      b_gate.reshape(depth * ne, 1, f), b_up.reshape(depth * ne, 1, f), b_down.reshape(depth * ne, 1, d),
      w_gate, w_up, w_down)


def _combine_kernel(dest_ref, x_ref, w_ref, gate_ref, y_hbm, o_ref, buf, sem, *, tm, nt, fg_ref):
    i = pl.program_id(0)
    n = pl.num_programs(0)
    slot = i % 2
    half = ROW_TILE * LANES

    def fetch(tile, s):
        def issue(r, carry):
            for k in range(TOP_K):
                d = dest_ref[k * nt + tile * tm + r]
                src = y_hbm.at[pl.ds(pl.multiple_of(d * ROW_TILE, ROW_TILE), ROW_TILE)]
                dst = buf.at[s, k, pl.ds(pl.multiple_of(r * ROW_TILE, ROW_TILE), ROW_TILE)]
                pltpu.make_async_copy(src, dst, sem.at[s]).start(priority=k % 2)
            return carry
        lax.fori_loop(0, tm, issue, 0, unroll=4)

    @pl.when(i == 0)
    def _():
        fetch(0, 0)

    @pl.when(i + 1 < n)
    def _():
        fetch(i + 1, 1 - slot)

    for k in range(TOP_K):
        pltpu.make_async_copy(y_hbm.at[pl.ds(0, tm * ROW_TILE)], buf.at[slot, k], sem.at[slot]).wait()

    ws = [w_ref[:, k:k + 1] for k in range(TOP_K)]
    for j in range(ROW_TILE):
        acc_lo = None
        acc_hi = None
        for k in range(TOP_K):
            lo, hi = _unpack_pairs(buf[slot, k, pl.ds(j, tm, stride=ROW_TILE), :])
            acc_lo = ws[k] * lo if acc_lo is None else acc_lo + ws[k] * lo
            acc_hi = ws[k] * hi if acc_hi is None else acc_hi + ws[k] * hi
        c_lo = slice(j * LANES, (j + 1) * LANES)
        c_hi = slice(half + j * LANES, half + (j + 1) * LANES)
        o_ref[:, c_lo] = x_ref[:, c_lo] + gate_ref[:, c_lo] * acc_lo
        o_ref[:, c_hi] = x_ref[:, c_hi] + gate_ref[:, c_hi] * acc_hi
    if fg_ref is not None:
        y = o_ref[...]
        o_ref[...] = y * lax.rsqrt(jnp.mean(y * y, axis=-1, keepdims=True) + EPS) * fg_ref[...]


def _combine_final_kernel(dest_ref, x_ref, w_ref, gate_ref, fg_ref, y_hbm, o_ref, buf, sem, **kw):
    _combine_kernel(dest_ref, x_ref, w_ref, gate_ref, y_hbm, o_ref, buf, sem, fg_ref=fg_ref, **kw)


def moe_combine(st, x, y_sorted, dest1, w_tok, mod, k_gate, final_g=None):
    d = x.shape[1]
    assert d // 2 == ROW_TILE * LANES
    tm = _pick(st.tm, (128, 64, 32, 16, 8))
    lat_only = final_g is not None
    n_tiles, row_blk, mod_row = st.tiles(tm, lat_only)
    nt = n_tiles * tm
    in_specs = [pl.BlockSpec(memory_space=pltpu.SMEM),
                pl.BlockSpec((tm, d), lambda i: (row_blk(i), 0)),
                pl.BlockSpec((tm, TOP_K), lambda i: (i, 0)),
                pl.BlockSpec((None, None, 1, d), lambda i: (k_gate, mod_row(i), 0, 0))]
    args = [dest1, x, w_tok, mod]
    if lat_only:
        in_specs.append(pl.BlockSpec((1, d), lambda i: (0, 0)))
        args.append(final_g.reshape(1, d))
    in_specs.append(pl.BlockSpec(memory_space=pl.ANY))
    args.append(y_sorted)
    body = _combine_final_kernel if lat_only else functools.partial(_combine_kernel, fg_ref=None)
    return pl.pallas_call(
        functools.partial(body, tm=tm, nt=nt),
        grid=(n_tiles,),
        in_specs=in_specs,
        out_specs=pl.BlockSpec((tm, d), lambda i: (i, 0)),
        out_shape=jax.ShapeDtypeStruct((nt, d), F32),
        scratch_shapes=[pltpu.VMEM((2, TOP_K, tm * ROW_TILE, LANES), jnp.uint32), pltpu.SemaphoreType.DMA((2,))],
        input_output_aliases={} if lat_only else {1: 0},
        compiler_params=_cparams("arbitrary"),
        name="moe_combine",
    )(*args)


def moe_layer(st, x, g, mod, layer, w_router, b_router, w_gate, b_gate, w_up, b_up, w_down, b_down, final_g=None):
    ne = w_router.shape[1]
    tb = MOE_ROWS
    h_packed, idx8, wgt8 = norm_router(st, x, g, mod, 3, 4, w_router, b_router, final_g is not None)
    nt = idx8.shape[1]
    n_assign = nt * TOP_K
    cap = -(-(n_assign + ne * (tb - 1)) // tb) * tb
    n_blocks = cap // tb
    dest2, blk, ends = route_plan(idx8, ne, tb, n_blocks)
    dest1 = dest2.reshape(-1)
    x_sorted = moe_dispatch(h_packed, dest1, ends, cap, ne)
    y_sorted = expert_ffn(x_sorted, blk, n_blocks, layer, w_gate, b_gate, w_up, b_up, w_down, b_down)
    return moe_combine(st, x, y_sorted, dest1, wgt8[:TOP_K].T, mod, 5, final_g)


def even_mixer(st, x, norm_g, mod, i, w_in, w_out, s5p, d_skip, glu_w, glu_b, fnet_w, tables):
    d = x.shape[1]
    fw = fnet_w.shape[0] * fnet_w.shape[1]
    sw = d - fw
    p = norm_matmul(st, x, norm_g, mod, 0, 1, w_in, i)
    b_bd, c_bd, lam8 = s5p
    y_f, y_b = s5_scan(st, p, sw, b_bd, c_bd, lam8)
    a = gelu_glu(st, y_f, y_b, p, d_skip, glu_w, i, glu_b)
    cd, sd, tab_s, tab_l = tables
    pcs = fnet_chan(p, sw // fw, fnet_w, cd, sd)
    fm = fnet_seq(st, pcs, tab_l, tab_s)
    return matmul([a, fm], w_out, i, mode="resid", extra=(x, mod, 2), st=st)


def odd_mixer(st, x, norm_g, mod, i, w_in, w_out, pool_w, pool_scale, q_gain, k_gain, rope):
    d = x.shape[1]
    pw = pool_w.shape[0] * pool_w.shape[1]
    n_q = (d - pw) // HEAD_DIM
    p = norm_matmul(st, x, norm_g, mod, 0, 1, w_in, i)
    cos_t, sin_t = rope
    gains = jnp.stack([q_gain, k_gain]).reshape(2, 1, HEAD_DIM)
    att = attention(st, p, gains, cos_t, sin_t, pw // HEAD_DIM, n_q, HEAD_DIM ** -0.5 * math.log2(math.e))
    pm = pool_mix(st, p, pool_w, pool_scale)
    return matmul([pm, att], w_out, i, mode="resid", extra=(x, mod, 2), st=st)


def kernel(x, c, ctx, c_ctx, ada_w, ada_b, norm_mix_g, norm_ffn_g, ev_w_in, ev_w_out, s5_lam_re, s5_lam_im, s5_log_dt, s5_b_re, s5_b_im, s5_c_re, s5_c_im, s5_d, s5_glu_w, s5_glu_b, fnet_w, od_w_in, od_w_out, pool_w, pool_scale, q_gain, k_gain, router_w, router_b, exp_w_gate, exp_b_gate, exp_w_up, exp_b_up, exp_w_down, exp_b_down, final_g):
    batch, seq, d = x.shape
    ctx_len = ctx.shape[1]
    depth = ada_w.shape[0]
    st = Stream(batch, seq, ctx_len)
    xs = jnp.concatenate([ctx, x], axis=1).reshape(st.nt, d)

    cond8 = jnp.concatenate([c, c_ctx[None], jnp.zeros((8 - batch - 1, d), F32)], axis=0)
    mod_all = adaln_all(cond8, ada_w, ada_b)
    mod_all = mod_all.reshape(depth, 8, 6, 1, d).transpose(0, 2, 1, 3, 4)

    dh = fnet_w.shape[2]
    cd, sd = dft_tables(dh)
    tab_s = tuple(t.astype(BF16) for t in dft_tables(seq))
    tab_l = tuple(t.astype(BF16) for t in dft_tables(ctx_len))
    tables = (cd, sd, tab_s, tab_l)
    rope = rope_tables(seq, ctx_len)

    for layer in range(depth):
        i = layer // 2
        mod = mod_all[layer]
        if layer % 2 == 0:
            s5p = s5_tables(s5_lam_re[i], s5_lam_im[i], s5_log_dt[i], s5_b_re[i], s5_b_im[i], s5_c_re[i], s5_c_im[i])
            xs = even_mixer(st, xs, norm_mix_g[layer], mod, i, ev_w_in, ev_w_out, s5p, s5_d[i], s5_glu_w,
                            s5_glu_b[i], fnet_w[i], tables)
        else:
            xs = odd_mixer(st, xs, norm_mix_g[layer], mod, i, od_w_in, od_w_out, pool_w[i], pool_scale[i],
                           q_gain[i], k_gain[i], rope)
        xs = moe_layer(st, xs, norm_ffn_g[layer], mod, layer, router_w[layer], router_b[layer],
                       exp_w_gate, exp_b_gate, exp_w_up, exp_b_up, exp_w_down, exp_b_down,
                       final_g if layer == depth - 1 else None)
    return xs.reshape(batch, seq, d)
```

```python
import functools
import math

import jax
import jax.numpy as jnp
from jax import lax
from jax.experimental import pallas as pl
from jax.experimental.pallas import tpu as pltpu

F32 = jnp.float32
BF16 = jnp.bfloat16
EPS = 1e-6

GRID_W = 64
FNET_HEADS = 4
S5_GROUP = 16
S5_STATE = 64
POOL_WINDOWS = (2, 4, 8, 16)
HEAD_DIM = 128
KV_HEADS = 4
ROPE_THETA = 10000.0
TOP_K = 4
SWIGLU_LIMIT = 7.0
SWIGLU_ALPHA = 1.702

LANES = 128
S5_BLOCK_GROUPS = LANES // S5_GROUP
VMEM_LIMIT = 48 * 1024 * 1024
MOE_ROWS = 256


def _cparams(*sem):
    return pltpu.CompilerParams(dimension_semantics=sem, vmem_limit_bytes=VMEM_LIMIT)


def _pick(n, prefs):
    for p in prefs:
        if n % p == 0:
            return p
    return n


def _adaln_kernel(c_ref, w_ref, b_ref, o_ref):
    c = c_ref[...]
    a = (c * jax.nn.sigmoid(c)).astype(BF16)
    o_ref[...] = jnp.dot(a, w_ref[...].astype(BF16), preferred_element_type=F32) + b_ref[...]


def adaln_all(cond8, ada_w, ada_b):
    depth, d, n = ada_w.shape
    tn = _pick(n, (1024, 512, 256, 128))
    return pl.pallas_call(
        _adaln_kernel,
        grid=(depth, n // tn),
        in_specs=[pl.BlockSpec((8, d), lambda l, j: (0, 0)),
                  pl.BlockSpec((None, d, tn), lambda l, j: (l, 0, j)),
                  pl.BlockSpec((None, 1, tn), lambda l, j: (l, 0, j))],
        out_specs=pl.BlockSpec((None, 8, tn), lambda l, j: (l, 0, j)),
        out_shape=jax.ShapeDtypeStruct((depth, 8, n), F32),
        compiler_params=_cparams("arbitrary", "arbitrary"),
        name="adaln",
    )(cond8, ada_w, ada_b.reshape(depth, 1, n))


class Stream:
    def __init__(self, batch, seq, ctx_len):
        self.b, self.s, self.l = batch, seq, ctx_len
        self.t = seq + ctx_len
        self.nt = batch * self.t
        self.tm = _pick(math.gcd(seq, ctx_len), (256, 128, 64, 32, 16))

    def tiles(self, tm, lat_only):
        if not lat_only:
            return self.nt // tm, (lambda i: i), self.mod_row(tm)
        tpb, lt, ns = self.t // tm, self.l // tm, self.s // tm
        return self.b * ns, (lambda i: (i // ns) * tpb + lt + i % ns), (lambda i: i // ns)

    def mod_row(self, tm):
        tpb, l, b = self.t // tm, self.l, self.b
        return lambda i: jnp.where((i % tpb) * tm < l, b, i // tpb)


def _mod_spec(st, tm, k, d, grid_rank, row_axis):
    mr = st.mod_row(tm)
    if grid_rank == 1:
        return pl.BlockSpec((None, None, 1, d), lambda i: (k, mr(i), 0, 0))
    if row_axis == 1:
        return pl.BlockSpec((None, None, 1, d), lambda j, i: (k, mr(i), 0, j))
    raise ValueError


def _mm_kernel(*refs, n_a, k_offs, mode):
    a_refs = refs[:n_a]
    w_ref = refs[n_a]
    rest = refs[n_a + 1:]
    wbf_ref = rest[-1]
    o_ref = rest[-2]

    @pl.when(pl.program_id(1) == 0)
    def _():
        wbf_ref[...] = w_ref[...].astype(BF16)

    acc = None
    for a_ref, (k0, k1) in zip(a_refs, k_offs):
        part = jnp.dot(a_ref[...], wbf_ref[k0:k1, :], preferred_element_type=F32)
        acc = part if acc is None else acc + part
    if mode == "plain":
        o_ref[...] = acc.astype(o_ref.dtype)
    elif mode == "resid":
        x_ref, gate_ref = rest[0], rest[1]
        o_ref[...] = x_ref[...] + gate_ref[...] * acc
    elif mode == "glu":
        g_ref, b_ref = rest[0], rest[1]
        o_ref[...] = (g_ref[...].astype(F32) * jax.nn.sigmoid(acc + b_ref[...])).astype(o_ref.dtype)
    else:
        raise ValueError(mode)


def _norm_mm_kernel(x_ref, g_ref, sh_ref, sc_ref, w_ref, o_ref, wbf_ref):
    @pl.when(pl.program_id(1) == 0)
    def _():
        wbf_ref[...] = w_ref[...].astype(BF16)

    x = x_ref[...]
    y = x * lax.rsqrt(jnp.mean(x * x, axis=-1, keepdims=True) + EPS) * g_ref[...]
    h = (y * (1.0 + sc_ref[...]) + sh_ref[...]).astype(BF16)
    o_ref[...] = jnp.dot(h, wbf_ref[...], preferred_element_type=F32).astype(o_ref.dtype)


def norm_matmul(st, x, g, mod, k_shift, k_scale, w_stack, w_idx, tn_prefs=(2048, 1536, 1024, 512, 256, 128)):
    m, k = x.shape
    n = w_stack.shape[2]
    tm = st.tm
    tn = _pick(n, tn_prefs)
    mr = st.mod_row(tm)
    mod_spec = lambda kk: pl.BlockSpec((None, None, 1, k), lambda j, i: (kk, mr(i), 0, 0))
    return pl.pallas_call(
        _norm_mm_kernel,
        grid=(n // tn, m // tm),
        in_specs=[pl.BlockSpec((tm, k), lambda j, i: (i, 0)),
                  pl.BlockSpec((1, k), lambda j, i: (0, 0)),
                  mod_spec(k_shift), mod_spec(k_scale),
                  pl.BlockSpec((None, k, tn), lambda j, i: (w_idx, 0, j), pipeline_mode=pl.Buffered(1))],
        out_specs=pl.BlockSpec((tm, tn), lambda j, i: (i, j)),
        out_shape=jax.ShapeDtypeStruct((m, n), BF16),
        scratch_shapes=[pltpu.VMEM((k, tn), BF16)],
        compiler_params=_cparams("arbitrary", "arbitrary"),
        name="norm_mm",
    )(x, g.reshape(1, k), mod, mod, w_stack)


def _gelu_glu_kernel(yf_ref, yb_ref, u_ref, d_ref, w_ref, b_ref, o_ref, wbf_ref):
    @pl.when(pl.program_id(0) == 0)
    def _():
        wbf_ref[...] = w_ref[...].astype(BF16)

    y = yf_ref[...].astype(F32) + yb_ref[...].astype(F32) + d_ref[...] * u_ref[...].astype(F32)
    g = jax.nn.gelu(y).astype(BF16)
    acc = jnp.dot(g, wbf_ref[...], preferred_element_type=F32)
    o_ref[...] = (g.astype(F32) * jax.nn.sigmoid(acc + b_ref[...])).astype(o_ref.dtype)


def gelu_glu(st, y_f, y_b, p, d_skip, w_stack, w_idx, bias):
    m, width = y_f.shape
    tm = st.tm
    row = pl.BlockSpec((tm, width), lambda i: (i, 0))
    vec = pl.BlockSpec((1, width), lambda i: (0, 0))
    return pl.pallas_call(
        _gelu_glu_kernel,
        grid=(m // tm,),
        in_specs=[row, row, row, vec,
                  pl.BlockSpec((None, width, width), lambda i: (w_idx, 0, 0), pipeline_mode=pl.Buffered(1)),
                  vec],
        out_specs=row,
        out_shape=jax.ShapeDtypeStruct((m, width), BF16),
        scratch_shapes=[pltpu.VMEM((width, width), BF16)],
        compiler_params=_cparams("arbitrary"),
        name="gelu_glu",
    )(y_f, y_b, p, d_skip.reshape(1, width), w_stack, bias.reshape(1, width))


def matmul(a_parts, w_stack, w_idx, mode="plain", out_dtype=BF16, extra=(), st=None,
           tn_prefs=(2048, 1536, 1024, 512, 256, 128)):
    m = a_parts[0].shape[0]
    _, k, n = w_stack.shape
    tm = _pick(m if st is None else st.tm, (512, 256, 128, 64, 32, 16, 8))
    tn = _pick(n, tn_prefs)
    k_offs, off = [], 0
    in_specs, args = [], []
    for a in a_parts:
        ka = a.shape[1]
        k_offs.append((off, off + ka))
        off += ka
        in_specs.append(pl.BlockSpec((tm, ka), lambda j, i: (i, 0)))
        args.append(a)
    assert off == k
    in_specs.append(pl.BlockSpec((None, k, tn), lambda j, i: (w_idx, 0, j), pipeline_mode=pl.Buffered(1)))
    args.append(w_stack)
    io_alias = {}
    if mode == "resid":
        x, mod, k_gate = extra
        in_specs.append(pl.BlockSpec((tm, tn), lambda j, i: (i, j)))
        args.append(x)
        in_specs.append(_mod_spec(st, tm, k_gate, tn, 2, 1))
        args.append(mod)
        io_alias = {len(args) - 2: 0}
        out_dtype = F32
    elif mode == "glu":
        g, bias = extra
        in_specs.append(pl.BlockSpec((tm, tn), lambda j, i: (i, j)))
        args.append(g)
        in_specs.append(pl.BlockSpec((1, tn), lambda j, i: (0, j)))
        args.append(bias.reshape(1, n))
    return pl.pallas_call(
        functools.partial(_mm_kernel, n_a=len(a_parts), k_offs=tuple(k_offs), mode=mode),
        grid=(n // tn, m // tm),
        in_specs=in_specs,
        out_specs=pl.BlockSpec((tm, tn), lambda j, i: (i, j)),
        out_shape=jax.ShapeDtypeStruct((m, n), out_dtype),
        scratch_shapes=[pltpu.VMEM((k, tn), BF16)],
        input_output_aliases=io_alias,
        compiler_params=_cparams("arbitrary", "arbitrary"),
        name="mm_" + mode,
    )(*args)


def _s5_kernel(uf_ref, ub_ref, b_ref, c_ref, lam_ref, yf_ref, yb_ref, h_ref, bu_ref, u_scr, y_scr, *, tc, jb):
    half = b_ref.shape[2] // 2
    nbat = uf_ref.shape[0]

    @pl.when(pl.program_id(1) == 0)
    def _():
        h_ref[...] = jnp.zeros_like(h_ref)

    rows = tc * 8
    is_fwd = (lax.broadcasted_iota(jnp.int32, (rows, LANES), 0) % 8) < 4
    flip = (lax.broadcasted_iota(jnp.int32, (tc, tc), 0)
            + lax.broadcasted_iota(jnp.int32, (tc, tc), 1) == tc - 1).astype(BF16)
    for q in range(jb):
        lanes = slice(q * LANES, (q + 1) * LANES)
        for b in range(nbat):
            u_scr[q, pl.ds(b, tc, stride=8), :] = uf_ref[b, :, lanes].astype(F32)
            u_scr[q, pl.ds(nbat + b, tc, stride=8), :] = jnp.dot(flip, ub_ref[b, :, lanes],
                                                                 preferred_element_type=F32)
        u = u_scr[q]
        lhs = jnp.concatenate([jnp.where(is_fwd, u, 0.0), jnp.where(is_fwd, 0.0, u)], axis=1).astype(BF16)
        bu_ref[q] = jnp.dot(lhs, b_ref[q], preferred_element_type=F32)

    for q in range(jb):
        lam_r = lam_ref[q, 0]
        lam_i = lam_ref[q, 1]

        def step(t, carry, q=q, lam_r=lam_r, lam_i=lam_i):
            hr, hi = carry
            r0 = pl.multiple_of(t * 8, 8)
            nhr = lam_r * hr - lam_i * hi + bu_ref[q, pl.ds(r0, 8), 0:half]
            nhi = lam_r * hi + lam_i * hr + bu_ref[q, pl.ds(r0, 8), half:2 * half]
            bu_ref[q, pl.ds(r0, 8), 0:half] = nhr
            bu_ref[q, pl.ds(r0, 8), half:2 * half] = nhi
            return nhr, nhi

        hr, hi = lax.fori_loop(0, tc, step, (h_ref[q, :, 0:half], h_ref[q, :, half:2 * half]), unroll=True)
        h_ref[q, :, 0:half] = hr
        h_ref[q, :, half:2 * half] = hi

    for q in range(jb):
        lanes = slice(q * LANES, (q + 1) * LANES)
        y2 = jnp.dot(bu_ref[q].astype(BF16), c_ref[q], preferred_element_type=F32)
        y_scr[q] = jnp.where(is_fwd, y2[:, 0:LANES], y2[:, LANES:2 * LANES])
        for b in range(nbat):
            yf_ref[b, :, lanes] = y_scr[q, pl.ds(b, tc, stride=8), :].astype(yf_ref.dtype)
            yb = y_scr[q, pl.ds(nbat + b, tc, stride=8), :].astype(BF16)
            yb_ref[b, :, lanes] = jnp.dot(flip, yb, preferred_element_type=F32).astype(yb_ref.dtype)


def s5_scan(st, p, width, b_bd, c_bd, lam8):
    bsz, t_len, l_len = st.b, st.t, st.l
    assert 2 * bsz == 8
    d = p.shape[1]
    nb = width // LANES
    jb = _pick(nb, (2, 1))
    tc = _pick(math.gcd(st.s, l_len), (128, 64, 32, 16, 8))
    rows = tc * 8
    ns = b_bd.shape[2]
    n_l, n_c = l_len // tc, t_len // tc

    def bwd_chunk(c):
        return jnp.where(c < n_l, n_l - 1 - c, n_c + n_l - 1 - c)

    p3 = p.reshape(bsz, t_len, d)
    out = jax.ShapeDtypeStruct((bsz, t_len, width), BF16)
    y_f, y_b = pl.pallas_call(
        functools.partial(_s5_kernel, tc=tc, jb=jb),
        grid=(nb // jb, n_c),
        in_specs=[pl.BlockSpec((bsz, tc, jb * LANES), lambda j, c: (0, c, j)),
                  pl.BlockSpec((bsz, tc, jb * LANES), lambda j, c: (0, bwd_chunk(c), j)),
                  pl.BlockSpec((jb, 2 * LANES, ns), lambda j, c: (j, 0, 0)),
                  pl.BlockSpec((jb, ns, 2 * LANES), lambda j, c: (j, 0, 0)),
                  pl.BlockSpec((jb, 2, 8, ns // 2), lambda j, c: (j, 0, 0, 0))],
        out_specs=[pl.BlockSpec((bsz, tc, jb * LANES), lambda j, c: (0, c, j)),
                   pl.BlockSpec((bsz, tc, jb * LANES), lambda j, c: (0, bwd_chunk(c), j))],
        out_shape=[out, out],
        scratch_shapes=[pltpu.VMEM((jb, 8, ns), F32), pltpu.VMEM((jb, rows, ns), F32),
                        pltpu.VMEM((jb, rows, LANES), F32), pltpu.VMEM((jb, rows, LANES), F32)],
        compiler_params=_cparams("arbitrary", "arbitrary"),
        name="s5_scan",
    )(p3, p3, b_bd, c_bd, lam8)
    return y_f.reshape(bsz * t_len, width), y_b.reshape(bsz * t_len, width)


def s5_tables(lam_re, lam_im, log_dt, b_re, b_im, c_re, c_im):
    _, g, p = lam_re.shape
    c = b_re.shape[-1]
    nb = g // S5_BLOCK_GROUPS
    gb = S5_BLOCK_GROUPS
    lam = lax.complex(lam_re.astype(F32), lam_im.astype(F32))
    dt = jnp.exp(log_dt.astype(F32))[..., None]
    lam_bar = jnp.exp(lam * dt)
    b_bar = ((lam_bar - 1.0) / lam)[..., None] * lax.complex(b_re.astype(F32), b_im.astype(F32))
    eye = jnp.eye(gb, dtype=F32)
    bb = b_bar.reshape(2, nb, gb, p, c)

    def b_lay(z):
        return jnp.einsum('djgpc,gh->jdgchp', z, eye).reshape(nb, 2 * gb * c, gb * p)

    b_bd = jnp.concatenate([b_lay(bb.real), b_lay(bb.imag)], axis=-1).astype(BF16)
    cc = lax.complex(c_re.astype(F32), c_im.astype(F32)).reshape(2, nb, gb, c, p)

    def c_lay(z):
        return jnp.einsum('djgcp,gh->jgpdhc', z, eye).reshape(nb, gb * p, 2 * gb * c)

    c_bd = jnp.concatenate([c_lay(cc.real), -c_lay(cc.imag)], axis=1).astype(BF16)
    lb = lam_bar.reshape(2, nb, gb * p)
    lam8 = jnp.stack([lb.real, lb.imag], axis=0)
    lam8 = jnp.repeat(lam8.transpose(2, 0, 1, 3), 4, axis=2)
    return b_bd, c_bd, lam8


def _fnet_chan_kernel(f_ref, cd_ref, sd_ref, w_ref, o_ref, wc_ref):
    nh, dh = w_ref.shape[0], w_ref.shape[1]

    @pl.when(pl.program_id(0) == 0)
    def _():
        for h in range(nh):
            wh = w_ref[h]
            wc_ref[h, :, 0:dh] = jnp.dot(cd_ref[...], wh, preferred_element_type=F32,
                                         precision=lax.Precision.HIGHEST).astype(BF16)
            wc_ref[h, :, dh:2 * dh] = jnp.dot(sd_ref[...], wh, preferred_element_type=F32,
                                              precision=lax.Precision.HIGHEST).astype(BF16)

    for h in range(nh):
        r = jnp.dot(f_ref[:, h * dh:(h + 1) * dh], wc_ref[h], preferred_element_type=F32)
        o_ref[:, h * dh:(h + 1) * dh] = r[:, 0:dh].astype(o_ref.dtype)
        o_ref[:, (nh + h) * dh:(nh + h + 1) * dh] = r[:, dh:2 * dh].astype(o_ref.dtype)


def fnet_chan(p, col_block, fnet_w, cd, sd):
    nt = p.shape[0]
    nh, dh, _ = fnet_w.shape
    width = nh * dh
    tm = _pick(nt, (512, 256, 128, 64, 32, 16, 8))
    return pl.pallas_call(
        _fnet_chan_kernel,
        grid=(nt // tm,),
        in_specs=[pl.BlockSpec((tm, width), lambda i: (i, col_block)),
                  pl.BlockSpec((dh, dh), lambda i: (0, 0)),
                  pl.BlockSpec((dh, dh), lambda i: (0, 0)),
                  pl.BlockSpec((nh, dh, dh), lambda i: (0, 0, 0))],
        out_specs=pl.BlockSpec((tm, 2 * width), lambda i: (i, 0)),
        out_shape=jax.ShapeDtypeStruct((nt, 2 * width), BF16),
        scratch_shapes=[pltpu.VMEM((nh, dh, 2 * dh), BF16)],
        compiler_params=_cparams("arbitrary"),
        name="fnet_chan",
    )(p, cd, sd, fnet_w)


def _fnet_seq_kernel(cl_ref, sl_ref, cs_ref, ss_ref, p_ref, o_ref, *, l_len):
    w = o_ref.shape[1]
    tm = o_ref.shape[0]
    t_len = p_ref.shape[0]
    in_ctx = pl.program_id(1) * tm < l_len

    @pl.when(in_ctx)
    def _():
        o_ref[...] = (jnp.dot(cl_ref[...], p_ref[0:l_len, 0:w], preferred_element_type=F32)
                      - jnp.dot(sl_ref[...], p_ref[0:l_len, w:2 * w], preferred_element_type=F32)).astype(o_ref.dtype)

    @pl.when(jnp.logical_not(in_ctx))
    def _():
        o_ref[...] = (jnp.dot(cs_ref[...], p_ref[l_len:t_len, 0:w], preferred_element_type=F32)
                      - jnp.dot(ss_ref[...], p_ref[l_len:t_len, w:2 * w], preferred_element_type=F32)).astype(o_ref.dtype)


def fnet_seq(st, pcs, tab_l, tab_s):
    w2 = pcs.shape[1]
    tm = st.tm
    tpb, lt = st.t // tm, st.l // tm
    ctx_tile = lambda b, i: (jnp.minimum(i, lt - 1), 0)
    lat_tile = lambda b, i: (jnp.maximum(i - lt, 0), 0)
    return pl.pallas_call(
        functools.partial(_fnet_seq_kernel, l_len=st.l),
        grid=(st.b, tpb),
        in_specs=[pl.BlockSpec((tm, st.l), ctx_tile), pl.BlockSpec((tm, st.l), ctx_tile),
                  pl.BlockSpec((tm, st.s), lat_tile), pl.BlockSpec((tm, st.s), lat_tile),
                  pl.BlockSpec((st.t, w2), lambda b, i: (b, 0))],
        out_specs=pl.BlockSpec((tm, w2 // 2), lambda b, i: (b * tpb + i, 0)),
        out_shape=jax.ShapeDtypeStruct((st.nt, w2 // 2), BF16),
        compiler_params=_cparams("arbitrary", "arbitrary"),
        name="fnet_seq",
    )(tab_l[0], tab_l[1], tab_s[0], tab_s[1], pcs)


def dft_tables(n):
    scale = 1.0 / math.sqrt(n)
    step = 2.0 * math.pi / n
    t = jnp.arange(n, dtype=jnp.int32)
    split = 64
    if n % split or n <= split:
        ang = ((t[:, None] * t[None, :]) % n).astype(F32) * step
        return jnp.cos(ang) * scale, jnp.sin(ang) * scale
    a = jnp.arange(n // split, dtype=jnp.int32) * split
    b = jnp.arange(split, dtype=jnp.int32)
    ang_a = ((a[:, None] * t[None, :]) % n).astype(F32) * step
    ang_b = ((b[:, None] * t[None, :]) % n).astype(F32) * step
    ca, sa = jnp.cos(ang_a)[:, None, :], jnp.sin(ang_a)[:, None, :]
    cb, sb = jnp.cos(ang_b)[None, :, :] * scale, jnp.sin(ang_b)[None, :, :] * scale
    return (ca * cb - sa * sb).reshape(n, n), (sa * cb + ca * sb).reshape(n, n)


def _pool_kernel(v_ref, w_ref, sc_ref, o_ref, *, windows, l_len):
    dh = w_ref.shape[1]
    for r0, r1 in ((0, l_len), (l_len, v_ref.shape[0])):
        t_len = r1 - r0
        t = lax.broadcasted_iota(jnp.int32, (t_len, dh), 0)
        for g, win in enumerate(windows):
            v = v_ref[r0:r1, g * dh:(g + 1) * dh].astype(F32)
            lo = win // 2
            acc = jnp.zeros_like(v)
            for j in range(-lo, win - lo):
                src = t + j
                shifted = v if j == 0 else pltpu.roll(v, (-j) % t_len, 0)
                acc = acc + jnp.where((src >= 0) & (src < t_len), shifted, 0.0)
            cnt = jnp.clip(t + (win - lo), 0, t_len) - jnp.clip(t - lo, 0, t_len)
            pooled = acc / cnt.astype(F32) - v
            y = jnp.dot(pooled.astype(BF16), w_ref[g].astype(BF16), preferred_element_type=F32)
            o_ref[r0:r1, g * dh:(g + 1) * dh] = (y * sc_ref[:, g * dh:(g + 1) * dh]).astype(o_ref.dtype)


def pool_mix(st, p, pool_w, pool_scale):
    ng, dh, _ = pool_w.shape
    width = ng * dh
    return pl.pallas_call(
        functools.partial(_pool_kernel, windows=POOL_WINDOWS, l_len=st.l),
        grid=(st.b,),
        in_specs=[pl.BlockSpec((st.t, width), lambda b: (b, 0)),
                  pl.BlockSpec((ng, dh, dh), lambda b: (0, 0, 0)),
                  pl.BlockSpec((1, width), lambda b: (0, 0))],
        out_specs=pl.BlockSpec((st.t, width), lambda b: (b, 0)),
        out_shape=jax.ShapeDtypeStruct((st.nt, width), BF16),
        compiler_params=_cparams("arbitrary"),
        name="pool_mix",
    )(p, pool_w, pool_scale.reshape(1, width))


def rope_tables(s, l_len):
    tm = l_len
    rows = s // GRID_W
    row = jnp.repeat(jnp.arange(rows), GRID_W)
    col = jnp.tile(jnp.arange(GRID_W), rows)
    nf = HEAD_DIM // 4
    inv = jnp.power(ROPE_THETA, -jnp.arange(nf, dtype=F32) / nf)
    a_row = row.astype(F32)[:, None] * inv
    a_col = col.astype(F32)[:, None] * inv
    ang = jnp.concatenate([a_row, a_row, a_col, a_col], axis=1)
    sign = jnp.tile(jnp.concatenate([-jnp.ones((nf,), F32), jnp.ones((nf,), F32)]), 2)
    cos_t = jnp.concatenate([jnp.ones((tm, HEAD_DIM), F32), jnp.cos(ang)], axis=0)
    sin_t = jnp.concatenate([jnp.zeros((tm, HEAD_DIM), F32), jnp.sin(ang) * sign], axis=0)
    return cos_t, sin_t


def _norm_rope(x, gain, cos, sin):
    lane = lax.broadcasted_iota(jnp.int32, x.shape, 1)
    first = (lane % (HEAD_DIM // 2)) < (HEAD_DIM // 4)
    n = x * lax.rsqrt(jnp.mean(x * x, axis=-1, keepdims=True) + EPS) * gain
    partner = jnp.where(first, pltpu.roll(n, HEAD_DIM - HEAD_DIM // 4, 1), pltpu.roll(n, HEAD_DIM // 4, 1))
    return n * cos + partner * sin


def _attn_kernel(*refs, q_per_kv, q_scale, l_len):
    q_refs = refs[:q_per_kv]
    kraw_ref, v_ref, g_ref, cq_ref, sq_ref, ck_ref, sk_ref, o_ref, k_scr = refs[q_per_kv:]
    tq = o_ref.shape[0]
    dn = (((1,), (1,)), ((), ()))

    @pl.when(pl.program_id(2) == 0)
    def _():
        k_scr[...] = _norm_rope(kraw_ref[...].astype(F32), g_ref[1], ck_ref[...], sk_ref[...]).astype(BF16)

    def run(n_keys):
        q_gain = g_ref[0] * q_scale
        qs = [_norm_rope(q_ref[...].astype(F32), q_gain, cq_ref[...], sq_ref[...]).astype(BF16) for q_ref in q_refs]
        k = k_scr[0:n_keys, :]
        v = v_ref[0:n_keys, :]
        for g in range(q_per_kv):
            q = qs[g]
            s_ = lax.dot_general(q, k, dn, preferred_element_type=F32)
            e = jnp.exp2(s_ - s_.max(axis=-1, keepdims=True))
            acc = jnp.dot(e.astype(BF16), v, preferred_element_type=F32)
            o_ref[:, g * HEAD_DIM:(g + 1) * HEAD_DIM] = (acc / e.sum(axis=-1, keepdims=True)).astype(o_ref.dtype)

    in_ctx = pl.program_id(2) * tq < l_len

    @pl.when(in_ctx)
    def _():
        run(l_len)

    @pl.when(jnp.logical_not(in_ctx))
    def _():
        run(k_scr.shape[0])


def attention(st, p, gains, cos_t, sin_t, q_col0, n_q_heads, q_scale):
    q_per_kv = n_q_heads // KV_HEADS
    qw = q_per_kv * HEAD_DIM
    tq = st.tm
    tpb = st.t // tq
    k_col0 = q_col0 + n_q_heads
    v_col0 = k_col0 + KV_HEADS
    q_specs = [pl.BlockSpec((tq, HEAD_DIM), lambda b, h, i, g=g: (b * tpb + i, q_col0 + h * q_per_kv + g))
               for g in range(q_per_kv)]
    tab_q = pl.BlockSpec((tq, HEAD_DIM), lambda b, h, i: (i, 0))
    tab_k = pl.BlockSpec((st.t, HEAD_DIM), lambda b, h, i: (0, 0))
    return pl.pallas_call(
        functools.partial(_attn_kernel, q_per_kv=q_per_kv, q_scale=q_scale, l_len=st.l),
        grid=(st.b, KV_HEADS, tpb),
        in_specs=q_specs + [pl.BlockSpec((st.t, HEAD_DIM), lambda b, h, i: (b, k_col0 + h)),
                            pl.BlockSpec((st.t, HEAD_DIM), lambda b, h, i: (b, v_col0 + h)),
                            pl.BlockSpec((2, 1, HEAD_DIM), lambda b, h, i: (0, 0, 0)),
                            tab_q, tab_q, tab_k, tab_k],
        out_specs=pl.BlockSpec((tq, qw), lambda b, h, i: (b * tpb + i, h)),
        out_shape=jax.ShapeDtypeStruct((st.nt, n_q_heads * HEAD_DIM), BF16),
        scratch_shapes=[pltpu.VMEM((st.t, HEAD_DIM), BF16)],
        compiler_params=_cparams("arbitrary", "arbitrary", "arbitrary"),
        name="attention",
    )(*([p] * q_per_kv), p, p, gains, cos_t, sin_t, cos_t, sin_t)


def _pack_pairs(lo, hi):
    lo_b = lax.bitcast_convert_type(lo.astype(BF16).astype(F32), jnp.uint32)
    hi_b = lax.bitcast_convert_type(hi.astype(BF16).astype(F32), jnp.uint32)
    return (lo_b >> 16) | (hi_b & jnp.uint32(0xFFFF0000))


def _unpack_pairs(word):
    lo = lax.bitcast_convert_type(word << 16, F32)
    hi = lax.bitcast_convert_type(word & jnp.uint32(0xFFFF0000), F32)
    return lo, hi


ROW_TILE = 8


def _store_row_tiles(ref, word):
    rows = word.shape[0]
    for j in range(ROW_TILE):
        ref[pl.ds(j, rows, stride=ROW_TILE), :] = word[:, j * LANES:(j + 1) * LANES]


def _load_row_tiles(ref, rows):
    return [ref[pl.ds(j, rows, stride=ROW_TILE), :] for j in range(ROW_TILE)]


def _router_kernel(x_ref, g_ref, sh_ref, sc_ref, wr_ref, br_ref, h_ref, idx_ref, wgt_ref):
    x = x_ref[...]
    y = x * lax.rsqrt(jnp.mean(x * x, axis=-1, keepdims=True) + EPS) * g_ref[...]
    h = y * (1.0 + sc_ref[...]) + sh_ref[...]
    half = h.shape[1] // 2
    _store_row_tiles(h_ref, _pack_pairs(h[:, 0:half], h[:, half:2 * half]))
    ne = wr_ref.shape[0]
    dn = (((1,), (1,)), ((), ()))
    h1 = h.astype(BF16)
    h2 = (h - h1.astype(F32)).astype(BF16)
    w = wr_ref[...]
    w1 = w.astype(BF16)
    w2 = (w - w1.astype(F32)).astype(BF16)
    r1 = lax.dot_general(jnp.concatenate([w1, w2], axis=0), h1, dn, preferred_element_type=F32)
    r2 = lax.dot_general(w1, h2, dn, preferred_element_type=F32)
    logits = r1[0:ne] + r1[ne:2 * ne] + r2 + br_ref[...]
    ne, tm = logits.shape
    eidx = lax.broadcasted_iota(jnp.int32, (ne, tm), 0)
    vals, idxs = [], []
    cur = logits
    for _ in range(TOP_K):
        m = cur.max(axis=0, keepdims=True)
        sel = jnp.min(jnp.where(cur == m, eidx, ne), axis=0, keepdims=True)
        vals.append(m)
        idxs.append(sel)
        cur = jnp.where(eidx == sel, -jnp.inf, cur)
    es = [jnp.exp(v - vals[0]) for v in vals]
    den = es[0]
    for e in es[1:]:
        den = den + e
    zi = jnp.zeros((8 - TOP_K, tm), jnp.int32)
    zf = jnp.zeros((8 - TOP_K, tm), F32)
    idx_ref[...] = jnp.concatenate(idxs + [zi], axis=0)
    wgt_ref[...] = jnp.concatenate([e / den for e in es] + [zf], axis=0)


def norm_router(st, x, g, mod, k_shift, k_scale, w_router, b_router, lat_only):
    d = x.shape[1]
    ne = w_router.shape[1]
    tm = _pick(st.tm, (256, 128))
    n_tiles, row_blk, mod_row = st.tiles(tm, lat_only)
    nt = n_tiles * tm
    mod_spec = lambda k: pl.BlockSpec((None, None, 1, d), lambda i: (k, mod_row(i), 0, 0))
    return pl.pallas_call(
        _router_kernel,
        grid=(n_tiles,),
        in_specs=[pl.BlockSpec((tm, d), lambda i: (row_blk(i), 0)),
                  pl.BlockSpec((1, d), lambda i: (0, 0)),
                  mod_spec(k_shift), mod_spec(k_scale),
                  pl.BlockSpec((ne, d), lambda i: (0, 0)),
                  pl.BlockSpec((ne, 1), lambda i: (0, 0))],
        out_specs=[pl.BlockSpec((tm * ROW_TILE, LANES), lambda i: (i, 0)),
                   pl.BlockSpec((8, tm), lambda i: (0, i)),
                   pl.BlockSpec((8, tm), lambda i: (0, i))],
        out_shape=[jax.ShapeDtypeStruct((nt * ROW_TILE, LANES), jnp.uint32),
                   jax.ShapeDtypeStruct((8, nt), jnp.int32),
                   jax.ShapeDtypeStruct((8, nt), F32)],
        compiler_params=_cparams("arbitrary"),
        name="norm_router",
    )(x, g.reshape(1, d), mod, mod, w_router.T, b_router.reshape(ne, 1))


def _route_plan_kernel(idx_ref, dest_ref, blk_ref, ends_ref, pre_ref, *, ne, tb, chunk):
    nt = idx_ref.shape[1]
    n_chunk = nt // chunk
    eidx = lax.broadcasted_iota(jnp.int32, (ne, chunk), 0)
    tri = (lax.broadcasted_iota(jnp.int32, (chunk, chunk), 0)
           < lax.broadcasted_iota(jnp.int32, (chunk, chunk), 1)).astype(BF16)

    def count(c, carry):
        c0 = pl.multiple_of(c * chunk, chunk)
        oh = jnp.zeros((ne, chunk), F32)
        for k in range(TOP_K):
            oh = oh + (eidx == idx_ref[k:k + 1, pl.ds(c0, chunk)]).astype(F32)
        pre_ref[:, pl.ds(c0, chunk)] = jnp.dot(oh.astype(BF16), tri, preferred_element_type=F32) + carry
        return carry + oh.sum(axis=1, keepdims=True)

    counts = lax.fori_loop(0, n_chunk, count, jnp.zeros((ne, 1), F32))
    padded = jnp.ceil(counts * (1.0 / tb)) * tb
    r_i = lax.broadcasted_iota(jnp.int32, (ne, ne), 0)
    c_i = lax.broadcasted_iota(jnp.int32, (ne, ne), 1)
    padded_row = jnp.sum(jnp.where(r_i == c_i, padded, 0.0), axis=0, keepdims=True)
    pstart = jnp.sum(jnp.where(c_i < r_i, padded_row, 0.0), axis=1, keepdims=True)
    pend = pstart + padded

    def place(c, carry):
        c0 = pl.multiple_of(c * chunk, chunk)
        base = pre_ref[:, pl.ds(c0, chunk)] + pstart
        for k in range(TOP_K):
            row = jnp.sum(jnp.where(eidx == idx_ref[k:k + 1, pl.ds(c0, chunk)], base, 0.0), axis=0, keepdims=True)
            row = row.astype(jnp.int32)
            for hh in range(chunk // LANES):
                r = k * (nt // LANES) + c * (chunk // LANES) + hh
                dest_ref[pl.ds(r, 1), :] = row[:, hh * LANES:(hh + 1) * LANES]
        return carry

    lax.fori_loop(0, n_chunk, place, 0)

    nbp = blk_ref.shape[1]
    starts = (lax.broadcasted_iota(jnp.int32, (ne, nbp), 1) * tb).astype(F32)
    blk_e = jnp.minimum(jnp.sum((pend <= starts).astype(F32), axis=0, keepdims=True), ne - 1.0)
    n_live = jnp.sum(padded, axis=0, keepdims=True) * (1.0 / tb)
    e_col = lax.broadcasted_iota(jnp.int32, (ne, nbp), 0).astype(F32)
    owns = jnp.where(padded > 0.0, 1.0, 0.0)
    later = jnp.where(e_col > blk_e, owns, 0.0)
    nxt = jnp.min(jnp.where(later > 0.0, e_col, float(ne)), axis=0, keepdims=True)
    nxt = jnp.where(nxt >= float(ne), -1.0, nxt)
    rank = jnp.sum(jnp.where(e_col < blk_e, owns, 0.0), axis=0, keepdims=True)
    slot = rank - 2.0 * jnp.floor(rank * 0.5)
    blk_ref[...] = jnp.concatenate([blk_e, jnp.broadcast_to(n_live, (1, nbp)), nxt, slot, jnp.zeros((4, nbp), F32)],
                                   axis=0).astype(jnp.int32)

    r_l = lax.broadcasted_iota(jnp.int32, (ne, LANES), 0)
    c_l = lax.broadcasted_iota(jnp.int32, (ne, LANES), 1)
    to_row = lambda col: jnp.sum(jnp.where(r_l == c_l, col, 0.0), axis=0, keepdims=True)
    ends_ref[...] = jnp.concatenate([to_row(pstart + counts), to_row(pend), jnp.zeros((6, LANES), F32)],
                                    axis=0).astype(jnp.int32)


def route_plan(idx8, ne, tb, n_blocks):
    nt = idx8.shape[1]
    chunk = _pick(nt, (256, 128))
    nbp = -(-n_blocks // LANES) * LANES
    return pl.pallas_call(
        functools.partial(_route_plan_kernel, ne=ne, tb=tb, chunk=chunk),
        out_shape=[jax.ShapeDtypeStruct((TOP_K * nt // LANES, LANES), jnp.int32),
                   jax.ShapeDtypeStruct((8, nbp), jnp.int32),
                   jax.ShapeDtypeStruct((8, LANES), jnp.int32)],
        scratch_shapes=[pltpu.VMEM((ne, nt), F32)],
        compiler_params=pltpu.CompilerParams(vmem_limit_bytes=VMEM_LIMIT),
        name="route_plan",
    )(idx8)


def _dispatch_kernel(dest_ref, ends_ref, h_ref, xs_hbm, zrow, sem, *, rows, ne, nt):
    i = pl.program_id(0)
    base = i * rows

    @pl.when(i == 0)
    def _():
        zrow[...] = jnp.zeros_like(zrow)

        def per_expert(e, carry, waiting):
            lo, hi = ends_ref[0, e], ends_ref[1, e]

            def put(r, c):
                dst = xs_hbm.at[pl.ds(pl.multiple_of(r * ROW_TILE, ROW_TILE), ROW_TILE)]
                pltpu.make_async_copy(zrow.at[pl.ds(0, ROW_TILE)], dst, sem.at[1]).start()
                return c

            def done(r, c):
                pltpu.make_async_copy(zrow.at[pl.ds(0, ROW_TILE)], xs_hbm.at[pl.ds(0, ROW_TILE)], sem.at[1]).wait()
                return c

            lax.fori_loop(lo, hi, done if waiting else put, 0)
            return carry

        for waiting in (False, True):
            lax.fori_loop(0, ne, functools.partial(per_expert, waiting=waiting), 0)

        blk = zrow.shape[0]
        tb = blk // ROW_TILE
        first_dead = ends_ref[1, ne - 1] // tb
        n_blk = xs_hbm.shape[0] // blk

        def put_blk(j, c):
            pltpu.make_async_copy(zrow, xs_hbm.at[pl.ds(pl.multiple_of(j * blk, blk), blk)], sem.at[1]).start()
            return c

        def done_blk(j, c):
            pltpu.make_async_copy(zrow, xs_hbm.at[pl.ds(0, blk)], sem.at[1]).wait()
            return c

        lax.fori_loop(first_dead, n_blk, put_blk, 0)
        lax.fori_loop(first_dead, n_blk, done_blk, 0)

    def issue(r, carry):
        src = h_ref.at[pl.ds(pl.multiple_of(r * ROW_TILE, ROW_TILE), ROW_TILE)]
        for k in range(TOP_K):
            d = dest_ref[k * nt + base + r]
            dst = xs_hbm.at[pl.ds(pl.multiple_of(d * ROW_TILE, ROW_TILE), ROW_TILE)]
            pltpu.make_async_copy(src, dst, sem.at[0]).start(priority=k % 2)
        return carry

    lax.fori_loop(0, rows, issue, 0, unroll=4)
    for _ in range(TOP_K):
        pltpu.make_async_copy(h_ref, xs_hbm.at[pl.ds(0, rows * ROW_TILE)], sem.at[0]).wait()


def moe_dispatch(h_tiles, dest1, ends, cap, ne):
    nt = h_tiles.shape[0] // ROW_TILE
    rows = _pick(nt, (512, 256, 128, 64, 32, 16, 8))
    return pl.pallas_call(
        functools.partial(_dispatch_kernel, rows=rows, ne=ne, nt=nt),
        grid=(nt // rows,),
        in_specs=[pl.BlockSpec(memory_space=pltpu.SMEM),
                  pl.BlockSpec(memory_space=pltpu.SMEM),
                  pl.BlockSpec((rows * ROW_TILE, LANES), lambda i: (i, 0))],
        out_specs=pl.BlockSpec(memory_space=pl.ANY),
        out_shape=jax.ShapeDtypeStruct((cap * ROW_TILE, LANES), h_tiles.dtype),
        scratch_shapes=[pltpu.VMEM((MOE_ROWS * ROW_TILE, LANES), h_tiles.dtype), pltpu.SemaphoreType.DMA((2,))],
        compiler_params=_cparams("arbitrary"),
        name="moe_dispatch",
    )(dest1, ends, h_tiles)


def _expert_kernel(be_ref, nb_ref, nxt_ref, slot_ref, x_ref, bg_ref, bu_ref, bd_ref, wg_hbm, wu_hbm, wd_hbm, o_ref,
                   wg_st, wu_st, wd_st, wg_bf, wu_bf, wd_bf, sem, *, layer):
    i = pl.program_id(0)
    e = be_ref[i]
    live = i < nb_ref[0]
    first = (i == 0) | (e != be_ref[jnp.maximum(i - 1, 0)])
    slot = slot_ref[i]
    half = wg_bf.shape[0] // 2

    def weight_copies(ex, s):
        return (pltpu.make_async_copy(wg_hbm.at[layer, ex], wg_st.at[s], sem.at[s, 0]),
                pltpu.make_async_copy(wu_hbm.at[layer, ex], wu_st.at[s], sem.at[s, 1]),
                pltpu.make_async_copy(wd_hbm.at[layer, ex], wd_st.at[s], sem.at[s, 2]))

    @pl.when(i == 0)
    def _():
        for cp in weight_copies(e, slot):
            cp.start(priority=1)

    @pl.when(live & first)
    def _():
        for cp in weight_copies(e, slot):
            cp.wait()

        @pl.when(nxt_ref[i] >= 0)
        def _():
            for cp in weight_copies(nxt_ref[i], 1 - slot):
                cp.start(priority=1)

        wg_bf[...] = wg_st[slot].astype(BF16)
        wu_bf[...] = wu_st[slot].astype(BF16)
        wd_bf[...] = wd_st[slot].astype(BF16)

    @pl.when(live)
    def _():
        tb = x_ref.shape[0] // ROW_TILE
        lo, hi = _unpack_pairs(jnp.concatenate(_load_row_tiles(x_ref, tb), axis=1))
        lo = lo.astype(BF16)
        hi = hi.astype(BF16)

        def proj(w_bf, b_ref):
            return (jnp.dot(lo, w_bf[0:half, :], preferred_element_type=F32)
                    + jnp.dot(hi, w_bf[half:2 * half, :], preferred_element_type=F32) + b_ref[...])

        g = jnp.minimum(proj(wg_bf, bg_ref), SWIGLU_LIMIT)
        u = jnp.clip(proj(wu_bf, bu_ref), -SWIGLU_LIMIT, SWIGLU_LIMIT)
        a = g * jax.nn.sigmoid(SWIGLU_ALPHA * g) * (u + 1.0)
        y = jnp.dot(a.astype(BF16), wd_bf[...], preferred_element_type=F32) + bd_ref[...]
        _store_row_tiles(o_ref, _pack_pairs(y[:, 0:half], y[:, half:2 * half]))

    @pl.when(jnp.logical_not(live))
    def _():
        o_ref[...] = jnp.zeros_like(o_ref)


def expert_ffn(x_sorted, blk, n_blocks, layer, w_gate, b_gate, w_up, b_up, w_down, b_down):
    depth, ne, d, f = w_gate.shape
    blk_rows = MOE_ROWS * ROW_TILE
    bias = lambda n: pl.BlockSpec((None, 1, n), lambda i, be, nb, nx, sl: (layer * ne + be[i], 0, 0))
    hbm = pl.BlockSpec(memory_space=pl.ANY)
    grid_spec = pltpu.PrefetchScalarGridSpec(
        num_scalar_prefetch=4,
        grid=(n_blocks,),
        in_specs=[pl.BlockSpec((blk_rows, LANES), lambda i, be, nb, nx, sl: (jnp.minimum(i, nb[0] - 1), 0)),
                  bias(f), bias(f), bias(d), hbm, hbm, hbm],
        out_specs=pl.BlockSpec((blk_rows, LANES), lambda i, be, nb, nx, sl: (i, 0)),
        scratch_shapes=[pltpu.VMEM((2, d, f), F32), pltpu.VMEM((2, d, f), F32), pltpu.VMEM((2, f, d), F32),
                        pltpu.VMEM((d, f), BF16), pltpu.VMEM((d, f), BF16), pltpu.VMEM((f, d), BF16),
                        pltpu.SemaphoreType.DMA((2, 3))],
    )
    return pl.pallas_call(
        functools.partial(_expert_kernel, layer=layer),
        grid_spec=grid_spec,
        out_shape=jax.ShapeDtypeStruct(x_sorted.shape, jnp.uint32),
        compiler_params=_cparams("arbitrary"),
        name="expert_ffn",
    )(blk[0, :n_blocks], blk[1, :1], blk[2, :n_blocks], blk[3, :n_blocks], x_sorted,
      b_gate.reshape(depth * ne, 1, f), b_up.reshape(depth * ne, 1, f), b_down.reshape(depth * ne, 1, d),
      w_gate, w_up, w_down)


def _combine_kernel(dest_ref, x_ref, w_ref, gate_ref, y_hbm, o_ref, buf, sem, *, tm, nt, fg_ref):
    i = pl.program_id(0)
    n = pl.num_programs(0)
    slot = i % 2
    half = ROW_TILE * LANES

    def fetch(tile, s):
        def issue(r, carry):
            for k in range(TOP_K):
                d = dest_ref[k * nt + tile * tm + r]
                src = y_hbm.at[pl.ds(pl.multiple_of(d * ROW_TILE, ROW_TILE), ROW_TILE)]
                dst = buf.at[s, k, pl.ds(pl.multiple_of(r * ROW_TILE, ROW_TILE), ROW_TILE)]
                pltpu.make_async_copy(src, dst, sem.at[s]).start(priority=k % 2)
            return carry
        lax.fori_loop(0, tm, issue, 0, unroll=4)

    @pl.when(i == 0)
    def _():
        fetch(0, 0)

    @pl.when(i + 1 < n)
    def _():
        fetch(i + 1, 1 - slot)

    for k in range(TOP_K):
        pltpu.make_async_copy(y_hbm.at[pl.ds(0, tm * ROW_TILE)], buf.at[slot, k], sem.at[slot]).wait()

    ws = [w_ref[:, k:k + 1] for k in range(TOP_K)]
    for j in range(ROW_TILE):
        acc_lo = None
        acc_hi = None
        for k in range(TOP_K):
            lo, hi = _unpack_pairs(buf[slot, k, pl.ds(j, tm, stride=ROW_TILE), :])
            acc_lo = ws[k] * lo if acc_lo is None else acc_lo + ws[k] * lo
            acc_hi = ws[k] * hi if acc_hi is None else acc_hi + ws[k] * hi
        c_lo = slice(j * LANES, (j + 1) * LANES)
        c_hi = slice(half + j * LANES, half + (j + 1) * LANES)
        o_ref[:, c_lo] = x_ref[:, c_lo] + gate_ref[:, c_lo] * acc_lo
        o_ref[:, c_hi] = x_ref[:, c_hi] + gate_ref[:, c_hi] * acc_hi
    if fg_ref is not None:
        y = o_ref[...]
        o_ref[...] = y * lax.rsqrt(jnp.mean(y * y, axis=-1, keepdims=True) + EPS) * fg_ref[...]


def _combine_final_kernel(dest_ref, x_ref, w_ref, gate_ref, fg_ref, y_hbm, o_ref, buf, sem, **kw):
    _combine_kernel(dest_ref, x_ref, w_ref, gate_ref, y_hbm, o_ref, buf, sem, fg_ref=fg_ref, **kw)


def moe_combine(st, x, y_sorted, dest1, w_tok, mod, k_gate, final_g=None):
    d = x.shape[1]
    assert d // 2 == ROW_TILE * LANES
    tm = _pick(st.tm, (256, 128, 64, 32, 16, 8))
    lat_only = final_g is not None
    n_tiles, row_blk, mod_row = st.tiles(tm, lat_only)
    nt = n_tiles * tm
    in_specs = [pl.BlockSpec(memory_space=pltpu.SMEM),
                pl.BlockSpec((tm, d), lambda i: (row_blk(i), 0)),
                pl.BlockSpec((tm, TOP_K), lambda i: (i, 0)),
                pl.BlockSpec((None, None, 1, d), lambda i: (k_gate, mod_row(i), 0, 0))]
    args = [dest1, x, w_tok, mod]
    if lat_only:
        in_specs.append(pl.BlockSpec((1, d), lambda i: (0, 0)))
        args.append(final_g.reshape(1, d))
    in_specs.append(pl.BlockSpec(memory_space=pl.ANY))
    args.append(y_sorted)
    body = _combine_final_kernel if lat_only else functools.partial(_combine_kernel, fg_ref=None)
    return pl.pallas_call(
        functools.partial(body, tm=tm, nt=nt),
        grid=(n_tiles,),
        in_specs=in_specs,
        out_specs=pl.BlockSpec((tm, d), lambda i: (i, 0)),
        out_shape=jax.ShapeDtypeStruct((nt, d), F32),
        scratch_shapes=[pltpu.VMEM((2, TOP_K, tm * ROW_TILE, LANES), jnp.uint32), pltpu.SemaphoreType.DMA((2,))],
        input_output_aliases={} if lat_only else {1: 0},
        compiler_params=_cparams("arbitrary"),
        name="moe_combine",
    )(*args)


def moe_layer(st, x, g, mod, layer, w_router, b_router, w_gate, b_gate, w_up, b_up, w_down, b_down, final_g=None):
    ne = w_router.shape[1]
    tb = MOE_ROWS
    h_packed, idx8, wgt8 = norm_router(st, x, g, mod, 3, 4, w_router, b_router, final_g is not None)
    nt = idx8.shape[1]
    n_assign = nt * TOP_K
    cap = -(-(n_assign + ne * (tb - 1)) // tb) * tb
    n_blocks = cap // tb
    dest2, blk, ends = route_plan(idx8, ne, tb, n_blocks)
    dest1 = dest2.reshape(-1)
    x_sorted = moe_dispatch(h_packed, dest1, ends, cap, ne)
    y_sorted = expert_ffn(x_sorted, blk, n_blocks, layer, w_gate, b_gate, w_up, b_up, w_down, b_down)
    return moe_combine(st, x, y_sorted, dest1, wgt8[:TOP_K].T, mod, 5, final_g)


def even_mixer(st, x, norm_g, mod, i, w_in, w_out, s5p, d_skip, glu_w, glu_b, fnet_w, tables):
    d = x.shape[1]
    fw = fnet_w.shape[0] * fnet_w.shape[1]
    sw = d - fw
    p = norm_matmul(st, x, norm_g, mod, 0, 1, w_in, i)
    b_bd, c_bd, lam8 = s5p
    y_f, y_b = s5_scan(st, p, sw, b_bd, c_bd, lam8)
    a = gelu_glu(st, y_f, y_b, p, d_skip, glu_w, i, glu_b)
    cd, sd, tab_s, tab_l = tables
    pcs = fnet_chan(p, sw // fw, fnet_w, cd, sd)
    fm = fnet_seq(st, pcs, tab_l, tab_s)
    return matmul([a, fm], w_out, i, mode="resid", extra=(x, mod, 2), st=st)


def odd_mixer(st, x, norm_g, mod, i, w_in, w_out, pool_w, pool_scale, q_gain, k_gain, rope):
    d = x.shape[1]
    pw = pool_w.shape[0] * pool_w.shape[1]
    n_q = (d - pw) // HEAD_DIM
    p = norm_matmul(st, x, norm_g, mod, 0, 1, w_in, i)
    cos_t, sin_t = rope
    gains = jnp.stack([q_gain, k_gain]).reshape(2, 1, HEAD_DIM)
    att = attention(st, p, gains, cos_t, sin_t, pw // HEAD_DIM, n_q, HEAD_DIM ** -0.5 * math.log2(math.e))
    pm = pool_mix(st, p, pool_w, pool_scale)
    return matmul([pm, att], w_out, i, mode="resid", extra=(x, mod, 2), st=st)


def kernel(x, c, ctx, c_ctx, ada_w, ada_b, norm_mix_g, norm_ffn_g, ev_w_in, ev_w_out, s5_lam_re, s5_lam_im, s5_log_dt, s5_b_re, s5_b_im, s5_c_re, s5_c_im, s5_d, s5_glu_w, s5_glu_b, fnet_w, od_w_in, od_w_out, pool_w, pool_scale, q_gain, k_gain, router_w, router_b, exp_w_gate, exp_b_gate, exp_w_up, exp_b_up, exp_w_down, exp_b_down, final_g):
    batch, seq, d = x.shape
    ctx_len = ctx.shape[1]
    depth = ada_w.shape[0]
    st = Stream(batch, seq, ctx_len)
    xs = jnp.concatenate([ctx, x], axis=1).reshape(st.nt, d)

    cond8 = jnp.concatenate([c, c_ctx[None], jnp.zeros((8 - batch - 1, d), F32)], axis=0)
    mod_all = adaln_all(cond8, ada_w, ada_b)
    mod_all = mod_all.reshape(depth, 8, 6, 1, d).transpose(0, 2, 1, 3, 4)

    dh = fnet_w.shape[2]
    cd, sd = dft_tables(dh)
    tab_s = tuple(t.astype(BF16) for t in dft_tables(seq))
    tab_l = tuple(t.astype(BF16) for t in dft_tables(ctx_len))
    tables = (cd, sd, tab_s, tab_l)
    rope = rope_tables(seq, ctx_len)

    for layer in range(depth):
        i = layer // 2
        mod = mod_all[layer]
        if layer % 2 == 0:
            s5p = s5_tables(s5_lam_re[i], s5_lam_im[i], s5_log_dt[i], s5_b_re[i], s5_b_im[i], s5_c_re[i], s5_c_im[i])
            xs = even_mixer(st, xs, norm_mix_g[layer], mod, i, ev_w_in, ev_w_out, s5p, s5_d[i], s5_glu_w,
                            s5_glu_b[i], fnet_w[i], tables)
        else:
            xs = odd_mixer(st, xs, norm_mix_g[layer], mod, i, od_w_in, od_w_out, pool_w[i], pool_scale[i],
                           q_gain[i], k_gain[i], rope)
        xs = moe_layer(st, xs, norm_ffn_g[layer], mod, layer, router_w[layer], router_b[layer],
                       exp_w_gate, exp_b_gate, exp_w_up, exp_b_up, exp_w_down, exp_b_down,
                       final_g if layer == depth - 1 else None)
    return xs.reshape(batch, seq, d)
```
